```python
import math
import jax, jax.numpy as jnp
from jax import lax
import numpy as np

D_MODEL = 1024
BATCH = 8
SEQ = 2048
DEPTH = 2

D_MIX = D_MODEL
SSD_INNER = D_MIX // 2
SSD_HEAD_DIM = 64
SSD_HEADS = SSD_INNER // SSD_HEAD_DIM
SSD_GROUPS = 2
SSD_STATE = 128
SSD_CONV = 4
SSD_CHUNK = 128
SSD_XBC = SSD_INNER + 2 * SSD_GROUPS * SSD_STATE
POOL_WIDTH = D_MIX // 4
POOL_GROUPS = 4
POOL_GROUP_DIM = POOL_WIDTH // POOL_GROUPS
ATTN_WIDTH = D_MIX - SSD_INNER - POOL_WIDTH
ATTN_HEAD_DIM = 64
ATTN_HEADS = ATTN_WIDTH // ATTN_HEAD_DIM
ROPE_DIM = ATTN_HEAD_DIM // 4
ROPE_THETA = 500000.0
MOBA_BLOCK = 256
MOBA_TOPK = 3
MOBA_QCHUNK = 64
D_FF = 2816
RMS_EPS = 1e-6

OFF_Z = 0
OFF_XBC = OFF_Z + SSD_INNER
OFF_DT = OFF_XBC + SSD_XBC
OFF_POOL = OFF_DT + SSD_HEADS
OFF_Q = OFF_POOL + POOL_WIDTH
OFF_K = OFF_Q + ATTN_WIDTH
OFF_V = OFF_K + ATTN_WIDTH
IN_COLS = OFF_V + ATTN_WIDTH

kernel_name = "hymba_ssd_pool_moba_macaron"


def _rmsnorm(x, g):
    x32 = x.astype(jnp.float32)
    y = x32 * lax.rsqrt(jnp.mean(x32 * x32, axis=-1, keepdims=True) + RMS_EPS)
    return y.astype(x.dtype) * g


def _swiglu(h, wg, wu, wd):
    return (jax.nn.silu(h @ wg) * (h @ wu)) @ wd


def _partial_rope(t, pos):
    half = ROPE_DIM // 2
    inv_freq = ROPE_THETA ** (-jnp.arange(0, ROPE_DIM, 2, dtype=jnp.float32) / ROPE_DIM)
    ang = pos[:, None] * inv_freq[None, :]
    cos = jnp.cos(ang)[None, :, None, :]
    sin = jnp.sin(ang)[None, :, None, :]
    t32 = t[..., :ROPE_DIM].astype(jnp.float32)
    x1, x2 = t32[..., :half], t32[..., half:]
    rot = jnp.concatenate([x1 * cos - x2 * sin, x2 * cos + x1 * sin], axis=-1).astype(t.dtype)
    return jnp.concatenate([rot, t[..., ROPE_DIM:]], axis=-1)


def _ssd_chunked(xdt, dA, Bm, Cm):
    b, S, H, P = xdt.shape
    G = SSD_GROUPS
    HG = H // G
    N = Bm.shape[-1]
    L = SSD_CHUNK
    nc = S // L
    x = xdt.reshape(b, nc, L, G, HG, P)
    a = dA.reshape(b, nc, L, G, HG).transpose(0, 3, 4, 1, 2)
    Bc = Bm.reshape(b, nc, L, G, N)
    Cc = Cm.reshape(b, nc, L, G, N)
    a_cs = jnp.cumsum(a, axis=-1)
    causal = jnp.tril(jnp.ones((L, L), dtype=bool))
    seg = jnp.exp(jnp.where(causal, a_cs[..., :, None] - a_cs[..., None, :], -jnp.inf))
    cb = jnp.einsum('bclgn,bcsgn->bgcls', Cc, Bc)
    y_diag = jnp.einsum('bgcls,bghcls,bcsghp->bclghp', cb, seg, x)
    decay_in = jnp.exp(a_cs[..., -1:] - a_cs)
    states = jnp.einsum('bclgn,bghcl,bclghp->bcghpn', Bc, decay_in, x)
    chunk_decay = jnp.exp(a_cs[..., -1])

    def step(h, inp):
        st, dec = inp
        return h * dec[..., None, None] + st, h

    h0 = jnp.zeros(states.shape[:1] + states.shape[2:], states.dtype)
    _, prev = lax.scan(step, h0, (jnp.moveaxis(states, 1, 0), jnp.moveaxis(chunk_decay, -1, 0)))
    prev = jnp.moveaxis(prev, 0, 1)
    y_off = jnp.einsum('bclgn,bcghpn,bghcl->bclghp', Cc, prev, jnp.exp(a_cs))
    return (y_diag + y_off).reshape(b, S, H, P)


def _ssd_mixer(z, xbc, dt_raw, conv_w, conv_b, dt_bias, a_log, d_skip, norm_g):
    b, S, C = xbc.shape
    xbc = lax.conv_general_dilated(
        xbc, conv_w[:, None, :], window_strides=(1,), padding=[(SSD_CONV - 1, 0)],
        dimension_numbers=('NWC', 'WIO', 'NWC'), feature_group_count=C)
    xbc = jax.nn.silu(xbc + conv_b)
    gn = SSD_GROUPS * SSD_STATE
    xs = xbc[..., :SSD_INNER].reshape(b, S, SSD_HEADS, SSD_HEAD_DIM)
    Bm = xbc[..., SSD_INNER:SSD_INNER + gn].reshape(b, S, SSD_GROUPS, SSD_STATE)
    Cm = xbc[..., SSD_INNER + gn:].reshape(b, S, SSD_GROUPS, SSD_STATE)
    dt = jax.nn.softplus((dt_raw + dt_bias).astype(jnp.float32))
    A = -jnp.exp(a_log.astype(jnp.float32))
    y = _ssd_chunked(xs * dt[..., None], dt * A, Bm, Cm)
    y = y + d_skip[:, None] * xs
    y = y.reshape(b, S, SSD_INNER).astype(z.dtype)
    return _rmsnorm(y * jax.nn.silu(z), norm_g)


def _pool_mixer(u, pool_w, pool_scale):
    S = u.shape[1]
    c = jnp.cumsum(u.astype(jnp.float32), axis=1)
    t = jnp.arange(1, S + 1, dtype=jnp.float32)
    outs = []
    for g in range(POOL_GROUPS):
        w = 2 ** (g + 1)
        sl = slice(g * POOL_GROUP_DIM, (g + 1) * POOL_GROUP_DIM)
        cg = c[..., sl]
        shifted = jnp.pad(cg, ((0, 0), (w, 0), (0, 0)))[:, :S]
        mean = (cg - shifted) / jnp.minimum(t, float(w))[None, :, None]
        outs.append(jnp.einsum('bsc,cd->bsd', (mean - u[..., sl]).astype(u.dtype), pool_w[g]))
    return jnp.concatenate(outs, axis=-1) * pool_scale


def _moba_attention(q, k, v):
    b, S, H, Dh = q.shape
    q = q.transpose(0, 2, 1, 3)
    k = k.transpose(0, 2, 1, 3)
    v = v.transpose(0, 2, 1, 3)
    nb = -(-S // MOBA_BLOCK)
    sp = nb * MOBA_BLOCK
    topk = min(MOBA_TOPK, nb)
    pad = ((0, 0), (0, 0), (0, sp - S), (0, 0))
    k_pad = jnp.pad(k, pad)
    v_pad = jnp.pad(v, pad)
    kb = k_pad.reshape(b, H, nb, MOBA_BLOCK, Dh)
    vb = v_pad.reshape(b, H, nb, MOBA_BLOCK, Dh)
    k_mean = jnp.mean(kb.astype(jnp.float32), axis=3)
    gate = jnp.einsum('bhsd,bhnd->bhsn', q.astype(jnp.float32), k_mean)
    q_blk = jnp.arange(S) // MOBA_BLOCK
    past = jnp.arange(nb)[None, :] < q_blk[:, None]
    gate = jnp.where(past, gate, -jnp.inf)
    _, sel = lax.top_k(gate, topk)
    sel_valid = sel < q_blk[None, None, :, None]
    nq = S // MOBA_QCHUNK

    def chunks(t):
        return jnp.moveaxis(t.reshape((b, H, nq, MOBA_QCHUNK) + t.shape[3:]), 2, 0)

    bidx = jnp.arange(b)[:, None, None, None]
    hidx = jnp.arange(H)[None, :, None, None]
    scale = Dh ** -0.5

    def attend(args):
        qi, si, vi, ci = args
        start = ci * MOBA_QCHUNK
        q_pos = start + jnp.arange(MOBA_QCHUNK)
        kg = kb[bidx, hidx, si]
        vg = vb[bidx, hidx, si]
        s_sel = jnp.einsum('bhqd,bhqkjd->bhqkj', qi, kg).astype(jnp.float32) * scale
        s_sel = jnp.where(vi[..., None], s_sel, -jnp.inf).reshape(b, H, MOBA_QCHUNK, topk * MOBA_BLOCK)
        own = (start // MOBA_BLOCK) * MOBA_BLOCK
        ko = lax.dynamic_slice_in_dim(k_pad, own, MOBA_BLOCK, axis=2)
        vo = lax.dynamic_slice_in_dim(v_pad, own, MOBA_BLOCK, axis=2)
        s_own = jnp.einsum('bhqd,bhjd->bhqj', qi, ko).astype(jnp.float32) * scale
        k_pos = own + jnp.arange(MOBA_BLOCK)
        s_own = jnp.where(k_pos[None, :] <= q_pos[:, None], s_own, -jnp.inf)
        p = jax.nn.softmax(jnp.concatenate([s_sel, s_own], axis=-1), axis=-1).astype(v.dtype)
        p_sel = p[..., :topk * MOBA_BLOCK].reshape(b, H, MOBA_QCHUNK, topk, MOBA_BLOCK)
        return (jnp.einsum('bhqkj,bhqkjd->bhqd', p_sel, vg)
                + jnp.einsum('bhqj,bhjd->bhqd', p[..., topk * MOBA_BLOCK:], vo))

    out = lax.map(attend, (chunks(q), chunks(sel), chunks(sel_valid), jnp.arange(nq)))
    out = jnp.moveaxis(out, 0, 2).reshape(b, H, S, Dh)
    return out.transpose(0, 2, 1, 3).reshape(b, S, H * Dh)


def _hybrid_mixer(h, w_in, conv_w, conv_b, dt_bias, a_log, d_skip, ssd_norm, pool_w, pool_scale, w_out):
    b, S, _ = h.shape
    proj = h @ w_in
    z = proj[..., OFF_Z:OFF_XBC]
    xbc = proj[..., OFF_XBC:OFF_DT]
    dt_raw = proj[..., OFF_DT:OFF_POOL]
    u = proj[..., OFF_POOL:OFF_Q]
    pos = jnp.arange(S, dtype=jnp.float32)
    q = _partial_rope(proj[..., OFF_Q:OFF_K].reshape(b, S, ATTN_HEADS, ATTN_HEAD_DIM), pos)
    k = _partial_rope(proj[..., OFF_K:OFF_V].reshape(b, S, ATTN_HEADS, ATTN_HEAD_DIM), pos)
    v = proj[..., OFF_V:IN_COLS].reshape(b, S, ATTN_HEADS, ATTN_HEAD_DIM)
    y_ssd = _ssd_mixer(z, xbc, dt_raw, conv_w, conv_b, dt_bias, a_log, d_skip, ssd_norm)
    y_pool = _pool_mixer(u, pool_w, pool_scale).astype(h.dtype)
    y_attn = _moba_attention(q, k, v)
    return jnp.concatenate([y_ssd, y_pool, y_attn], axis=-1) @ w_out


def setup_inputs(seed: int = 0) -> dict:
    key = jax.random.key(seed)
    ks = jax.random.split(key, 24)
    f32 = jnp.float32

    def nrm(k, shape, fan_in):
        return jax.random.normal(k, shape, f32) * fan_in ** -0.5

    def gain(k, shape):
        return 1.0 + 0.05 * jax.random.normal(k, shape, f32)

    u_dt = jax.random.uniform(ks[9], (DEPTH, SSD_HEADS), f32)
    dt0 = jnp.exp(u_dt * (math.log(0.1) - math.log(0.001)) + math.log(0.001))
    dt_bias = dt0 + jnp.log(-jnp.expm1(-dt0))
    a_log = jnp.log(jax.random.uniform(ks[10], (DEPTH, SSD_HEADS), f32, 1.0, 16.0))
    return {
        "x": jax.random.normal(ks[0], (BATCH, SEQ, D_MODEL), f32),
        "ff1_norm_pre": gain(ks[1], (DEPTH, D_MODEL)),
        "ff1_w_gate": nrm(ks[2], (DEPTH, D_MODEL, D_FF), D_MODEL),
        "ff1_w_up": nrm(ks[3], (DEPTH, D_MODEL, D_FF), D_MODEL),
        "ff1_w_down": nrm(ks[4], (DEPTH, D_FF, D_MODEL), D_FF),
        "ff1_norm_post": gain(ks[5], (DEPTH, D_MODEL)),
        "mix_norm_pre": gain(ks[6], (DEPTH, D_MODEL)),
        "w_in": nrm(ks[7], (DEPTH, D_MODEL, IN_COLS), D_MODEL),
        "conv_w": nrm(ks[8], (DEPTH, SSD_CONV, SSD_XBC), SSD_CONV),
        "conv_b": 0.01 * jax.random.normal(ks[11], (DEPTH, SSD_XBC), f32),
        "dt_bias": dt_bias,
        "a_log": a_log,
        "d_skip": gain(ks[12], (DEPTH, SSD_HEADS)),
        "ssd_norm": gain(ks[13], (DEPTH, SSD_INNER)),
        "pool_w": nrm(ks[14], (DEPTH, POOL_GROUPS, POOL_GROUP_DIM, POOL_GROUP_DIM), POOL_GROUP_DIM),
        "pool_scale": gain(ks[15], (DEPTH, POOL_WIDTH)),
        "w_out": nrm(ks[16], (DEPTH, D_MIX, D_MODEL), D_MIX),
        "mix_norm_post": gain(ks[17], (DEPTH, D_MODEL)),
        "ff2_norm_pre": gain(ks[18], (DEPTH, D_MODEL)),
        "ff2_w_gate": nrm(ks[19], (DEPTH, D_MODEL, D_FF), D_MODEL),
        "ff2_w_up": nrm(ks[20], (DEPTH, D_MODEL, D_FF), D_MODEL),
        "ff2_w_down": nrm(ks[21], (DEPTH, D_FF, D_MODEL), D_FF),
        "ff2_norm_post": gain(ks[22], (DEPTH, D_MODEL)),
    }


def reference(x, ff1_norm_pre, ff1_w_gate, ff1_w_up, ff1_w_down, ff1_norm_post,
              mix_norm_pre, w_in, conv_w, conv_b, dt_bias, a_log, d_skip, ssd_norm,
              pool_w, pool_scale, w_out, mix_norm_post,
              ff2_norm_pre, ff2_w_gate, ff2_w_up, ff2_w_down, ff2_norm_post):
    h = x
    for l in range(DEPTH):
        f = _swiglu(_rmsnorm(h, ff1_norm_pre[l]), ff1_w_gate[l], ff1_w_up[l], ff1_w_down[l])
        h = h + 0.5 * _rmsnorm(f, ff1_norm_post[l])
        m = _hybrid_mixer(_rmsnorm(h, mix_norm_pre[l]), w_in[l], conv_w[l], conv_b[l], dt_bias[l],
                          a_log[l], d_skip[l], ssd_norm[l], pool_w[l], pool_scale[l], w_out[l])
        h = h + _rmsnorm(m, mix_norm_post[l])
        f = _swiglu(_rmsnorm(h, ff2_norm_pre[l]), ff2_w_gate[l], ff2_w_up[l], ff2_w_down[l])
        h = h + 0.5 * _rmsnorm(f, ff2_norm_post[l])
    return h
```

```python
import functools
import math

import jax
import jax.numpy as jnp
from jax import lax
from jax.experimental import pallas as pl
from jax.experimental.pallas import tpu as pltpu

F32 = jnp.float32
BF16 = jnp.bfloat16
HIGHEST = lax.Precision.HIGHEST

D_MODEL = 1024
D_FF = 2816
SSD_INNER = 512
SSD_HEAD_DIM = 64
SSD_HEADS = SSD_INNER // SSD_HEAD_DIM
SSD_GROUPS = 2
SSD_STATE = 128
SSD_CONV = 4
SSD_CHUNK = 128
SSD_BC = SSD_GROUPS * SSD_STATE
SSD_XBC = SSD_INNER + 2 * SSD_BC
POOL_WIDTH = 256
POOL_GROUPS = 4
POOL_GROUP_DIM = POOL_WIDTH // POOL_GROUPS
ATTN_WIDTH = 256
ATTN_HEAD_DIM = 64
ATTN_HEADS = ATTN_WIDTH // ATTN_HEAD_DIM
ROPE_DIM = ATTN_HEAD_DIM // 4
ROPE_THETA = 500000.0
MOBA_BLOCK = 256
MOBA_TOPK = 3
RMS_EPS = 1e-6

OFF_Z = 0
OFF_XBC = OFF_Z + SSD_INNER
OFF_DT = OFF_XBC + SSD_XBC
OFF_POOL = OFF_DT + SSD_HEADS
OFF_Q = OFF_POOL + POOL_WIDTH
OFF_K = OFF_Q + ATTN_WIDTH
OFF_V = OFF_K + ATTN_WIDTH
IN_COLS = OFF_V + ATTN_WIDTH

LANES = 128
SUBLANES = 8
MXU_DIM = 256
VMEM_LIMIT_BYTES = 56 * 1024 * 1024

DT_PAD = LANES
TOKEN_TILE = 512
FF_CHUNK = 2 * MXU_DIM


def _rms(x):
    return x * lax.rsqrt(jnp.mean(x * x, axis=-1, keepdims=True) + RMS_EPS)


def _silu(x):
    return x * jax.nn.sigmoid(x)


def _softplus(x):
    return jnp.maximum(x, 0.0) + jnp.log1p(jnp.exp(-jnp.abs(x)))


def _const_spec(shape):
    zeros = (0,) * len(shape)
    return pl.BlockSpec(shape, lambda *_: zeros, pipeline_mode=pl.Buffered(1))


def _params(*sem):
    return pltpu.CompilerParams(dimension_semantics=sem,
                                vmem_limit_bytes=VMEM_LIMIT_BYTES)


def _ffn_body(x_ref, gpre_ref, wg_ref, wu_ref, wd_ref, gpost_ref, o_ref, h_ref):
    x = x_ref[...]
    xb = (_rms(x) * gpre_ref[...]).astype(BF16)
    for c in range(0, D_FF, FF_CHUNK):
        sl = slice(c, min(c + FF_CHUNK, D_FF))
        g = jnp.dot(xb, wg_ref[:, sl], preferred_element_type=F32)
        u = jnp.dot(xb, wu_ref[:, sl], preferred_element_type=F32)
        h_ref[:, sl] = (_silu(g) * u).astype(BF16)
    f = jnp.dot(h_ref[...], wd_ref[...], preferred_element_type=F32)
    o_ref[...] = x + 0.5 * (_rms(f) * gpost_ref[...])


def _ffn(x, gpre, wg, wu, wd, gpost):
    m = x.shape[0]
    row = lambda i: (i, 0)
    return pl.pallas_call(
        _ffn_body,
        grid=(m // TOKEN_TILE,),
        in_specs=[
            pl.BlockSpec((TOKEN_TILE, D_MODEL), row),
            _const_spec((1, D_MODEL)),
            _const_spec((D_MODEL, D_FF)),
            _const_spec((D_MODEL, D_FF)),
            _const_spec((D_FF, D_MODEL)),
            _const_spec((1, D_MODEL)),
        ],
        out_specs=pl.BlockSpec((TOKEN_TILE, D_MODEL), row),
        out_shape=jax.ShapeDtypeStruct((m, D_MODEL), F32),
        scratch_shapes=[pltpu.VMEM((TOKEN_TILE, D_FF), BF16)],
        compiler_params=_params("parallel"),
        name="ffn",
    )(x, gpre, wg, wu, wd, gpost)


_PROJ_WIDTHS = (SSD_INNER, SSD_XBC, POOL_WIDTH, ATTN_WIDTH, ATTN_WIDTH,
                ATTN_WIDTH, DT_PAD)
_PROJ_COLS = sum(_PROJ_WIDTHS)


def _inproj_body(x_ref, g_ref, w_ref, *o_refs):
    xb = (_rms(x_ref[...]) * g_ref[...]).astype(BF16)
    off = 0
    for o_ref, width in zip(o_refs, _PROJ_WIDTHS):
        o_ref[...] = jnp.dot(xb, w_ref[:, off:off + width],
                             preferred_element_type=F32)
        off += width


def _inproj(x, g, w):
    m = x.shape[0]
    row = lambda i: (i, 0)
    return pl.pallas_call(
        _inproj_body,
        grid=(m // TOKEN_TILE,),
        in_specs=[
            pl.BlockSpec((TOKEN_TILE, D_MODEL), row),
            _const_spec((1, D_MODEL)),
            _const_spec((D_MODEL, _PROJ_COLS)),
        ],
        out_specs=[pl.BlockSpec((TOKEN_TILE, width), row)
                   for width in _PROJ_WIDTHS],
        out_shape=[jax.ShapeDtypeStruct((m, width), F32)
                   for width in _PROJ_WIDTHS],
        compiler_params=_params("parallel"),
        name="inproj",
    )(x, g, w)


def _ssd_body(xbc_ref, z_ref, dt_ref, convw_ref, convb_ref, dtb_ref, alog_ref,
              dskip_ref, gn_ref, o_ref, xpad_ref, state_ref):
    L = SSD_CHUNK
    HP = SSD_INNER
    P = SSD_HEAD_DIM
    N = SSD_STATE
    GW = HP // SSD_GROUPS
    HPG = SSD_HEADS // SSD_GROUPS

    @pl.when(pl.program_id(1) == 0)
    def _():
        xpad_ref[0:SUBLANES, :] = jnp.zeros((SUBLANES, SSD_XBC), F32)
        state_ref[...] = jnp.zeros_like(state_ref)

    xpad_ref[SUBLANES:SUBLANES + L, :] = xbc_ref[...]
    acc = jnp.broadcast_to(convb_ref[...], (L, SSD_XBC))
    for j in range(SSD_CONV):
        start = SUBLANES - (SSD_CONV - 1) + j
        acc = acc + convw_ref[j:j + 1, :] * xpad_ref[start:start + L, :]
    xpad_ref[0:SUBLANES, :] = xpad_ref[L:L + SUBLANES, :]
    xc = _silu(acc)
    xs = xc[:, :HP]
    bm = xc[:, HP:HP + SSD_BC].astype(BF16)
    cm = xc[:, HP + SSD_BC:].astype(BF16)

    dtraw = dt_ref[...]
    head_of_lane = lax.broadcasted_iota(jnp.int32, (L, HP), 1) // P
    dtx = jnp.zeros((L, HP), F32)
    for h in range(SSD_HEADS):
        dtx = jnp.where(head_of_lane == h, dtraw[:, h:h + 1], dtx)
    dt = _softplus(dtx + dtb_ref[...])
    da = dt * (-jnp.exp(alog_ref[...]))

    r = lax.broadcasted_iota(jnp.int32, (L, L), 0)
    s = lax.broadcasted_iota(jnp.int32, (L, L), 1)
    causal = s <= r
    tril = causal.astype(F32)
    acs = jnp.dot(tril, da, precision=HIGHEST, preferred_element_type=F32)
    a_last = acs[L - 1:L, :]
    decay_out = jnp.exp(acs)
    decay_in = jnp.exp(a_last - acs)
    chunk_decay = jnp.exp(a_last)

    xdt = xs * dt
    xdt_b = xdt.astype(BF16)
    xw_b = (xdt * decay_in).astype(BF16)

    y_parts = []
    for g in range(SSD_GROUPS):
        bg = bm[:, g * N:(g + 1) * N]
        cg = cm[:, g * N:(g + 1) * N]
        gsl = slice(g * GW, (g + 1) * GW)
        cb = lax.dot_general(cg, bg, (((1,), (1,)), ((), ())),
                             preferred_element_type=F32)
        st = state_ref[:, gsl]
        y_off = jnp.dot(cg, st.astype(BF16), preferred_element_type=F32)
        new_st = lax.dot_general(bg, xw_b[:, gsl], (((0,), (0,)), ((), ())),
                                 preferred_element_type=F32)
        state_ref[:, gsl] = st * chunk_decay[:, gsl] + new_st
        for pair in range(HPG // 2):
            lo = g * GW + pair * 2 * P
            acs_t = acs[:, lo:lo + 2 * P].T
            x_pair = xdt_b[:, lo:lo + 2 * P]
            ys = []
            for i in range(2):
                col = acs[:, lo + i * P:lo + i * P + 1]
                row = acs_t[i * P:i * P + 1, :]
                seg = jnp.exp(jnp.where(causal, col - row, -jnp.inf))
                ys.append(jnp.dot((cb * seg).astype(BF16), x_pair,
                                  preferred_element_type=F32))
            lane = lax.broadcasted_iota(jnp.int32, (L, 2 * P), 1)
            y_diag = jnp.where(lane < P, ys[0], ys[1])
            psl = slice(pair * 2 * P, (pair + 1) * 2 * P)
            y_parts.append(y_diag + y_off[:, psl] * decay_out[:, lo:lo + 2 * P])
    y = jnp.concatenate(y_parts, axis=-1) + dskip_ref[...] * xs
    o_ref[...] = _rms(y * _silu(z_ref[...])) * gn_ref[...]


def _ssd(xbc, z, dt, convw, convb, dtb, alog, dskip, gn, batch, seq):
    nc = seq // SSD_CHUNK
    row = lambda b, c: (b * nc + c, 0)
    return pl.pallas_call(
        _ssd_body,
        grid=(batch, nc),
        in_specs=[
            pl.BlockSpec((SSD_CHUNK, SSD_XBC), row),
            pl.BlockSpec((SSD_CHUNK, SSD_INNER), row),
            pl.BlockSpec((SSD_CHUNK, DT_PAD), row),
            _const_spec((SSD_CONV, SSD_XBC)),
            _const_spec((1, SSD_XBC)),
            _const_spec((1, SSD_INNER)),
            _const_spec((1, SSD_INNER)),
            _const_spec((1, SSD_INNER)),
            _const_spec((1, SSD_INNER)),
        ],
        out_specs=pl.BlockSpec((SSD_CHUNK, SSD_INNER), row),
        out_shape=jax.ShapeDtypeStruct((batch * seq, SSD_INNER), F32),
        scratch_shapes=[
            pltpu.VMEM((SUBLANES + SSD_CHUNK, SSD_XBC), F32),
            pltpu.VMEM((SSD_STATE, SSD_INNER), F32),
        ],
        compiler_params=_params("parallel", "arbitrary"),
        name="ssd",
    )(xbc, z, dt, convw, convb, dtb, alog, dskip, gn)


def _pool_body(u_ref, w_ref, scale_ref, o_ref):
    u = u_ref[...]
    seq = u.shape[0]
    t = lax.broadcasted_iota(jnp.int32, u.shape, 0)
    group = lax.broadcasted_iota(jnp.int32, u.shape, 1) // POOL_GROUP_DIM
    win_sum = u
    mean = jnp.zeros_like(u)
    for g in range(POOL_GROUPS):
        half = 2 ** g
        shifted = jnp.where(t >= half, pltpu.roll(win_sum, half, 0), 0.0)
        win_sum = win_sum + shifted
        count = jnp.minimum(t + 1, 2 * half).astype(F32)
        mean = jnp.where(group == g, win_sum / count, mean)
    d = (mean - u).astype(BF16)
    del seq
    o_ref[...] = jnp.dot(d, w_ref[...], preferred_element_type=F32) * scale_ref[...]


def _pool(u, w_blockdiag, scale, batch, seq):
    row = lambda b: (b, 0)
    return pl.pallas_call(
        _pool_body,
        grid=(batch,),
        in_specs=[
            pl.BlockSpec((seq, POOL_WIDTH), row),
            _const_spec((POOL_WIDTH, POOL_WIDTH)),
            _const_spec((1, POOL_WIDTH)),
        ],
        out_specs=pl.BlockSpec((seq, POOL_WIDTH), row),
        out_shape=jax.ShapeDtypeStruct((batch * seq, POOL_WIDTH), F32),
        compiler_params=_params("parallel"),
        name="pool",
    )(u, w_blockdiag, scale)


def _moba_body(q_ref, k_ref, v_ref, cos_ref, sin_ref, swap_ref, o_ref,
               qs_ref, ks_ref, vt_ref, sel_ref):
    seq = k_ref.shape[0]
    nb = seq // MOBA_BLOCK
    Dh = ATTN_HEAD_DIM
    BL = MOBA_BLOCK
    i = pl.program_id(1)
    nt = (((1,), (1,)), ((), ()))

    @pl.when(i == 0)
    def _():
        cos = cos_ref[...]
        sin = sin_ref[...]
        swap = swap_ref[...]
        eye = (lax.broadcasted_iota(jnp.int32, (Dh, Dh), 0) ==
               lax.broadcasted_iota(jnp.int32, (Dh, Dh), 1)).astype(BF16)
        blk = lax.broadcasted_iota(jnp.int32, (nb, seq), 0)
        qblk = lax.broadcasted_iota(jnp.int32, (nb, seq), 1) // BL
        for h in range(ATTN_HEADS):
            hs = slice(h * Dh, (h + 1) * Dh)
            qh = q_ref[:, hs]
            kh = k_ref[:, hs]
            qh = qh * cos + jnp.dot(qh, swap, precision=HIGHEST,
                                    preferred_element_type=F32) * sin
            kh = kh * cos + jnp.dot(kh, swap, precision=HIGHEST,
                                    preferred_element_type=F32) * sin
            qs_ref[h] = qh.astype(BF16)
            ks_ref[h] = kh.astype(BF16)
            vt_ref[h] = lax.dot_general(
                eye, v_ref[:, hs].astype(BF16), nt,
                preferred_element_type=F32).astype(BF16)
            kmean = jnp.concatenate(
                [jnp.mean(kh[j * BL:(j + 1) * BL], axis=0, keepdims=True)
                 for j in range(nb)], axis=0)
            gate = lax.dot_general(kmean, qh, nt, precision=HIGHEST,
                                   preferred_element_type=F32)
            past = blk < qblk
            gate = jnp.where(past, gate, -jnp.inf)
            rank = jnp.zeros((nb, seq), jnp.int32)
            for j2 in range(nb):
                g2 = gate[j2:j2 + 1, :]
                ahead = (g2 > gate) | ((g2 == gate) & (j2 < blk))
                rank = rank + ahead.astype(jnp.int32)
            sel_ref[h] = (past & (rank < MOBA_TOPK)).astype(F32)

    scale = Dh ** -0.5
    kpos = lax.broadcasted_iota(jnp.int32, (BL, BL), 0)
    qpos = lax.broadcasted_iota(jnp.int32, (BL, BL), 1)
    own_mask = kpos <= qpos
    q0 = pl.multiple_of(i * BL, BL)
    outs = []
    for h in range(ATTN_HEADS):
        qh = qs_ref[h, pl.ds(q0, BL), :]

        def scores(j):
            k0 = pl.multiple_of(j * BL, BL)
            kj = ks_ref[h, pl.ds(k0, BL), :]
            st = lax.dot_general(kj, qh, nt, preferred_element_type=F32)
            return st * scale, k0

        st, k0 = scores(i)
        st = jnp.where(own_mask, st, -jnp.inf)
        m = jnp.max(st, axis=0, keepdims=True)
        p = jnp.exp(st - m)
        l = jnp.sum(p, axis=0, keepdims=True)
        acc = jnp.dot(vt_ref[h, :, pl.ds(k0, BL)], p.astype(BF16),
                      preferred_element_type=F32)

        def step(j, carry):
            m, l, acc = carry
            st, k0 = scores(j)
            keep = sel_ref[h, pl.ds(j, 1), pl.ds(q0, BL)] > 0.5
            st = jnp.where(keep, st, -jnp.inf)
            m_new = jnp.maximum(m, jnp.max(st, axis=0, keepdims=True))
            alpha = jnp.exp(m - m_new)
            p = jnp.exp(st - m_new)
            l = alpha * l + jnp.sum(p, axis=0, keepdims=True)
            acc = alpha * acc + jnp.dot(vt_ref[h, :, pl.ds(k0, BL)],
                                        p.astype(BF16),
                                        preferred_element_type=F32)
            return m_new, l, acc

        m, l, acc = lax.fori_loop(0, i, step, (m, l, acc))
        outs.append(acc / l)
    out_t = jnp.concatenate(outs, axis=0)
    o_ref[...] = out_t.T


def _moba(q, k, v, cos, sin, swap, batch, seq):
    nb = seq // MOBA_BLOCK
    full = lambda b, i: (b, 0)
    const = lambda b, i: (0, 0)
    return pl.pallas_call(
        _moba_body,
        grid=(batch, nb),
        in_specs=[
            pl.BlockSpec((seq, ATTN_WIDTH), full),
            pl.BlockSpec((seq, ATTN_WIDTH), full),
            pl.BlockSpec((seq, ATTN_WIDTH), full),
            pl.BlockSpec((seq, ATTN_HEAD_DIM), const),
            pl.BlockSpec((seq, ATTN_HEAD_DIM), const),
            pl.BlockSpec((ATTN_HEAD_DIM, ATTN_HEAD_DIM), const),
        ],
        out_specs=pl.BlockSpec((MOBA_BLOCK, ATTN_WIDTH),
                               lambda b, i: (b * nb + i, 0)),
        out_shape=jax.ShapeDtypeStruct((batch * seq, ATTN_WIDTH), F32),
        scratch_shapes=[
            pltpu.VMEM((ATTN_HEADS, seq, ATTN_HEAD_DIM), BF16),
            pltpu.VMEM((ATTN_HEADS, seq, ATTN_HEAD_DIM), BF16),
            pltpu.VMEM((ATTN_HEADS, ATTN_HEAD_DIM, seq), BF16),
            pltpu.VMEM((ATTN_HEADS, nb, seq), F32),
        ],
        compiler_params=_params("parallel", "arbitrary"),
        name="moba",
    )(q, k, v, cos, sin, swap)


def _outproj_body(x_ref, ys_ref, yp_ref, ya_ref, w_ref, g_ref, o_ref):
    o1 = SSD_INNER
    o2 = SSD_INNER + POOL_WIDTH
    m = jnp.dot(ys_ref[...].astype(BF16), w_ref[0:o1, :],
                preferred_element_type=F32)
    m = m + jnp.dot(yp_ref[...].astype(BF16), w_ref[o1:o2, :],
                    preferred_element_type=F32)
    m = m + jnp.dot(ya_ref[...].astype(BF16), w_ref[o2:, :],
                    preferred_element_type=F32)
    o_ref[...] = x_ref[...] + _rms(m) * g_ref[...]


def _outproj(x, y_ssd, y_pool, y_attn, w, g):
    m = x.shape[0]
    row = lambda i: (i, 0)
    return pl.pallas_call(
        _outproj_body,
        grid=(m // TOKEN_TILE,),
        in_specs=[
            pl.BlockSpec((TOKEN_TILE, D_MODEL), row),
            pl.BlockSpec((TOKEN_TILE, SSD_INNER), row),
            pl.BlockSpec((TOKEN_TILE, POOL_WIDTH), row),
            pl.BlockSpec((TOKEN_TILE, ATTN_WIDTH), row),
            _const_spec((D_MODEL, D_MODEL)),
            _const_spec((1, D_MODEL)),
        ],
        out_specs=pl.BlockSpec((TOKEN_TILE, D_MODEL), row),
        out_shape=jax.ShapeDtypeStruct((m, D_MODEL), F32),
        compiler_params=_params("parallel"),
        name="outproj",
    )(x, y_ssd, y_pool, y_attn, w, g)


def _rope_tables(seq):
    half = ROPE_DIM // 2
    inv_freq = ROPE_THETA ** (-jnp.arange(0, ROPE_DIM, 2, dtype=F32) / ROPE_DIM)
    ang = jnp.arange(seq, dtype=F32)[:, None] * inv_freq[None, :]
    ones = jnp.ones((seq, ATTN_HEAD_DIM - ROPE_DIM), F32)
    cos = jnp.concatenate([jnp.cos(ang), jnp.cos(ang), ones], axis=-1)
    sin = jnp.concatenate([jnp.sin(ang), jnp.sin(ang), 0.0 * ones], axis=-1)
    idx = jnp.arange(half)
    swap = jnp.zeros((ATTN_HEAD_DIM, ATTN_HEAD_DIM), F32)
    swap = swap.at[idx + half, idx].set(-1.0)
    swap = swap.at[idx, idx + half].set(1.0)
    return cos, sin, swap


def _arrange_w_in(w):
    dt_cols = jnp.pad(w[:, OFF_DT:OFF_POOL], ((0, 0), (0, DT_PAD - SSD_HEADS)))
    return jnp.concatenate(
        [w[:, OFF_Z:OFF_XBC], w[:, OFF_XBC:OFF_DT], w[:, OFF_POOL:OFF_Q],
         w[:, OFF_Q:OFF_K], w[:, OFF_K:OFF_V], w[:, OFF_V:IN_COLS], dt_cols],
        axis=1).astype(BF16)


def _per_head_lanes(p):
    return jnp.repeat(p, SSD_HEAD_DIM)[None, :]


def _pool_blockdiag(w):
    out = jnp.zeros((POOL_WIDTH, POOL_WIDTH), F32)
    for g in range(POOL_GROUPS):
        sl = slice(g * POOL_GROUP_DIM, (g + 1) * POOL_GROUP_DIM)
        out = out.at[sl, sl].set(w[g])
    return out.astype(BF16)


def kernel(x, ff1_norm_pre, ff1_w_gate, ff1_w_up, ff1_w_down, ff1_norm_post,
           mix_norm_pre, w_in, conv_w, conv_b, dt_bias, a_log, d_skip, ssd_norm,
           pool_w, pool_scale, w_out, mix_norm_post,
           ff2_norm_pre, ff2_w_gate, ff2_w_up, ff2_w_down, ff2_norm_post):
    batch, seq, d = x.shape
    depth = w_in.shape[0]
    h = x.reshape(batch * seq, d)
    cos, sin, swap = _rope_tables(seq)
    row = lambda p: p[None, :]
    for l in range(depth):
        h = _ffn(h, row(ff1_norm_pre[l]), ff1_w_gate[l].astype(BF16),
                 ff1_w_up[l].astype(BF16), ff1_w_down[l].astype(BF16),
                 row(ff1_norm_post[l]))
        z, xbc, u, q, k, v, dt = _inproj(h, row(mix_norm_pre[l]),
                                         _arrange_w_in(w_in[l]))
        y_ssd = _ssd(xbc, z, dt, conv_w[l], row(conv_b[l]),
                     _per_head_lanes(dt_bias[l]), _per_head_lanes(a_log[l]),
                     _per_head_lanes(d_skip[l]), row(ssd_norm[l]), batch, seq)
        y_pool = _pool(u, _pool_blockdiag(pool_w[l]), row(pool_scale[l]),
                       batch, seq)
        y_attn = _moba(q, k, v, cos, sin, swap, batch, seq)
        h = _outproj(h, y_ssd, y_pool, y_attn, w_out[l].astype(BF16),
                     row(mix_norm_post[l]))
        h = _ffn(h, row(ff2_norm_pre[l]), ff2_w_gate[l].astype(BF16),
                 ff2_w_up[l].astype(BF16), ff2_w_down[l].astype(BF16),
                 row(ff2_norm_post[l]))
    return h.reshape(batch, seq, d)
```

```python
import functools
import math

import jax
import jax.numpy as jnp
from jax import lax
from jax.experimental import pallas as pl
from jax.experimental.pallas import tpu as pltpu

F32 = jnp.float32
BF16 = jnp.bfloat16
HIGHEST = lax.Precision.HIGHEST

D_MODEL = 1024
D_FF = 2816
SSD_INNER = 512
SSD_HEAD_DIM = 64
SSD_HEADS = SSD_INNER // SSD_HEAD_DIM
SSD_GROUPS = 2
SSD_STATE = 128
SSD_CONV = 4
SSD_CHUNK = 128
SSD_BC = SSD_GROUPS * SSD_STATE
SSD_XBC = SSD_INNER + 2 * SSD_BC
POOL_WIDTH = 256
POOL_GROUPS = 4
POOL_GROUP_DIM = POOL_WIDTH // POOL_GROUPS
ATTN_WIDTH = 256
ATTN_HEAD_DIM = 64
ATTN_HEADS = ATTN_WIDTH // ATTN_HEAD_DIM
ROPE_DIM = ATTN_HEAD_DIM // 4
ROPE_THETA = 500000.0
MOBA_BLOCK = 256
MOBA_TOPK = 3
RMS_EPS = 1e-6

OFF_Z = 0
OFF_XBC = OFF_Z + SSD_INNER
OFF_DT = OFF_XBC + SSD_XBC
OFF_POOL = OFF_DT + SSD_HEADS
OFF_Q = OFF_POOL + POOL_WIDTH
OFF_K = OFF_Q + ATTN_WIDTH
OFF_V = OFF_K + ATTN_WIDTH
IN_COLS = OFF_V + ATTN_WIDTH

LANES = 128
SUBLANES = 8
MXU_DIM = 256
VMEM_LIMIT_BYTES = 56 * 1024 * 1024

DT_PAD = LANES
TOKEN_TILE = 512
FF_CHUNK = 2 * MXU_DIM


def _rms(x):
    return x * lax.rsqrt(jnp.mean(x * x, axis=-1, keepdims=True) + RMS_EPS)


def _silu(x):
    return x * jax.nn.sigmoid(x)


def _softplus(x):
    return jnp.maximum(x, 0.0) + jnp.log1p(jnp.exp(-jnp.abs(x)))


def _const_spec(shape):
    zeros = (0,) * len(shape)
    return pl.BlockSpec(shape, lambda *_: zeros, pipeline_mode=pl.Buffered(1))


def _params(*sem):
    return pltpu.CompilerParams(dimension_semantics=sem,
                                vmem_limit_bytes=VMEM_LIMIT_BYTES)


def _ffn_body(x_ref, gpre_ref, wg_ref, wu_ref, wd_ref, gpost_ref, o_ref, h_ref):
    x = x_ref[...]
    xb = (_rms(x) * gpre_ref[...]).astype(BF16)
    for c in range(0, D_FF, FF_CHUNK):
        sl = slice(c, min(c + FF_CHUNK, D_FF))
        g = jnp.dot(xb, wg_ref[:, sl], preferred_element_type=F32)
        u = jnp.dot(xb, wu_ref[:, sl], preferred_element_type=F32)
        h_ref[:, sl] = (_silu(g) * u).astype(BF16)
    f = jnp.dot(h_ref[...], wd_ref[...], preferred_element_type=F32)
    o_ref[...] = x + 0.5 * (_rms(f) * gpost_ref[...])


def _ffn(x, gpre, wg, wu, wd, gpost):
    m = x.shape[0]
    row = lambda i: (i, 0)
    return pl.pallas_call(
        _ffn_body,
        grid=(m // TOKEN_TILE,),
        in_specs=[
            pl.BlockSpec((TOKEN_TILE, D_MODEL), row),
            _const_spec((1, D_MODEL)),
            _const_spec((D_MODEL, D_FF)),
            _const_spec((D_MODEL, D_FF)),
            _const_spec((D_FF, D_MODEL)),
            _const_spec((1, D_MODEL)),
        ],
        out_specs=pl.BlockSpec((TOKEN_TILE, D_MODEL), row),
        out_shape=jax.ShapeDtypeStruct((m, D_MODEL), F32),
        scratch_shapes=[pltpu.VMEM((TOKEN_TILE, D_FF), BF16)],
        compiler_params=_params("parallel"),
        name="ffn",
    )(x, gpre, wg, wu, wd, gpost)


_PROJ_WIDTHS = (SSD_INNER, SSD_XBC, POOL_WIDTH, ATTN_WIDTH, ATTN_WIDTH,
                ATTN_WIDTH, DT_PAD)
_PROJ_COLS = sum(_PROJ_WIDTHS)


def _inproj_body(x_ref, g_ref, w_ref, *o_refs):
    xb = (_rms(x_ref[...]) * g_ref[...]).astype(BF16)
    off = 0
    for o_ref, width in zip(o_refs, _PROJ_WIDTHS):
        o_ref[...] = jnp.dot(xb, w_ref[:, off:off + width],
                             preferred_element_type=F32)
        off += width


def _inproj(x, g, w):
    m = x.shape[0]
    row = lambda i: (i, 0)
    return pl.pallas_call(
        _inproj_body,
        grid=(m // TOKEN_TILE,),
        in_specs=[
            pl.BlockSpec((TOKEN_TILE, D_MODEL), row),
            _const_spec((1, D_MODEL)),
            _const_spec((D_MODEL, _PROJ_COLS)),
        ],
        out_specs=[pl.BlockSpec((TOKEN_TILE, width), row)
                   for width in _PROJ_WIDTHS],
        out_shape=[jax.ShapeDtypeStruct((m, width), F32)
                   for width in _PROJ_WIDTHS],
        compiler_params=_params("parallel"),
        name="inproj",
    )(x, g, w)


def _ssd_body(xbc_ref, z_ref, dt_ref, convw_ref, convb_ref, dtb_ref, alog_ref,
              dskip_ref, gn_ref, o_ref, xpad_ref, state_ref):
    L = SSD_CHUNK
    HP = SSD_INNER
    P = SSD_HEAD_DIM
    N = SSD_STATE
    GW = HP // SSD_GROUPS
    HPG = SSD_HEADS // SSD_GROUPS

    @pl.when(pl.program_id(1) == 0)
    def _():
        xpad_ref[0:SUBLANES, :] = jnp.zeros((SUBLANES, SSD_XBC), F32)
        state_ref[...] = jnp.zeros_like(state_ref)

    xpad_ref[SUBLANES:SUBLANES + L, :] = xbc_ref[...]
    acc = jnp.broadcast_to(convb_ref[...], (L, SSD_XBC))
    for j in range(SSD_CONV):
        start = SUBLANES - (SSD_CONV - 1) + j
        acc = acc + convw_ref[j:j + 1, :] * xpad_ref[start:start + L, :]
    xpad_ref[0:SUBLANES, :] = xpad_ref[L:L + SUBLANES, :]
    xc = _silu(acc)
    xs = xc[:, :HP]
    bm = xc[:, HP:HP + SSD_BC].astype(BF16)
    cm = xc[:, HP + SSD_BC:].astype(BF16)

    dtraw = dt_ref[...]
    head_of_lane = lax.broadcasted_iota(jnp.int32, (L, HP), 1) // P
    dtx = jnp.zeros((L, HP), F32)
    for h in range(SSD_HEADS):
        dtx = jnp.where(head_of_lane == h, dtraw[:, h:h + 1], dtx)
    dt = _softplus(dtx + dtb_ref[...])
    da = dt * (-jnp.exp(alog_ref[...]))

    r = lax.broadcasted_iota(jnp.int32, (L, L), 0)
    s = lax.broadcasted_iota(jnp.int32, (L, L), 1)
    causal = s <= r
    tril = causal.astype(F32)
    acs = jnp.dot(tril, da, precision=HIGHEST, preferred_element_type=F32)
    a_last = acs[L - 1:L, :]
    decay_out = jnp.exp(acs)
    decay_in = jnp.exp(a_last - acs)
    chunk_decay = jnp.exp(a_last)

    xdt = xs * dt
    xdt_b = xdt.astype(BF16)
    xw_b = (xdt * decay_in).astype(BF16)

    y_parts = []
    for g in range(SSD_GROUPS):
        bg = bm[:, g * N:(g + 1) * N]
        cg = cm[:, g * N:(g + 1) * N]
        gsl = slice(g * GW, (g + 1) * GW)
        cb = lax.dot_general(cg, bg, (((1,), (1,)), ((), ())),
                             preferred_element_type=F32)
        st = state_ref[:, gsl]
        y_off = jnp.dot(cg, st.astype(BF16), preferred_element_type=F32)
        new_st = lax.dot_general(bg, xw_b[:, gsl], (((0,), (0,)), ((), ())),
                                 preferred_element_type=F32)
        state_ref[:, gsl] = st * chunk_decay[:, gsl] + new_st
        for pair in range(HPG // 2):
            lo = g * GW + pair * 2 * P
            acs_t = acs[:, lo:lo + 2 * P].T
            x_pair = xdt_b[:, lo:lo + 2 * P]
            ys = []
            for i in range(2):
                col = acs[:, lo + i * P:lo + i * P + 1]
                row = acs_t[i * P:i * P + 1, :]
                seg = jnp.exp(jnp.where(causal, col - row, -jnp.inf))
                ys.append(jnp.dot((cb * seg).astype(BF16), x_pair,
                                  preferred_element_type=F32))
            lane = lax.broadcasted_iota(jnp.int32, (L, 2 * P), 1)
            y_diag = jnp.where(lane < P, ys[0], ys[1])
            psl = slice(pair * 2 * P, (pair + 1) * 2 * P)
            y_parts.append(y_diag + y_off[:, psl] * decay_out[:, lo:lo + 2 * P])
    y = jnp.concatenate(y_parts, axis=-1) + dskip_ref[...] * xs
    o_ref[...] = _rms(y * _silu(z_ref[...])) * gn_ref[...]


def _ssd(xbc, z, dt, convw, convb, dtb, alog, dskip, gn, batch, seq):
    nc = seq // SSD_CHUNK
    row = lambda b, c: (b * nc + c, 0)
    return pl.pallas_call(
        _ssd_body,
        grid=(batch, nc),
        in_specs=[
            pl.BlockSpec((SSD_CHUNK, SSD_XBC), row),
            pl.BlockSpec((SSD_CHUNK, SSD_INNER), row),
            pl.BlockSpec((SSD_CHUNK, DT_PAD), row),
            _const_spec((SSD_CONV, SSD_XBC)),
            _const_spec((1, SSD_XBC)),
            _const_spec((1, SSD_INNER)),
            _const_spec((1, SSD_INNER)),
            _const_spec((1, SSD_INNER)),
            _const_spec((1, SSD_INNER)),
        ],
        out_specs=pl.BlockSpec((SSD_CHUNK, SSD_INNER), row),
        out_shape=jax.ShapeDtypeStruct((batch * seq, SSD_INNER), F32),
        scratch_shapes=[
            pltpu.VMEM((SUBLANES + SSD_CHUNK, SSD_XBC), F32),
            pltpu.VMEM((SSD_STATE, SSD_INNER), F32),
        ],
        compiler_params=_params("parallel", "arbitrary"),
        name="ssd",
    )(xbc, z, dt, convw, convb, dtb, alog, dskip, gn)


def _pool_body(u_ref, w_ref, scale_ref, o_ref):
    u = u_ref[...]
    seq = u.shape[0]
    t = lax.broadcasted_iota(jnp.int32, u.shape, 0)
    group = lax.broadcasted_iota(jnp.int32, u.shape, 1) // POOL_GROUP_DIM
    win_sum = u
    mean = jnp.zeros_like(u)
    for g in range(POOL_GROUPS):
        half = 2 ** g
        shifted = jnp.where(t >= half, pltpu.roll(win_sum, half, 0), 0.0)
        win_sum = win_sum + shifted
        count = jnp.minimum(t + 1, 2 * half).astype(F32)
        mean = jnp.where(group == g, win_sum / count, mean)
    d = (mean - u).astype(BF16)
    del seq
    o_ref[...] = jnp.dot(d, w_ref[...], preferred_element_type=F32) * scale_ref[...]


def _pool(u, w_blockdiag, scale, batch, seq):
    row = lambda b: (b, 0)
    return pl.pallas_call(
        _pool_body,
        grid=(batch,),
        in_specs=[
            pl.BlockSpec((seq, POOL_WIDTH), row),
            _const_spec((POOL_WIDTH, POOL_WIDTH)),
            _const_spec((1, POOL_WIDTH)),
        ],
        out_specs=pl.BlockSpec((seq, POOL_WIDTH), row),
        out_shape=jax.ShapeDtypeStruct((batch * seq, POOL_WIDTH), F32),
        compiler_params=_params("parallel"),
        name="pool",
    )(u, w_blockdiag, scale)


def _rope(t, cos, sin_up, sin_dn):
    half = ROPE_DIM // 2
    return (t * cos + pltpu.roll(t, half, 1) * sin_up
            + pltpu.roll(t, ATTN_WIDTH - half, 1) * sin_dn)


def _moba_body(q_ref, k_ref, v_ref, cosq_ref, supq_ref, sdnq_ref,
               cosk_ref, supk_ref, sdnk_ref, o_ref,
               ks_ref, vt_ref, kmean_ref, sel_ref, s_ref):
    seq = k_ref.shape[0]
    nb = seq // MOBA_BLOCK
    Dh = ATTN_HEAD_DIM
    BL = MOBA_BLOCK
    H = ATTN_HEADS
    i = pl.program_id(1)
    nt = (((1,), (1,)), ((), ()))

    @pl.when(i == 0)
    def _():
        for j in range(nb):
            rows = slice(j * BL, (j + 1) * BL)
            kj = _rope(k_ref[rows, :], cosk_ref[rows, :], supk_ref[rows, :],
                       sdnk_ref[rows, :])
            kmean_ref[j:j + 1, :] = jnp.mean(kj, axis=0, keepdims=True)
            kb = kj.astype(BF16)
            for h in range(H):
                ks_ref[h, rows, :] = kb[:, h * Dh:(h + 1) * Dh]
            vt_ref[:, rows] = v_ref[rows, :].T.astype(BF16)

    qf = _rope(q_ref[...], cosq_ref[...], supq_ref[...], sdnq_ref[...])
    qb = (qf * (Dh ** -0.5)).astype(BF16)
    blk = lax.broadcasted_iota(jnp.int32, (nb, BL), 0)
    past = blk < i
    for h in range(H):
        hs = slice(h * Dh, (h + 1) * Dh)
        gate = lax.dot_general(kmean_ref[:, hs], qf[:, hs], nt,
                               precision=HIGHEST, preferred_element_type=F32)
        gate = jnp.where(past, gate, -jnp.inf)
        rank = jnp.zeros((nb, BL), jnp.int32)
        for j2 in range(nb):
            g2 = gate[j2:j2 + 1, :]
            ahead = (g2 > gate) | ((g2 == gate) & (j2 < blk))
            rank = rank + ahead.astype(jnp.int32)
        sel_ref[h] = (past & (rank < MOBA_TOPK)).astype(F32)

    qh = [qb[:, h * Dh:(h + 1) * Dh] for h in range(H)]

    def masked_scores(h, j, keep):
        k0 = pl.multiple_of(j * BL, BL)
        st = lax.dot_general(ks_ref[h, pl.ds(k0, BL), :], qh[h], nt,
                             preferred_element_type=F32)
        st = jnp.where(keep, st, -jnp.inf)
        s_ref[h, j] = st
        return jnp.max(st, axis=0, keepdims=True)

    own_mask = (lax.broadcasted_iota(jnp.int32, (BL, BL), 0) <=
                lax.broadcasted_iota(jnp.int32, (BL, BL), 1))
    m_own = tuple(masked_scores(h, i, own_mask) for h in range(H))

    def pass1(j, ms):
        return tuple(
            jnp.maximum(ms[h], masked_scores(
                h, j, sel_ref[h, pl.ds(j, 1), :] > 0.5)) for h in range(H))

    ms = lax.fori_loop(0, i, pass1, m_own)

    def pass2(j, carry):
        k0 = pl.multiple_of(j * BL, BL)
        out = []
        for h in range(H):
            p = jnp.exp(s_ref[h, j] - ms[h])
            l = carry[2 * h] + jnp.sum(p, axis=0, keepdims=True)
            acc = carry[2 * h + 1] + jnp.dot(
                vt_ref[h * Dh:(h + 1) * Dh, pl.ds(k0, BL)], p.astype(BF16),
                preferred_element_type=F32)
            out.extend((l, acc))
        return tuple(out)

    init = (jnp.zeros((1, BL), F32), jnp.zeros((Dh, BL), F32)) * H
    carry = lax.fori_loop(0, i + 1, pass2, init)
    out_t = jnp.concatenate(
        [carry[2 * h + 1] / carry[2 * h] for h in range(H)], axis=0)
    o_ref[...] = out_t.T


def _moba(q, k, v, tables, batch, seq):
    nb = seq // MOBA_BLOCK
    full = lambda b, i: (b, 0)
    qrow = lambda b, i: (b * nb + i, 0)
    trow = lambda b, i: (i, 0)
    const = lambda b, i: (0, 0)
    qspec = pl.BlockSpec((MOBA_BLOCK, ATTN_WIDTH), qrow)
    tq = pl.BlockSpec((MOBA_BLOCK, ATTN_WIDTH), trow)
    tk = pl.BlockSpec((seq, ATTN_WIDTH), const)
    kv = pl.BlockSpec((seq, ATTN_WIDTH), full)
    return pl.pallas_call(
        _moba_body,
        grid=(batch, nb),
        in_specs=[qspec, kv, kv, tq, tq, tq, tk, tk, tk],
        out_specs=qspec,
        out_shape=jax.ShapeDtypeStruct((batch * seq, ATTN_WIDTH), F32),
        scratch_shapes=[
            pltpu.VMEM((ATTN_HEADS, seq, ATTN_HEAD_DIM), BF16),
            pltpu.VMEM((ATTN_WIDTH, seq), BF16),
            pltpu.VMEM((nb, ATTN_WIDTH), F32),
            pltpu.VMEM((ATTN_HEADS, nb, MOBA_BLOCK), F32),
            pltpu.VMEM((ATTN_HEADS, nb, MOBA_BLOCK, MOBA_BLOCK), F32),
        ],
        compiler_params=_params("parallel", "arbitrary"),
        name="moba",
    )(q, k, v, *tables, *tables)


def _outproj_body(x_ref, ys_ref, yp_ref, ya_ref, w_ref, g_ref, o_ref):
    o1 = SSD_INNER
    o2 = SSD_INNER + POOL_WIDTH
    m = jnp.dot(ys_ref[...].astype(BF16), w_ref[0:o1, :],
                preferred_element_type=F32)
    m = m + jnp.dot(yp_ref[...].astype(BF16), w_ref[o1:o2, :],
                    preferred_element_type=F32)
    m = m + jnp.dot(ya_ref[...].astype(BF16), w_ref[o2:, :],
                    preferred_element_type=F32)
    o_ref[...] = x_ref[...] + _rms(m) * g_ref[...]


def _outproj(x, y_ssd, y_pool, y_attn, w, g):
    m = x.shape[0]
    row = lambda i: (i, 0)
    return pl.pallas_call(
        _outproj_body,
        grid=(m // TOKEN_TILE,),
        in_specs=[
            pl.BlockSpec((TOKEN_TILE, D_MODEL), row),
            pl.BlockSpec((TOKEN_TILE, SSD_INNER), row),
            pl.BlockSpec((TOKEN_TILE, POOL_WIDTH), row),
            pl.BlockSpec((TOKEN_TILE, ATTN_WIDTH), row),
            _const_spec((D_MODEL, D_MODEL)),
            _const_spec((1, D_MODEL)),
        ],
        out_specs=pl.BlockSpec((TOKEN_TILE, D_MODEL), row),
        out_shape=jax.ShapeDtypeStruct((m, D_MODEL), F32),
        compiler_params=_params("parallel"),
        name="outproj",
    )(x, y_ssd, y_pool, y_attn, w, g)


def _rope_tables(seq):
    half = ROPE_DIM // 2
    inv_freq = ROPE_THETA ** (-jnp.arange(0, ROPE_DIM, 2, dtype=F32) / ROPE_DIM)
    ang = jnp.arange(seq, dtype=F32)[:, None] * inv_freq[None, :]
    rest = ATTN_HEAD_DIM - ROPE_DIM
    one = jnp.ones((seq, rest), F32)
    zero = jnp.zeros((seq, rest), F32)
    zh = jnp.zeros((seq, half), F32)
    cos = jnp.concatenate([jnp.cos(ang), jnp.cos(ang), one], axis=-1)
    sin_up = jnp.concatenate([zh, jnp.sin(ang), zero], axis=-1)
    sin_dn = jnp.concatenate([-jnp.sin(ang), zh, zero], axis=-1)
    return tuple(jnp.tile(t, (1, ATTN_HEADS)) for t in (cos, sin_up, sin_dn))


def _arrange_w_in(w):
    dt_cols = jnp.pad(w[:, OFF_DT:OFF_POOL], ((0, 0), (0, DT_PAD - SSD_HEADS)))
    return jnp.concatenate(
        [w[:, OFF_Z:OFF_XBC], w[:, OFF_XBC:OFF_DT], w[:, OFF_POOL:OFF_Q],
         w[:, OFF_Q:OFF_K], w[:, OFF_K:OFF_V], w[:, OFF_V:IN_COLS], dt_cols],
        axis=1).astype(BF16)


def _per_head_lanes(p):
    return jnp.repeat(p, SSD_HEAD_DIM)[None, :]


def _pool_blockdiag(w):
    out = jnp.zeros((POOL_WIDTH, POOL_WIDTH), F32)
    for g in range(POOL_GROUPS):
        sl = slice(g * POOL_GROUP_DIM, (g + 1) * POOL_GROUP_DIM)
        out = out.at[sl, sl].set(w[g])
    return out.astype(BF16)


def kernel(x, ff1_norm_pre, ff1_w_gate, ff1_w_up, ff1_w_down, ff1_norm_post,
           mix_norm_pre, w_in, conv_w, conv_b, dt_bias, a_log, d_skip, ssd_norm,
           pool_w, pool_scale, w_out, mix_norm_post,
           ff2_norm_pre, ff2_w_gate, ff2_w_up, ff2_w_down, ff2_norm_post):
    batch, seq, d = x.shape
    depth = w_in.shape[0]
    h = x.reshape(batch * seq, d)
    rope_tables = _rope_tables(seq)
    row = lambda p: p[None, :]
    for l in range(depth):
        h = _ffn(h, row(ff1_norm_pre[l]), ff1_w_gate[l].astype(BF16),
                 ff1_w_up[l].astype(BF16), ff1_w_down[l].astype(BF16),
                 row(ff1_norm_post[l]))
        z, xbc, u, q, k, v, dt = _inproj(h, row(mix_norm_pre[l]),
                                         _arrange_w_in(w_in[l]))
        y_ssd = _ssd(xbc, z, dt, conv_w[l], row(conv_b[l]),
                     _per_head_lanes(dt_bias[l]), _per_head_lanes(a_log[l]),
                     _per_head_lanes(d_skip[l]), row(ssd_norm[l]), batch, seq)
        y_pool = _pool(u, _pool_blockdiag(pool_w[l]), row(pool_scale[l]),
                       batch, seq)
        y_attn = _moba(q, k, v, rope_tables, batch, seq)
        h = _outproj(h, y_ssd, y_pool, y_attn, w_out[l].astype(BF16),
                     row(mix_norm_post[l]))
        h = _ffn(h, row(ff2_norm_pre[l]), ff2_w_gate[l].astype(BF16),
                 ff2_w_up[l].astype(BF16), ff2_w_down[l].astype(BF16),
                 row(ff2_norm_post[l]))
    return h.reshape(batch, seq, d)
```

```python
import functools
import math

import jax
import jax.numpy as jnp
from jax import lax
from jax.experimental import pallas as pl
from jax.experimental.pallas import tpu as pltpu

F32 = jnp.float32
BF16 = jnp.bfloat16
HIGHEST = lax.Precision.HIGHEST

D_MODEL = 1024
D_FF = 2816
SSD_INNER = 512
SSD_HEAD_DIM = 64
SSD_HEADS = SSD_INNER // SSD_HEAD_DIM
SSD_GROUPS = 2
SSD_STATE = 128
SSD_CONV = 4
SSD_CHUNK = 128
SSD_BC = SSD_GROUPS * SSD_STATE
SSD_XBC = SSD_INNER + 2 * SSD_BC
POOL_WIDTH = 256
POOL_GROUPS = 4
POOL_GROUP_DIM = POOL_WIDTH // POOL_GROUPS
ATTN_WIDTH = 256
ATTN_HEAD_DIM = 64
ATTN_HEADS = ATTN_WIDTH // ATTN_HEAD_DIM
ROPE_DIM = ATTN_HEAD_DIM // 4
ROPE_THETA = 500000.0
MOBA_BLOCK = 256
MOBA_TOPK = 3
RMS_EPS = 1e-6

OFF_Z = 0
OFF_XBC = OFF_Z + SSD_INNER
OFF_DT = OFF_XBC + SSD_XBC
OFF_POOL = OFF_DT + SSD_HEADS
OFF_Q = OFF_POOL + POOL_WIDTH
OFF_K = OFF_Q + ATTN_WIDTH
OFF_V = OFF_K + ATTN_WIDTH
IN_COLS = OFF_V + ATTN_WIDTH

LANES = 128
SUBLANES = 8
MXU_DIM = 256
VMEM_LIMIT_BYTES = 56 * 1024 * 1024

DT_PAD = LANES
TOKEN_TILE = 512
FF_CHUNK = 2 * MXU_DIM
SSD_STEP_ROWS = 4 * SSD_CHUNK


def _rms(x):
    return x * lax.rsqrt(jnp.mean(x * x, axis=-1, keepdims=True) + RMS_EPS)


def _silu(x):
    return x * jax.nn.sigmoid(x)


def _softplus(x):
    return jnp.maximum(x, 0.0) + jnp.log1p(jnp.exp(-jnp.abs(x)))


def _const_spec(shape):
    zeros = (0,) * len(shape)
    return pl.BlockSpec(shape, lambda *_: zeros, pipeline_mode=pl.Buffered(1))


def _layer_spec(shape, layer):
    zeros = (0,) * len(shape)
    return pl.BlockSpec((None,) + shape, lambda *_: (layer,) + zeros,
                        pipeline_mode=pl.Buffered(1))


def _params(*sem):
    return pltpu.CompilerParams(dimension_semantics=sem,
                                vmem_limit_bytes=VMEM_LIMIT_BYTES)


def _ffn_body(x_ref, gpre_ref, wg_ref, wu_ref, wd_ref, gpost_ref, o_ref, h_ref):
    x = x_ref[...]
    xb = (_rms(x) * gpre_ref[...]).astype(BF16)
    for c in range(0, D_FF, FF_CHUNK):
        sl = slice(c, min(c + FF_CHUNK, D_FF))
        g = jnp.dot(xb, wg_ref[:, sl], preferred_element_type=F32)
        u = jnp.dot(xb, wu_ref[:, sl], preferred_element_type=F32)
        h_ref[:, sl] = (_silu(g) * u).astype(BF16)
    f = jnp.dot(h_ref[...], wd_ref[...], preferred_element_type=F32)
    o_ref[...] = x + 0.5 * (_rms(f) * gpost_ref[...])


def _ffn(x, gpre, wg, wu, wd, gpost, layer):
    m = x.shape[0]
    row = lambda i: (i, 0)
    return pl.pallas_call(
        _ffn_body,
        grid=(m // TOKEN_TILE,),
        in_specs=[
            pl.BlockSpec((TOKEN_TILE, D_MODEL), row),
            _layer_spec((1, D_MODEL), layer),
            _layer_spec((D_MODEL, D_FF), layer),
            _layer_spec((D_MODEL, D_FF), layer),
            _layer_spec((D_FF, D_MODEL), layer),
            _layer_spec((1, D_MODEL), layer),
        ],
        out_specs=pl.BlockSpec((TOKEN_TILE, D_MODEL), row),
        out_shape=jax.ShapeDtypeStruct((m, D_MODEL), F32),
        scratch_shapes=[pltpu.VMEM((TOKEN_TILE, D_FF), BF16)],
        compiler_params=_params("parallel"),
        name="ffn",
    )(x, gpre, wg, wu, wd, gpost)


_PROJ_WIDTHS = (SSD_INNER, SSD_XBC, POOL_WIDTH, ATTN_WIDTH, ATTN_WIDTH,
                ATTN_WIDTH, DT_PAD)
_PROJ_COLS = sum(_PROJ_WIDTHS)


def _inproj_body(x_ref, g_ref, w_ref, *o_refs):
    xb = (_rms(x_ref[...]) * g_ref[...]).astype(BF16)
    off = 0
    for o_ref, width in zip(o_refs, _PROJ_WIDTHS):
        o_ref[...] = jnp.dot(xb, w_ref[:, off:off + width],
                             preferred_element_type=F32)
        off += width


def _inproj(x, g, w, layer):
    m = x.shape[0]
    row = lambda i: (i, 0)
    return pl.pallas_call(
        _inproj_body,
        grid=(m // TOKEN_TILE,),
        in_specs=[
            pl.BlockSpec((TOKEN_TILE, D_MODEL), row),
            _layer_spec((1, D_MODEL), layer),
            _layer_spec((D_MODEL, _PROJ_COLS), layer),
        ],
        out_specs=[pl.BlockSpec((TOKEN_TILE, width), row)
                   for width in _PROJ_WIDTHS],
        out_shape=[jax.ShapeDtypeStruct((m, width), F32)
                   for width in _PROJ_WIDTHS],
        compiler_params=_params("parallel"),
        name="inproj",
    )(x, g, w)


def _heads_to_lanes(xc):
    rows = xc.shape[0]
    lane = lax.broadcasted_iota(jnp.int32, (rows, LANES), 1)
    per_tile = LANES // SSD_HEAD_DIM
    parts = []
    for t in range(SSD_INNER // LANES):
        tile = jnp.broadcast_to(xc[:, t * per_tile:t * per_tile + 1], (rows, LANES))
        for i in range(1, per_tile):
            h = t * per_tile + i
            tile = jnp.where(lane < i * SSD_HEAD_DIM, tile,
                             jnp.broadcast_to(xc[:, h:h + 1], (rows, LANES)))
        parts.append(tile)
    return jnp.concatenate(parts, axis=1)


def _ssd_body(xbc_ref, z_ref, dt_ref, convw_ref, convb_ref, dtb_ref, alog_ref,
              dskip_ref, gn_ref, o_ref, tail_ref, state_ref):
    L = SSD_CHUNK
    HP = SSD_INNER
    P = SSD_HEAD_DIM
    N = SSD_STATE
    GW = HP // SSD_GROUPS
    HPG = SSD_HEADS // SSD_GROUPS

    @pl.when(pl.program_id(1) == 0)
    def _():
        tail_ref[...] = jnp.zeros_like(tail_ref)
        state_ref[...] = jnp.zeros_like(state_ref)

    r = lax.broadcasted_iota(jnp.int32, (L, L), 0)
    s = lax.broadcasted_iota(jnp.int32, (L, L), 1)
    causal = s <= r
    tril = causal.astype(F32)
    pair_lane = lax.broadcasted_iota(jnp.int32, (L, 2 * P), 1)
    neg_a = -jnp.exp(alog_ref[...])

    def chunk(c, tail):
        r0 = pl.multiple_of(c * L, L)
        rows = pl.ds(r0, L)
        x = xbc_ref[rows, :]
        x_ext = jnp.concatenate([tail, x], axis=0)
        acc = convb_ref[...] + convw_ref[SSD_CONV - 1:SSD_CONV, :] * x
        for j in range(SSD_CONV - 1):
            shifted = pltpu.roll(x_ext, SSD_CONV - 1 - j, 0)[SUBLANES:, :]
            acc = acc + convw_ref[j:j + 1, :] * shifted
        xc = _silu(acc)
        xs = xc[:, :HP]
        bm = xc[:, HP:HP + SSD_BC].astype(BF16)
        cm = xc[:, HP + SSD_BC:].astype(BF16)

        dt_c = _softplus(dt_ref[rows, :] + dtb_ref[...])
        acs_c = jnp.dot(tril, dt_c * neg_a, precision=HIGHEST,
                        preferred_element_type=F32)
        acs_t = acs_c.T
        dt = _heads_to_lanes(dt_c)
        acs = _heads_to_lanes(acs_c)
        a_last = acs[L - 1:L, :]
        decay_out = jnp.exp(acs)
        decay_in = jnp.exp(a_last - acs)
        chunk_decay = jnp.exp(a_last)

        xdt = xs * dt
        xdt_b = xdt.astype(BF16)
        xw_b = (xdt * decay_in).astype(BF16)

        y_parts = []
        for g in range(SSD_GROUPS):
            bg = bm[:, g * N:(g + 1) * N]
            cg = cm[:, g * N:(g + 1) * N]
            gsl = slice(g * GW, (g + 1) * GW)
            cb = lax.dot_general(cg, bg, (((1,), (1,)), ((), ())),
                                 preferred_element_type=F32)
            st = state_ref[:, gsl]
            y_off = jnp.dot(cg, st.astype(BF16), preferred_element_type=F32)
            new_st = lax.dot_general(bg, xw_b[:, gsl], (((0,), (0,)), ((), ())),
                                     preferred_element_type=F32)
            state_ref[:, gsl] = st * chunk_decay[:, gsl] + new_st
            for pair in range(HPG // 2):
                lo = g * GW + pair * 2 * P
                x_pair = xdt_b[:, lo:lo + 2 * P]
                ys = []
                for i in range(2):
                    h = lo // P + i
                    seg = jnp.exp(jnp.where(
                        causal, acs_c[:, h:h + 1] - acs_t[h:h + 1, :], -jnp.inf))
                    ys.append(jnp.dot((cb * seg).astype(BF16), x_pair,
                                      preferred_element_type=F32))
                y_diag = jnp.where(pair_lane < P, ys[0], ys[1])
                psl = slice(pair * 2 * P, (pair + 1) * 2 * P)
                y_parts.append(
                    y_diag + y_off[:, psl] * decay_out[:, lo:lo + 2 * P])
        y = jnp.concatenate(y_parts, axis=-1) + dskip_ref[...] * xs
        o_ref[rows, :] = _rms(y * _silu(z_ref[rows, :])) * gn_ref[...]
        return x[L - SUBLANES:, :]

    n_chunks = xbc_ref.shape[0] // L
    tail_ref[...] = lax.fori_loop(0, n_chunks, chunk, tail_ref[...])


def _ssd(xbc, z, dt, convw, convb, dtb, alog, dskip, gn, batch, seq):
    nblk = seq // SSD_STEP_ROWS
    row = lambda b, c: (b * nblk + c, 0)
    return pl.pallas_call(
        _ssd_body,
        grid=(batch, nblk),
        in_specs=[
            pl.BlockSpec((SSD_STEP_ROWS, SSD_XBC), row),
            pl.BlockSpec((SSD_STEP_ROWS, SSD_INNER), row),
            pl.BlockSpec((SSD_STEP_ROWS, DT_PAD), row),
            _const_spec((SSD_CONV, SSD_XBC)),
            _const_spec((1, SSD_XBC)),
            _const_spec((1, DT_PAD)),
            _const_spec((1, DT_PAD)),
            _const_spec((1, SSD_INNER)),
            _const_spec((1, SSD_INNER)),
        ],
        out_specs=pl.BlockSpec((SSD_STEP_ROWS, SSD_INNER), row),
        out_shape=jax.ShapeDtypeStruct((batch * seq, SSD_INNER), F32),
        scratch_shapes=[
            pltpu.VMEM((SUBLANES, SSD_XBC), F32),
            pltpu.VMEM((SSD_STATE, SSD_INNER), F32),
        ],
        compiler_params=_params("parallel", "arbitrary"),
        name="ssd",
    )(xbc, z, dt, convw, convb, dtb, alog, dskip, gn)


def _pool_body(u_ref, w_ref, scale_ref, o_ref):
    u = u_ref[...]
    t = lax.broadcasted_iota(jnp.int32, u.shape, 0)
    group = lax.broadcasted_iota(jnp.int32, u.shape, 1) // POOL_GROUP_DIM
    win_sum = u
    mean = jnp.zeros_like(u)
    for g in range(POOL_GROUPS):
        half = 2 ** g
        shifted = jnp.where(t >= half, pltpu.roll(win_sum, half, 0), 0.0)
        win_sum = win_sum + shifted
        count = jnp.minimum(t + 1, 2 * half).astype(F32)
        mean = jnp.where(group == g, win_sum / count, mean)
    d = (mean - u).astype(BF16)
    o_ref[...] = jnp.dot(d, w_ref[...], preferred_element_type=F32) * scale_ref[...]


def _pool(u, w_blockdiag, scale, batch, seq):
    row = lambda b: (b, 0)
    return pl.pallas_call(
        _pool_body,
        grid=(batch,),
        in_specs=[
            pl.BlockSpec((seq, POOL_WIDTH), row),
            _const_spec((POOL_WIDTH, POOL_WIDTH)),
            _const_spec((1, POOL_WIDTH)),
        ],
        out_specs=pl.BlockSpec((seq, POOL_WIDTH), row),
        out_shape=jax.ShapeDtypeStruct((batch * seq, POOL_WIDTH), F32),
        compiler_params=_params("parallel"),
        name="pool",
    )(u, w_blockdiag, scale)


def _rope(t, cos, sin_up, sin_dn):
    half = ROPE_DIM // 2
    return (t * cos + pltpu.roll(t, half, 1) * sin_up
            + pltpu.roll(t, ATTN_WIDTH - half, 1) * sin_dn)


def _moba_body(q_ref, k_ref, v_ref, cosq_ref, supq_ref, sdnq_ref,
               cosk_ref, supk_ref, sdnk_ref, o_ref,
               ks_ref, vt_ref, kmean_ref, sel_ref, s_ref):
    seq = k_ref.shape[0]
    nb = seq // MOBA_BLOCK
    Dh = ATTN_HEAD_DIM
    BL = MOBA_BLOCK
    H = ATTN_HEADS
    i = pl.program_id(1)
    nt = (((1,), (1,)), ((), ()))

    @pl.when(i == 0)
    def _():
        for j in range(nb):
            rows = slice(j * BL, (j + 1) * BL)
            kj = _rope(k_ref[rows, :], cosk_ref[rows, :], supk_ref[rows, :],
                       sdnk_ref[rows, :])
            kmean_ref[j:j + 1, :] = jnp.mean(kj, axis=0, keepdims=True)
            kb = kj.astype(BF16)
            for h in range(H):
                ks_ref[h, rows, :] = kb[:, h * Dh:(h + 1) * Dh]
            vt_ref[:, rows] = v_ref[rows, :].T.astype(BF16)

    qf = _rope(q_ref[...], cosq_ref[...], supq_ref[...], sdnq_ref[...])
    qb = (qf * (Dh ** -0.5)).astype(BF16)
    blk = lax.broadcasted_iota(jnp.int32, (nb, BL), 0)
    past = blk < i
    for h in range(H):
        hs = slice(h * Dh, (h + 1) * Dh)
        gate = lax.dot_general(kmean_ref[:, hs], qf[:, hs], nt,
                               precision=HIGHEST, preferred_element_type=F32)
        gate = jnp.where(past, gate, -jnp.inf)
        rank = jnp.zeros((nb, BL), jnp.int32)
        for j2 in range(nb):
            g2 = gate[j2:j2 + 1, :]
            ahead = (g2 > gate) | ((g2 == gate) & (j2 < blk))
            rank = rank + ahead.astype(jnp.int32)
        sel_ref[h] = (past & (rank < MOBA_TOPK)).astype(F32)

    qh = [qb[:, h * Dh:(h + 1) * Dh] for h in range(H)]

    def masked_scores(h, j, keep):
        k0 = pl.multiple_of(j * BL, BL)
        st = lax.dot_general(ks_ref[h, pl.ds(k0, BL), :], qh[h], nt,
                             preferred_element_type=F32)
        st = jnp.where(keep, st, -jnp.inf)
        s_ref[h, j] = st
        return jnp.max(st, axis=0, keepdims=True)

    own_mask = (lax.broadcasted_iota(jnp.int32, (BL, BL), 0) <=
                lax.broadcasted_iota(jnp.int32, (BL, BL), 1))
    m_own = tuple(masked_scores(h, i, own_mask) for h in range(H))

    def pass1(j, ms):
        return tuple(
            jnp.maximum(ms[h], masked_scores(
                h, j, sel_ref[h, pl.ds(j, 1), :] > 0.5)) for h in range(H))

    ms = lax.fori_loop(0, i, pass1, m_own)

    def pass2(j, carry):
        k0 = pl.multiple_of(j * BL, BL)
        out = []
        for h in range(H):
            p = jnp.exp(s_ref[h, j] - ms[h])
            l = carry[2 * h] + jnp.sum(p, axis=0, keepdims=True)
            acc = carry[2 * h + 1] + jnp.dot(
                vt_ref[h * Dh:(h + 1) * Dh, pl.ds(k0, BL)], p.astype(BF16),
                preferred_element_type=F32)
            out.extend((l, acc))
        return tuple(out)

    init = (jnp.zeros((1, BL), F32), jnp.zeros((Dh, BL), F32)) * H
    carry = lax.fori_loop(0, i + 1, pass2, init)
    out_t = jnp.concatenate(
        [carry[2 * h + 1] / carry[2 * h] for h in range(H)], axis=0)
    o_ref[...] = out_t.T


def _moba(q, k, v, tables, batch, seq):
    nb = seq // MOBA_BLOCK
    full = lambda b, i: (b, 0)
    qrow = lambda b, i: (b * nb + i, 0)
    trow = lambda b, i: (i, 0)
    const = lambda b, i: (0, 0)
    qspec = pl.BlockSpec((MOBA_BLOCK, ATTN_WIDTH), qrow)
    tq = pl.BlockSpec((MOBA_BLOCK, ATTN_WIDTH), trow)
    tk = pl.BlockSpec((seq, ATTN_WIDTH), const)
    kv = pl.BlockSpec((seq, ATTN_WIDTH), full)
    return pl.pallas_call(
        _moba_body,
        grid=(batch, nb),
        in_specs=[qspec, kv, kv, tq, tq, tq, tk, tk, tk],
        out_specs=qspec,
        out_shape=jax.ShapeDtypeStruct((batch * seq, ATTN_WIDTH), F32),
        scratch_shapes=[
            pltpu.VMEM((ATTN_HEADS, seq, ATTN_HEAD_DIM), BF16),
            pltpu.VMEM((ATTN_WIDTH, seq), BF16),
            pltpu.VMEM((nb, ATTN_WIDTH), F32),
            pltpu.VMEM((ATTN_HEADS, nb, MOBA_BLOCK), F32),
            pltpu.VMEM((ATTN_HEADS, nb, MOBA_BLOCK, MOBA_BLOCK), F32),
        ],
        compiler_params=_params("parallel", "arbitrary"),
        name="moba",
    )(q, k, v, *tables, *tables)


def _outproj_body(x_ref, ys_ref, yp_ref, ya_ref, w_ref, g_ref, o_ref):
    o1 = SSD_INNER
    o2 = SSD_INNER + POOL_WIDTH
    m = jnp.dot(ys_ref[...].astype(BF16), w_ref[0:o1, :],
                preferred_element_type=F32)
    m = m + jnp.dot(yp_ref[...].astype(BF16), w_ref[o1:o2, :],
                    preferred_element_type=F32)
    m = m + jnp.dot(ya_ref[...].astype(BF16), w_ref[o2:, :],
                    preferred_element_type=F32)
    o_ref[...] = x_ref[...] + _rms(m) * g_ref[...]


def _outproj(x, y_ssd, y_pool, y_attn, w, g, layer):
    m = x.shape[0]
    row = lambda i: (i, 0)
    return pl.pallas_call(
        _outproj_body,
        grid=(m // TOKEN_TILE,),
        in_specs=[
            pl.BlockSpec((TOKEN_TILE, D_MODEL), row),
            pl.BlockSpec((TOKEN_TILE, SSD_INNER), row),
            pl.BlockSpec((TOKEN_TILE, POOL_WIDTH), row),
            pl.BlockSpec((TOKEN_TILE, ATTN_WIDTH), row),
            _layer_spec((D_MODEL, D_MODEL), layer),
            _layer_spec((1, D_MODEL), layer),
        ],
        out_specs=pl.BlockSpec((TOKEN_TILE, D_MODEL), row),
        out_shape=jax.ShapeDtypeStruct((m, D_MODEL), F32),
        compiler_params=_params("parallel"),
        name="outproj",
    )(x, y_ssd, y_pool, y_attn, w, g)


def _rope_tables(seq):
    half = ROPE_DIM // 2
    inv_freq = ROPE_THETA ** (-jnp.arange(0, ROPE_DIM, 2, dtype=F32) / ROPE_DIM)
    ang = jnp.arange(seq, dtype=F32)[:, None] * inv_freq[None, :]
    rest = ATTN_HEAD_DIM - ROPE_DIM
    one = jnp.ones((seq, rest), F32)
    zero = jnp.zeros((seq, rest), F32)
    zh = jnp.zeros((seq, half), F32)
    cos = jnp.concatenate([jnp.cos(ang), jnp.cos(ang), one], axis=-1)
    sin_up = jnp.concatenate([zh, jnp.sin(ang), zero], axis=-1)
    sin_dn = jnp.concatenate([-jnp.sin(ang), zh, zero], axis=-1)
    return tuple(jnp.tile(t, (1, ATTN_HEADS)) for t in (cos, sin_up, sin_dn))


def _arrange_w_in(w):
    pad = [(0, 0)] * (w.ndim - 1) + [(0, DT_PAD - SSD_HEADS)]
    dt_cols = jnp.pad(w[..., OFF_DT:OFF_POOL], pad)
    return jnp.concatenate(
        [w[..., OFF_Z:OFF_XBC], w[..., OFF_XBC:OFF_DT], w[..., OFF_POOL:OFF_Q],
         w[..., OFF_Q:OFF_K], w[..., OFF_K:OFF_V], w[..., OFF_V:IN_COLS],
         dt_cols], axis=-1).astype(BF16)


def _per_head_lanes(p):
    return jnp.repeat(p, SSD_HEAD_DIM)[None, :]


def _per_head_cols(p):
    return jnp.pad(p, (0, DT_PAD - SSD_HEADS))[None, :]


def _pool_blockdiag(w):
    out = jnp.zeros((POOL_WIDTH, POOL_WIDTH), F32)
    for g in range(POOL_GROUPS):
        sl = slice(g * POOL_GROUP_DIM, (g + 1) * POOL_GROUP_DIM)
        out = out.at[sl, sl].set(w[g])
    return out.astype(BF16)


def kernel(x, ff1_norm_pre, ff1_w_gate, ff1_w_up, ff1_w_down, ff1_norm_post,
           mix_norm_pre, w_in, conv_w, conv_b, dt_bias, a_log, d_skip, ssd_norm,
           pool_w, pool_scale, w_out, mix_norm_post,
           ff2_norm_pre, ff2_w_gate, ff2_w_up, ff2_w_down, ff2_norm_post):
    batch, seq, d = x.shape
    depth = w_in.shape[0]
    h = x.reshape(batch * seq, d)
    rope_tables = _rope_tables(seq)
    row = lambda p: p[None, :]
    stack_row = lambda p: p[:, None, :]
    bf = lambda w: w.astype(BF16)
    ff1 = (stack_row(ff1_norm_pre), bf(ff1_w_gate), bf(ff1_w_up), bf(ff1_w_down),
           stack_row(ff1_norm_post))
    ff2 = (stack_row(ff2_norm_pre), bf(ff2_w_gate), bf(ff2_w_up), bf(ff2_w_down),
           stack_row(ff2_norm_post))
    w_in_b = _arrange_w_in(w_in)
    w_out_b = bf(w_out)
    g_mix_pre = stack_row(mix_norm_pre)
    g_mix_post = stack_row(mix_norm_post)
    for l in range(depth):
        h = _ffn(h, *ff1, l)
        z, xbc, u, q, k, v, dt = _inproj(h, g_mix_pre, w_in_b, l)
        y_ssd = _ssd(xbc, z, dt, conv_w[l], row(conv_b[l]),
                     _per_head_cols(dt_bias[l]), _per_head_cols(a_log[l]),
                     _per_head_lanes(d_skip[l]), row(ssd_norm[l]), batch, seq)
        y_pool = _pool(u, _pool_blockdiag(pool_w[l]), row(pool_scale[l]),
                       batch, seq)
        y_attn = _moba(q, k, v, rope_tables, batch, seq)
        h = _outproj(h, y_ssd, y_pool, y_attn, w_out_b, g_mix_post, l)
        h = _ffn(h, *ff2, l)
    return h.reshape(batch, seq, d)
```

```python
import functools
import math

import jax
import jax.numpy as jnp
from jax import lax
from jax.experimental import pallas as pl
from jax.experimental.pallas import tpu as pltpu

F32 = jnp.float32
BF16 = jnp.bfloat16
HIGHEST = lax.Precision.HIGHEST

D_MODEL = 1024
D_FF = 2816
SSD_INNER = 512
SSD_HEAD_DIM = 64
SSD_HEADS = SSD_INNER // SSD_HEAD_DIM
SSD_GROUPS = 2
SSD_STATE = 128
SSD_CONV = 4
SSD_CHUNK = 128
SSD_BC = SSD_GROUPS * SSD_STATE
SSD_XBC = SSD_INNER + 2 * SSD_BC
POOL_WIDTH = 256
POOL_GROUPS = 4
POOL_GROUP_DIM = POOL_WIDTH // POOL_GROUPS
ATTN_WIDTH = 256
ATTN_HEAD_DIM = 64
ATTN_HEADS = ATTN_WIDTH // ATTN_HEAD_DIM
ROPE_DIM = ATTN_HEAD_DIM // 4
ROPE_THETA = 500000.0
MOBA_BLOCK = 256
MOBA_TOPK = 3
RMS_EPS = 1e-6

OFF_Z = 0
OFF_XBC = OFF_Z + SSD_INNER
OFF_DT = OFF_XBC + SSD_XBC
OFF_POOL = OFF_DT + SSD_HEADS
OFF_Q = OFF_POOL + POOL_WIDTH
OFF_K = OFF_Q + ATTN_WIDTH
OFF_V = OFF_K + ATTN_WIDTH
IN_COLS = OFF_V + ATTN_WIDTH

LANES = 128
SUBLANES = 8
MXU_DIM = 256
VMEM_LIMIT_BYTES = 56 * 1024 * 1024

DT_PAD = LANES
TOKEN_TILE = 512
FF_CHUNK = 2 * MXU_DIM
SSD_STEP_ROWS = 4 * SSD_CHUNK


def _rms(x):
    return x * lax.rsqrt(jnp.mean(x * x, axis=-1, keepdims=True) + RMS_EPS)


def _silu(x):
    return x * jax.nn.sigmoid(x)


def _softplus(x):
    return jnp.maximum(x, 0.0) + jnp.log1p(jnp.exp(-jnp.abs(x)))


def _const_spec(shape):
    zeros = (0,) * len(shape)
    return pl.BlockSpec(shape, lambda *_: zeros, pipeline_mode=pl.Buffered(1))


def _layer_spec(shape, layer):
    zeros = (0,) * len(shape)
    return pl.BlockSpec((None,) + shape, lambda *_: (layer,) + zeros,
                        pipeline_mode=pl.Buffered(1))


def _params(*sem):
    return pltpu.CompilerParams(dimension_semantics=sem,
                                vmem_limit_bytes=VMEM_LIMIT_BYTES)


def _ffn_tile(x, gpre_ref, wg_ref, wu_ref, wd_ref, gpost_ref, h_ref):
    xb = (_rms(x) * gpre_ref[...]).astype(BF16)
    for c in range(0, D_FF, FF_CHUNK):
        sl = slice(c, min(c + FF_CHUNK, D_FF))
        g = jnp.dot(xb, wg_ref[:, sl], preferred_element_type=F32)
        u = jnp.dot(xb, wu_ref[:, sl], preferred_element_type=F32)
        h_ref[:, sl] = (_silu(g) * u).astype(BF16)
    f = jnp.dot(h_ref[...], wd_ref[...], preferred_element_type=F32)
    return x + 0.5 * (_rms(f) * gpost_ref[...])


_PROJ_WIDTHS = (SSD_INNER, SSD_XBC, POOL_WIDTH, ATTN_WIDTH, ATTN_WIDTH,
                ATTN_WIDTH, DT_PAD)
_PROJ_COLS = sum(_PROJ_WIDTHS)


def _ffn_inproj_body(x_ref, gpre_ref, wg_ref, wu_ref, wd_ref, gpost_ref,
                     g_ref, w_ref, o_ref, *rest):
    *proj_refs, h_ref = rest
    x = _ffn_tile(x_ref[...], gpre_ref, wg_ref, wu_ref, wd_ref, gpost_ref, h_ref)
    o_ref[...] = x
    xb = (_rms(x) * g_ref[...]).astype(BF16)
    off = 0
    for p_ref, width in zip(proj_refs, _PROJ_WIDTHS):
        p_ref[...] = jnp.dot(xb, w_ref[:, off:off + width],
                             preferred_element_type=F32)
        off += width


def _ffn_inproj(x, gpre, wg, wu, wd, gpost, g, w, layer):
    m = x.shape[0]
    row = lambda i: (i, 0)
    widths = (D_MODEL,) + _PROJ_WIDTHS
    return pl.pallas_call(
        _ffn_inproj_body,
        grid=(m // TOKEN_TILE,),
        in_specs=[
            pl.BlockSpec((TOKEN_TILE, D_MODEL), row),
            _layer_spec((1, D_MODEL), layer),
            _layer_spec((D_MODEL, D_FF), layer),
            _layer_spec((D_MODEL, D_FF), layer),
            _layer_spec((D_FF, D_MODEL), layer),
            _layer_spec((1, D_MODEL), layer),
            _layer_spec((1, D_MODEL), layer),
            _layer_spec((D_MODEL, _PROJ_COLS), layer),
        ],
        out_specs=[pl.BlockSpec((TOKEN_TILE, width), row) for width in widths],
        out_shape=[jax.ShapeDtypeStruct((m, width), F32) for width in widths],
        scratch_shapes=[pltpu.VMEM((TOKEN_TILE, D_FF), BF16)],
        compiler_params=_params("parallel"),
        name="ffn_inproj",
    )(x, gpre, wg, wu, wd, gpost, g, w)


def _heads_to_lanes(xc):
    rows = xc.shape[0]
    lane = lax.broadcasted_iota(jnp.int32, (rows, LANES), 1)
    per_tile = LANES // SSD_HEAD_DIM
    parts = []
    for t in range(SSD_INNER // LANES):
        tile = jnp.broadcast_to(xc[:, t * per_tile:t * per_tile + 1], (rows, LANES))
        for i in range(1, per_tile):
            h = t * per_tile + i
            tile = jnp.where(lane < i * SSD_HEAD_DIM, tile,
                             jnp.broadcast_to(xc[:, h:h + 1], (rows, LANES)))
        parts.append(tile)
    return jnp.concatenate(parts, axis=1)


def _ssd_body(xbc_ref, z_ref, dt_ref, convw_ref, convb_ref, dtb_ref, alog_ref,
              dskip_ref, gn_ref, o_ref, tail_ref, state_ref):
    L = SSD_CHUNK
    HP = SSD_INNER
    P = SSD_HEAD_DIM
    N = SSD_STATE
    GW = HP // SSD_GROUPS
    HPG = SSD_HEADS // SSD_GROUPS

    @pl.when(pl.program_id(1) == 0)
    def _():
        tail_ref[...] = jnp.zeros_like(tail_ref)
        state_ref[...] = jnp.zeros_like(state_ref)

    r = lax.broadcasted_iota(jnp.int32, (L, L), 0)
    s = lax.broadcasted_iota(jnp.int32, (L, L), 1)
    causal = s <= r
    tril = causal.astype(F32)
    pair_lane = lax.broadcasted_iota(jnp.int32, (L, 2 * P), 1)
    neg_a = -jnp.exp(alog_ref[...])

    def chunk(c, tail):
        r0 = pl.multiple_of(c * L, L)
        rows = pl.ds(r0, L)
        x = xbc_ref[rows, :]
        x_ext = jnp.concatenate([tail, x], axis=0)
        prev_ext = pltpu.roll(x_ext, 1, 0)
        near = (convw_ref[3:4, :] * x
                + convw_ref[2:3, :] * prev_ext[SUBLANES:, :])
        far_ext = convw_ref[1:2, :] * x_ext + convw_ref[0:1, :] * prev_ext
        acc = near + pltpu.roll(far_ext, 2, 0)[SUBLANES:, :] + convb_ref[...]
        xc = _silu(acc)
        xs = xc[:, :HP]
        bm = xc[:, HP:HP + SSD_BC].astype(BF16)
        cm = xc[:, HP + SSD_BC:].astype(BF16)

        dt_c = _softplus(dt_ref[rows, :] + dtb_ref[...])
        acs_c = jnp.dot(tril, dt_c * neg_a, precision=HIGHEST,
                        preferred_element_type=F32)
        acs_t = acs_c.T
        dt = _heads_to_lanes(dt_c)
        acs = _heads_to_lanes(acs_c)
        a_last = acs[L - 1:L, :]
        decay_out = jnp.exp(acs)
        decay_in = jnp.exp(a_last - acs)
        chunk_decay = jnp.exp(a_last)

        xdt = xs * dt
        xdt_b = xdt.astype(BF16)
        xw_b = (xdt * decay_in).astype(BF16)

        y_parts = []
        for g in range(SSD_GROUPS):
            bg = bm[:, g * N:(g + 1) * N]
            cg = cm[:, g * N:(g + 1) * N]
            gsl = slice(g * GW, (g + 1) * GW)
            cb = lax.dot_general(cg, bg, (((1,), (1,)), ((), ())),
                                 preferred_element_type=F32)
            st = state_ref[:, gsl]
            y_off = jnp.dot(cg, st.astype(BF16), preferred_element_type=F32)
            new_st = lax.dot_general(bg, xw_b[:, gsl], (((0,), (0,)), ((), ())),
                                     preferred_element_type=F32)
            state_ref[:, gsl] = st * chunk_decay[:, gsl] + new_st
            for pair in range(HPG // 2):
                lo = g * GW + pair * 2 * P
                x_pair = xdt_b[:, lo:lo + 2 * P]
                ys = []
                for i in range(2):
                    h = lo // P + i
                    seg = jnp.exp(jnp.where(
                        causal, acs_c[:, h:h + 1] - acs_t[h:h + 1, :], -jnp.inf))
                    ys.append(jnp.dot((cb * seg).astype(BF16), x_pair,
                                      preferred_element_type=F32))
                y_diag = jnp.where(pair_lane < P, ys[0], ys[1])
                psl = slice(pair * 2 * P, (pair + 1) * 2 * P)
                y_parts.append(
                    y_diag + y_off[:, psl] * decay_out[:, lo:lo + 2 * P])
        y = jnp.concatenate(y_parts, axis=-1) + dskip_ref[...] * xs
        o_ref[rows, :] = _rms(y * _silu(z_ref[rows, :])) * gn_ref[...]
        return x[L - SUBLANES:, :]

    n_chunks = xbc_ref.shape[0] // L
    tail_ref[...] = lax.fori_loop(0, n_chunks, chunk, tail_ref[...])


def _ssd(xbc, z, dt, convw, convb, dtb, alog, dskip, gn, batch, seq):
    nblk = seq // SSD_STEP_ROWS
    row = lambda b, c: (b * nblk + c, 0)
    return pl.pallas_call(
        _ssd_body,
        grid=(batch, nblk),
        in_specs=[
            pl.BlockSpec((SSD_STEP_ROWS, SSD_XBC), row),
            pl.BlockSpec((SSD_STEP_ROWS, SSD_INNER), row),
            pl.BlockSpec((SSD_STEP_ROWS, DT_PAD), row),
            _const_spec((SSD_CONV, SSD_XBC)),
            _const_spec((1, SSD_XBC)),
            _const_spec((1, DT_PAD)),
            _const_spec((1, DT_PAD)),
            _const_spec((1, SSD_INNER)),
            _const_spec((1, SSD_INNER)),
        ],
        out_specs=pl.BlockSpec((SSD_STEP_ROWS, SSD_INNER), row),
        out_shape=jax.ShapeDtypeStruct((batch * seq, SSD_INNER), F32),
        scratch_shapes=[
            pltpu.VMEM((SUBLANES, SSD_XBC), F32),
            pltpu.VMEM((SSD_STATE, SSD_INNER), F32),
        ],
        compiler_params=_params("parallel", "arbitrary"),
        name="ssd",
    )(xbc, z, dt, convw, convb, dtb, alog, dskip, gn)


def _pool_body(u_ref, w_ref, scale_ref, o_ref):
    u = u_ref[...]
    t = lax.broadcasted_iota(jnp.int32, u.shape, 0)
    group = lax.broadcasted_iota(jnp.int32, u.shape, 1) // POOL_GROUP_DIM
    win_sum = u
    mean = jnp.zeros_like(u)
    for g in range(POOL_GROUPS):
        half = 2 ** g
        shifted = jnp.where(t >= half, pltpu.roll(win_sum, half, 0), 0.0)
        win_sum = win_sum + shifted
        count = jnp.minimum(t + 1, 2 * half).astype(F32)
        mean = jnp.where(group == g, win_sum / count, mean)
    d = (mean - u).astype(BF16)
    o_ref[...] = jnp.dot(d, w_ref[...], preferred_element_type=F32) * scale_ref[...]


def _pool(u, w_blockdiag, scale, batch, seq):
    row = lambda b: (b, 0)
    return pl.pallas_call(
        _pool_body,
        grid=(batch,),
        in_specs=[
            pl.BlockSpec((seq, POOL_WIDTH), row),
            _const_spec((POOL_WIDTH, POOL_WIDTH)),
            _const_spec((1, POOL_WIDTH)),
        ],
        out_specs=pl.BlockSpec((seq, POOL_WIDTH), row),
        out_shape=jax.ShapeDtypeStruct((batch * seq, POOL_WIDTH), F32),
        compiler_params=_params("parallel"),
        name="pool",
    )(u, w_blockdiag, scale)


def _rope(t, cos, sin_up, sin_dn):
    half = ROPE_DIM // 2
    return (t * cos + pltpu.roll(t, half, 1) * sin_up
            + pltpu.roll(t, ATTN_WIDTH - half, 1) * sin_dn)


def _moba_body(q_ref, k_ref, v_ref, cosq_ref, supq_ref, sdnq_ref,
               cosk_ref, supk_ref, sdnk_ref, o_ref,
               ks_ref, vt_ref, kmean_ref, sel_ref, s_ref):
    seq = k_ref.shape[0]
    nb = seq // MOBA_BLOCK
    Dh = ATTN_HEAD_DIM
    BL = MOBA_BLOCK
    H = ATTN_HEADS
    i = pl.program_id(1)
    nt = (((1,), (1,)), ((), ()))

    head_of_lane = lax.broadcasted_iota(jnp.int32, (1, ATTN_WIDTH), 1) // Dh

    @pl.when(i == 0)
    def _():
        for j in range(nb):
            rows = slice(j * BL, (j + 1) * BL)
            kj = _rope(k_ref[rows, :], cosk_ref[rows, :], supk_ref[rows, :],
                       sdnk_ref[rows, :])
            kmean = jnp.mean(kj, axis=0, keepdims=True)
            for h in range(H):
                kmean_ref[h * nb + j:h * nb + j + 1, :] = jnp.where(
                    head_of_lane == h, kmean, 0.0)
            kb = kj.astype(BF16)
            for h in range(H):
                ks_ref[h, rows, :] = kb[:, h * Dh:(h + 1) * Dh]
            vt_ref[:, rows] = v_ref[rows, :].T.astype(BF16)

    qf = _rope(q_ref[...], cosq_ref[...], supq_ref[...], sdnq_ref[...])
    qb = (qf * (Dh ** -0.5)).astype(BF16)
    blk = lax.broadcasted_iota(jnp.int32, (nb, BL), 0)
    past = blk < i
    gates = lax.dot_general(kmean_ref[...], qf, nt, precision=HIGHEST,
                            preferred_element_type=F32)
    for h in range(H):
        gate = jnp.where(past, gates[h * nb:(h + 1) * nb, :], -jnp.inf)
        rank = jnp.zeros((nb, BL), jnp.int32)
        for j2 in range(nb):
            g2 = gate[j2:j2 + 1, :]
            ahead = (g2 > gate) | ((g2 == gate) & (j2 < blk))
            rank = rank + ahead.astype(jnp.int32)
        sel_ref[h] = (past & (rank < MOBA_TOPK)).astype(F32)

    qh = [qb[:, h * Dh:(h + 1) * Dh] for h in range(H)]

    def masked_scores(h, j, keep):
        k0 = pl.multiple_of(j * BL, BL)
        st = lax.dot_general(ks_ref[h, pl.ds(k0, BL), :], qh[h], nt,
                             preferred_element_type=F32)
        st = jnp.where(keep, st, -jnp.inf)
        s_ref[h, j] = st
        return jnp.max(st, axis=0, keepdims=True)

    own_mask = (lax.broadcasted_iota(jnp.int32, (BL, BL), 0) <=
                lax.broadcasted_iota(jnp.int32, (BL, BL), 1))
    m_own = tuple(masked_scores(h, i, own_mask) for h in range(H))

    def pass1(j, ms):
        return tuple(
            jnp.maximum(ms[h], masked_scores(
                h, j, sel_ref[h, pl.ds(j, 1), :] > 0.5)) for h in range(H))

    ms = lax.fori_loop(0, i, pass1, m_own)

    def pass2(j, carry):
        k0 = pl.multiple_of(j * BL, BL)
        out = []
        for h in range(H):
            p = jnp.exp(s_ref[h, j] - ms[h])
            l = carry[2 * h] + jnp.sum(p, axis=0, keepdims=True)
            acc = carry[2 * h + 1] + jnp.dot(
                vt_ref[h * Dh:(h + 1) * Dh, pl.ds(k0, BL)], p.astype(BF16),
                preferred_element_type=F32)
            out.extend((l, acc))
        return tuple(out)

    init = (jnp.zeros((1, BL), F32), jnp.zeros((Dh, BL), F32)) * H
    carry = lax.fori_loop(0, i + 1, pass2, init)
    out_t = jnp.concatenate(
        [carry[2 * h + 1] / carry[2 * h] for h in range(H)], axis=0)
    o_ref[...] = out_t.T


def _moba(q, k, v, tables, batch, seq):
    nb = seq // MOBA_BLOCK
    full = lambda b, i: (b, 0)
    qrow = lambda b, i: (b * nb + i, 0)
    trow = lambda b, i: (i, 0)
    const = lambda b, i: (0, 0)
    qspec = pl.BlockSpec((MOBA_BLOCK, ATTN_WIDTH), qrow)
    tq = pl.BlockSpec((MOBA_BLOCK, ATTN_WIDTH), trow)
    tk = pl.BlockSpec((seq, ATTN_WIDTH), const)
    kv = pl.BlockSpec((seq, ATTN_WIDTH), full)
    return pl.pallas_call(
        _moba_body,
        grid=(batch, nb),
        in_specs=[qspec, kv, kv, tq, tq, tq, tk, tk, tk],
        out_specs=qspec,
        out_shape=jax.ShapeDtypeStruct((batch * seq, ATTN_WIDTH), F32),
        scratch_shapes=[
            pltpu.VMEM((ATTN_HEADS, seq, ATTN_HEAD_DIM), BF16),
            pltpu.VMEM((ATTN_WIDTH, seq), BF16),
            pltpu.VMEM((ATTN_HEADS * nb, ATTN_WIDTH), F32),
            pltpu.VMEM((ATTN_HEADS, nb, MOBA_BLOCK), F32),
            pltpu.VMEM((ATTN_HEADS, nb, MOBA_BLOCK, MOBA_BLOCK), F32),
        ],
        compiler_params=_params("parallel", "arbitrary"),
        name="moba",
    )(q, k, v, *tables, *tables)


def _outproj_ffn_body(x_ref, ys_ref, yp_ref, ya_ref, w_ref, g_ref,
                      gpre_ref, wg_ref, wu_ref, wd_ref, gpost_ref, o_ref, h_ref):
    o1 = SSD_INNER
    o2 = SSD_INNER + POOL_WIDTH
    m = jnp.dot(ys_ref[...].astype(BF16), w_ref[0:o1, :],
                preferred_element_type=F32)
    m = m + jnp.dot(yp_ref[...].astype(BF16), w_ref[o1:o2, :],
                    preferred_element_type=F32)
    m = m + jnp.dot(ya_ref[...].astype(BF16), w_ref[o2:, :],
                    preferred_element_type=F32)
    x = x_ref[...] + _rms(m) * g_ref[...]
    o_ref[...] = _ffn_tile(x, gpre_ref, wg_ref, wu_ref, wd_ref, gpost_ref, h_ref)


def _outproj_ffn(x, y_ssd, y_pool, y_attn, w, g, gpre, wg, wu, wd, gpost, layer):
    m = x.shape[0]
    row = lambda i: (i, 0)
    return pl.pallas_call(
        _outproj_ffn_body,
        grid=(m // TOKEN_TILE,),
        in_specs=[
            pl.BlockSpec((TOKEN_TILE, D_MODEL), row),
            pl.BlockSpec((TOKEN_TILE, SSD_INNER), row),
            pl.BlockSpec((TOKEN_TILE, POOL_WIDTH), row),
            pl.BlockSpec((TOKEN_TILE, ATTN_WIDTH), row),
            _layer_spec((D_MODEL, D_MODEL), layer),
            _layer_spec((1, D_MODEL), layer),
            _layer_spec((1, D_MODEL), layer),
            _layer_spec((D_MODEL, D_FF), layer),
            _layer_spec((D_MODEL, D_FF), layer),
            _layer_spec((D_FF, D_MODEL), layer),
            _layer_spec((1, D_MODEL), layer),
        ],
        out_specs=pl.BlockSpec((TOKEN_TILE, D_MODEL), row),
        out_shape=jax.ShapeDtypeStruct((m, D_MODEL), F32),
        scratch_shapes=[pltpu.VMEM((TOKEN_TILE, D_FF), BF16)],
        compiler_params=_params("parallel"),
        name="outproj_ffn",
    )(x, y_ssd, y_pool, y_attn, w, g, gpre, wg, wu, wd, gpost)


def _rope_tables(seq):
    half = ROPE_DIM // 2
    inv_freq = ROPE_THETA ** (-jnp.arange(0, ROPE_DIM, 2, dtype=F32) / ROPE_DIM)
    ang = jnp.arange(seq, dtype=F32)[:, None] * inv_freq[None, :]
    rest = ATTN_HEAD_DIM - ROPE_DIM
    one = jnp.ones((seq, rest), F32)
    zero = jnp.zeros((seq, rest), F32)
    zh = jnp.zeros((seq, half), F32)
    cos = jnp.concatenate([jnp.cos(ang), jnp.cos(ang), one], axis=-1)
    sin_up = jnp.concatenate([zh, jnp.sin(ang), zero], axis=-1)
    sin_dn = jnp.concatenate([-jnp.sin(ang), zh, zero], axis=-1)
    return tuple(jnp.tile(t, (1, ATTN_HEADS)) for t in (cos, sin_up, sin_dn))


def _arrange_w_in(w):
    pad = [(0, 0)] * (w.ndim - 1) + [(0, DT_PAD - SSD_HEADS)]
    dt_cols = jnp.pad(w[..., OFF_DT:OFF_POOL], pad)
    return jnp.concatenate(
        [w[..., OFF_Z:OFF_XBC], w[..., OFF_XBC:OFF_DT], w[..., OFF_POOL:OFF_Q],
         w[..., OFF_Q:OFF_K], w[..., OFF_K:OFF_V], w[..., OFF_V:IN_COLS],
         dt_cols], axis=-1).astype(BF16)


def _per_head_lanes(p):
    return jnp.repeat(p, SSD_HEAD_DIM)[None, :]


def _per_head_cols(p):
    return jnp.pad(p, (0, DT_PAD - SSD_HEADS))[None, :]


def _pool_blockdiag(w):
    out = jnp.zeros((POOL_WIDTH, POOL_WIDTH), F32)
    for g in range(POOL_GROUPS):
        sl = slice(g * POOL_GROUP_DIM, (g + 1) * POOL_GROUP_DIM)
        out = out.at[sl, sl].set(w[g])
    return out.astype(BF16)


def kernel(x, ff1_norm_pre, ff1_w_gate, ff1_w_up, ff1_w_down, ff1_norm_post,
           mix_norm_pre, w_in, conv_w, conv_b, dt_bias, a_log, d_skip, ssd_norm,
           pool_w, pool_scale, w_out, mix_norm_post,
           ff2_norm_pre, ff2_w_gate, ff2_w_up, ff2_w_down, ff2_norm_post):
    batch, seq, d = x.shape
    depth = w_in.shape[0]
    h = x.reshape(batch * seq, d)
    rope_tables = _rope_tables(seq)
    row = lambda p: p[None, :]
    stack_row = lambda p: p[:, None, :]
    bf = lambda w: w.astype(BF16)
    ff1 = (stack_row(ff1_norm_pre), bf(ff1_w_gate), bf(ff1_w_up), bf(ff1_w_down),
           stack_row(ff1_norm_post))
    ff2 = (stack_row(ff2_norm_pre), bf(ff2_w_gate), bf(ff2_w_up), bf(ff2_w_down),
           stack_row(ff2_norm_post))
    w_in_b = _arrange_w_in(w_in)
    w_out_b = bf(w_out)
    g_mix_pre = stack_row(mix_norm_pre)
    g_mix_post = stack_row(mix_norm_post)
    for l in range(depth):
        h, z, xbc, u, q, k, v, dt = _ffn_inproj(h, *ff1, g_mix_pre, w_in_b, l)
        y_ssd = _ssd(xbc, z, dt, conv_w[l], row(conv_b[l]),
                     _per_head_cols(dt_bias[l]), _per_head_cols(a_log[l]),
                     _per_head_lanes(d_skip[l]), row(ssd_norm[l]), batch, seq)
        y_pool = _pool(u, _pool_blockdiag(pool_w[l]), row(pool_scale[l]),
                       batch, seq)
        y_attn = _moba(q, k, v, rope_tables, batch, seq)
        h = _outproj_ffn(h, y_ssd, y_pool, y_attn, w_out_b, g_mix_post, *ff2, l)
    return h.reshape(batch, seq, d)
```

```python
import functools
import math

import jax
import jax.numpy as jnp
from jax import lax
from jax.experimental import pallas as pl
from jax.experimental.pallas import tpu as pltpu

F32 = jnp.float32
BF16 = jnp.bfloat16
HIGHEST = lax.Precision.HIGHEST

D_MODEL = 1024
D_FF = 2816
SSD_INNER = 512
SSD_HEAD_DIM = 64
SSD_HEADS = SSD_INNER // SSD_HEAD_DIM
SSD_GROUPS = 2
SSD_STATE = 128
SSD_CONV = 4
SSD_CHUNK = 128
SSD_BC = SSD_GROUPS * SSD_STATE
SSD_XBC = SSD_INNER + 2 * SSD_BC
POOL_WIDTH = 256
POOL_GROUPS = 4
POOL_GROUP_DIM = POOL_WIDTH // POOL_GROUPS
ATTN_WIDTH = 256
ATTN_HEAD_DIM = 64
ATTN_HEADS = ATTN_WIDTH // ATTN_HEAD_DIM
ROPE_DIM = ATTN_HEAD_DIM // 4
ROPE_THETA = 500000.0
MOBA_BLOCK = 256
MOBA_TOPK = 3
RMS_EPS = 1e-6

OFF_Z = 0
OFF_XBC = OFF_Z + SSD_INNER
OFF_DT = OFF_XBC + SSD_XBC
OFF_POOL = OFF_DT + SSD_HEADS
OFF_Q = OFF_POOL + POOL_WIDTH
OFF_K = OFF_Q + ATTN_WIDTH
OFF_V = OFF_K + ATTN_WIDTH
IN_COLS = OFF_V + ATTN_WIDTH

LANES = 128
SUBLANES = 8
MXU_DIM = 256
VMEM_LIMIT_BYTES = 56 * 1024 * 1024

DT_PAD = LANES
TOKEN_TILE = 512
SUB_TILE = 256
FF_CHUNK = 2 * MXU_DIM
SSD_STEP_ROWS = 4 * SSD_CHUNK


def _rms(x):
    return x * lax.rsqrt(jnp.mean(x * x, axis=-1, keepdims=True) + RMS_EPS)


def _silu(x):
    return x * jax.nn.sigmoid(x)


def _softplus(x):
    return jnp.maximum(x, 0.0) + jnp.log1p(jnp.exp(-jnp.abs(x)))


def _const_spec(shape):
    zeros = (0,) * len(shape)
    return pl.BlockSpec(shape, lambda *_: zeros, pipeline_mode=pl.Buffered(1))


def _layer_spec(shape, layer):
    zeros = (0,) * len(shape)
    return pl.BlockSpec((None,) + shape, lambda *_: (layer,) + zeros,
                        pipeline_mode=pl.Buffered(1))


def _params(*sem):
    return pltpu.CompilerParams(dimension_semantics=sem,
                                vmem_limit_bytes=VMEM_LIMIT_BYTES)


def _ffn_gate_up(x, gpre_ref, wg_ref, wu_ref, h_ref):
    xb = (_rms(x) * gpre_ref[...]).astype(BF16)
    for c in range(0, D_FF, FF_CHUNK):
        sl = slice(c, min(c + FF_CHUNK, D_FF))
        g = jnp.dot(xb, wg_ref[:, sl], preferred_element_type=F32)
        u = jnp.dot(xb, wu_ref[:, sl], preferred_element_type=F32)
        h_ref[:, sl] = (_silu(g) * u).astype(BF16)


def _ffn_down(x, wd_ref, gpost_ref, h_ref):
    f = jnp.dot(h_ref[...], wd_ref[...], preferred_element_type=F32)
    return x + 0.5 * (_rms(f) * gpost_ref[...])


def _sub_tiles():
    return [slice(r0, r0 + SUB_TILE) for r0 in range(0, TOKEN_TILE, SUB_TILE)]


_PROJ_WIDTHS = (SSD_XBC, POOL_WIDTH, SSD_INNER, ATTN_WIDTH, ATTN_WIDTH,
                ATTN_WIDTH, DT_PAD)
_PROJ_COLS = sum(_PROJ_WIDTHS)
POOL_HALO = 2 ** POOL_GROUPS


def _pool_tile(u, tail, pos, w_ref, scale_ref):
    group = lax.broadcasted_iota(jnp.int32, u.shape, 1) // POOL_GROUP_DIM
    win_sum = jnp.concatenate([tail, u], axis=0)
    mean = jnp.zeros_like(u)
    for g in range(POOL_GROUPS):
        half = 2 ** g
        win_sum = win_sum + pltpu.roll(win_sum, half, 0)
        count = jnp.minimum(pos + 1, 2 * half).astype(F32)
        mean = jnp.where(group == g, win_sum[POOL_HALO:, :] / count, mean)
    d = (mean - u).astype(BF16)
    return jnp.dot(d, w_ref[...], preferred_element_type=F32) * scale_ref[...]


def _ffn_inproj_body(tiles_per_seq, x_ref, gpre_ref, wg_ref, wu_ref, wd_ref,
                     gpost_ref, g_ref, w_ref, poolw_ref, pscale_ref,
                     o_ref, xbc_ref, yp_ref, z_ref, q_ref, k_ref, v_ref, dt_ref,
                     h_ref, utail_ref):
    tile_in_seq = pl.program_id(0) % tiles_per_seq

    @pl.when(tile_in_seq == 0)
    def _():
        utail_ref[...] = jnp.zeros_like(utail_ref)

    offs = [0]
    for width in _PROJ_WIDTHS:
        offs.append(offs[-1] + width)
    tiles = _sub_tiles()

    def down(rows):
        x = _ffn_down(x_ref[rows, :], wd_ref, gpost_ref, h_ref.at[rows])
        o_ref[rows, :] = x
        return (_rms(x) * g_ref[...]).astype(BF16)

    def project(rows, xb, utail):
        def proj(i):
            return jnp.dot(xb, w_ref[:, offs[i]:offs[i + 1]],
                           preferred_element_type=F32)

        u = proj(1)
        xbc_ref[rows, :] = proj(0)
        for i, ref in enumerate((z_ref, q_ref, k_ref, v_ref, dt_ref), start=2):
            ref[rows, :] = proj(i)
        pos = (tile_in_seq * TOKEN_TILE + rows.start
               + lax.broadcasted_iota(jnp.int32, u.shape, 0))
        yp_ref[rows, :] = _pool_tile(u, utail, pos, poolw_ref, pscale_ref)
        return u[SUB_TILE - POOL_HALO:, :]

    utail = utail_ref[...]
    _ffn_gate_up(x_ref[tiles[0], :], gpre_ref, wg_ref, wu_ref, h_ref.at[tiles[0]])
    for i, rows in enumerate(tiles):
        xb = down(rows)
        if i + 1 < len(tiles):
            nxt = tiles[i + 1]
            _ffn_gate_up(x_ref[nxt, :], gpre_ref, wg_ref, wu_ref, h_ref.at[nxt])
        utail = project(rows, xb, utail)
    utail_ref[...] = utail


def _ffn_inproj(x, gpre, wg, wu, wd, gpost, g, w, poolw, pscale, layer, seq):
    m = x.shape[0]
    row = lambda i: (i, 0)
    widths = (D_MODEL,) + _PROJ_WIDTHS
    return pl.pallas_call(
        functools.partial(_ffn_inproj_body, seq // TOKEN_TILE),
        grid=(m // TOKEN_TILE,),
        in_specs=[
            pl.BlockSpec((TOKEN_TILE, D_MODEL), row),
            _layer_spec((1, D_MODEL), layer),
            _layer_spec((D_MODEL, D_FF), layer),
            _layer_spec((D_MODEL, D_FF), layer),
            _layer_spec((D_FF, D_MODEL), layer),
            _layer_spec((1, D_MODEL), layer),
            _layer_spec((1, D_MODEL), layer),
            _layer_spec((D_MODEL, _PROJ_COLS), layer),
            _layer_spec((POOL_WIDTH, POOL_WIDTH), layer),
            _layer_spec((1, POOL_WIDTH), layer),
        ],
        out_specs=[pl.BlockSpec((TOKEN_TILE, width), row) for width in widths],
        out_shape=[jax.ShapeDtypeStruct((m, width), F32) for width in widths],
        scratch_shapes=[
            pltpu.VMEM((TOKEN_TILE, D_FF), BF16),
            pltpu.VMEM((POOL_HALO, POOL_WIDTH), F32),
        ],
        compiler_params=_params("arbitrary"),
        name="ffn_inproj",
    )(x, gpre, wg, wu, wd, gpost, g, w, poolw, pscale)


def _causal_conv4(x, tail, w_ref, b_ref):
    x_ext = jnp.concatenate([tail, x], axis=0)
    prev_ext = pltpu.roll(x_ext, 1, 0)
    near = w_ref[3:4, :] * x + w_ref[2:3, :] * prev_ext[SUBLANES:, :]
    far_ext = w_ref[1:2, :] * x_ext + w_ref[0:1, :] * prev_ext
    return near + pltpu.roll(far_ext, 2, 0)[SUBLANES:, :] + b_ref[...]


def _heads_to_lanes(xc):
    rows = xc.shape[0]
    lane = lax.broadcasted_iota(jnp.int32, (rows, LANES), 1)
    per_tile = LANES // SSD_HEAD_DIM
    parts = []
    for t in range(SSD_INNER // LANES):
        tile = jnp.broadcast_to(xc[:, t * per_tile:t * per_tile + 1], (rows, LANES))
        for i in range(1, per_tile):
            h = t * per_tile + i
            tile = jnp.where(lane < i * SSD_HEAD_DIM, tile,
                             jnp.broadcast_to(xc[:, h:h + 1], (rows, LANES)))
        parts.append(tile)
    return jnp.concatenate(parts, axis=1)


def _ssd_body(xbc_ref, z_ref, dt_ref, convw_ref, convb_ref, dtb_ref, alog_ref,
              dskip_ref, gn_ref, o_ref, tail_ref, state_ref):
    L = SSD_CHUNK
    HP = SSD_INNER
    P = SSD_HEAD_DIM
    N = SSD_STATE
    GW = HP // SSD_GROUPS
    HPG = SSD_HEADS // SSD_GROUPS

    @pl.when(pl.program_id(1) == 0)
    def _():
        tail_ref[...] = jnp.zeros_like(tail_ref)
        state_ref[...] = jnp.zeros_like(state_ref)

    r = lax.broadcasted_iota(jnp.int32, (L, L), 0)
    s = lax.broadcasted_iota(jnp.int32, (L, L), 1)
    causal = s <= r
    tril = causal.astype(F32)
    pair_lane = lax.broadcasted_iota(jnp.int32, (L, 2 * P), 1)
    neg_a = -jnp.exp(alog_ref[...])

    def chunk(c, tail):
        r0 = pl.multiple_of(c * L, L)
        rows = pl.ds(r0, L)
        x = xbc_ref[rows, :]
        xc = _silu(_causal_conv4(x, tail, convw_ref, convb_ref))
        xs = xc[:, :HP]
        bm = xc[:, HP:HP + SSD_BC].astype(BF16)
        cm = xc[:, HP + SSD_BC:].astype(BF16)

        dt_c = _softplus(dt_ref[rows, :] + dtb_ref[...])
        acs_c = jnp.dot(tril, dt_c * neg_a, precision=HIGHEST,
                        preferred_element_type=F32)
        acs_t = acs_c.T
        dt = _heads_to_lanes(dt_c)
        acs = _heads_to_lanes(acs_c)
        a_last = acs[L - 1:L, :]
        decay_out = jnp.exp(acs)
        decay_in = jnp.exp(a_last - acs)
        chunk_decay = jnp.exp(a_last)

        xdt = xs * dt
        xdt_b = xdt.astype(BF16)
        xw_b = (xdt * decay_in).astype(BF16)

        y_parts = []
        for g in range(SSD_GROUPS):
            bg = bm[:, g * N:(g + 1) * N]
            cg = cm[:, g * N:(g + 1) * N]
            gsl = slice(g * GW, (g + 1) * GW)
            cb = lax.dot_general(cg, bg, (((1,), (1,)), ((), ())),
                                 preferred_element_type=F32)
            st = state_ref[:, gsl]
            y_off = jnp.dot(cg, st.astype(BF16), preferred_element_type=F32)
            new_st = lax.dot_general(bg, xw_b[:, gsl], (((0,), (0,)), ((), ())),
                                     preferred_element_type=F32)
            state_ref[:, gsl] = st * chunk_decay[:, gsl] + new_st
            for pair in range(HPG // 2):
                lo = g * GW + pair * 2 * P
                x_pair = xdt_b[:, lo:lo + 2 * P]
                ys = []
                for i in range(2):
                    h = lo // P + i
                    seg = jnp.exp(jnp.where(
                        causal, acs_c[:, h:h + 1] - acs_t[h:h + 1, :], -jnp.inf))
                    ys.append(jnp.dot((cb * seg).astype(BF16), x_pair,
                                      preferred_element_type=F32))
                y_diag = jnp.where(pair_lane < P, ys[0], ys[1])
                psl = slice(pair * 2 * P, (pair + 1) * 2 * P)
                y_parts.append(
                    y_diag + y_off[:, psl] * decay_out[:, lo:lo + 2 * P])
        y = jnp.concatenate(y_parts, axis=-1) + dskip_ref[...] * xs
        o_ref[rows, :] = _rms(y * _silu(z_ref[rows, :])) * gn_ref[...]
        return x[L - SUBLANES:, :]

    n_chunks = xbc_ref.shape[0] // L
    tail_ref[...] = lax.fori_loop(0, n_chunks, chunk, tail_ref[...], unroll=True)


def _ssd(xbc, z, dt, convw, convb, dtb, alog, dskip, gn, batch, seq):
    nblk = seq // SSD_STEP_ROWS
    row = lambda b, c: (b * nblk + c, 0)
    return pl.pallas_call(
        _ssd_body,
        grid=(batch, nblk),
        in_specs=[
            pl.BlockSpec((SSD_STEP_ROWS, SSD_XBC), row),
            pl.BlockSpec((SSD_STEP_ROWS, SSD_INNER), row),
            pl.BlockSpec((SSD_STEP_ROWS, DT_PAD), row),
            _const_spec((SSD_CONV, SSD_XBC)),
            _const_spec((1, SSD_XBC)),
            _const_spec((1, DT_PAD)),
            _const_spec((1, DT_PAD)),
            _const_spec((1, SSD_INNER)),
            _const_spec((1, SSD_INNER)),
        ],
        out_specs=pl.BlockSpec((SSD_STEP_ROWS, SSD_INNER), row),
        out_shape=jax.ShapeDtypeStruct((batch * seq, SSD_INNER), F32),
        scratch_shapes=[
            pltpu.VMEM((SUBLANES, SSD_XBC), F32),
            pltpu.VMEM((SSD_STATE, SSD_INNER), F32),
        ],
        compiler_params=_params("parallel", "arbitrary"),
        name="ssd",
    )(xbc, z, dt, convw, convb, dtb, alog, dskip, gn)


def _rope(t, cos, sin_up, sin_dn):
    half = ROPE_DIM // 2
    return (t * cos + pltpu.roll(t, half, 1) * sin_up
            + pltpu.roll(t, ATTN_WIDTH - half, 1) * sin_dn)


def _moba_body(q_ref, k_ref, v_ref, cosq_ref, supq_ref, sdnq_ref,
               cosk_ref, supk_ref, sdnk_ref, o_ref,
               ks_ref, vt_ref, kmean_ref, sel_ref, s_ref):
    seq = k_ref.shape[0]
    nb = seq // MOBA_BLOCK
    Dh = ATTN_HEAD_DIM
    BL = MOBA_BLOCK
    H = ATTN_HEADS
    i = pl.program_id(1)
    nt = (((1,), (1,)), ((), ()))

    head_of_lane = lax.broadcasted_iota(jnp.int32, (1, ATTN_WIDTH), 1) // Dh

    @pl.when(i == 0)
    def _():
        for j in range(nb):
            rows = slice(j * BL, (j + 1) * BL)
            kj = _rope(k_ref[rows, :], cosk_ref[rows, :], supk_ref[rows, :],
                       sdnk_ref[rows, :])
            kmean = jnp.mean(kj, axis=0, keepdims=True)
            for h in range(H):
                kmean_ref[h * nb + j:h * nb + j + 1, :] = jnp.where(
                    head_of_lane == h, kmean, 0.0)
            kb = kj.astype(BF16)
            for h in range(H):
                ks_ref[h, rows, :] = kb[:, h * Dh:(h + 1) * Dh]
            vt_ref[:, rows] = v_ref[rows, :].T.astype(BF16)

    qf = _rope(q_ref[...], cosq_ref[...], supq_ref[...], sdnq_ref[...])
    qb = (qf * (Dh ** -0.5)).astype(BF16)
    blk = lax.broadcasted_iota(jnp.int32, (nb, BL), 0)
    past = blk < i
    gates = lax.dot_general(kmean_ref[...], qf, nt, precision=HIGHEST,
                            preferred_element_type=F32)
    for h in range(H):
        gate = jnp.where(past, gates[h * nb:(h + 1) * nb, :], -jnp.inf)
        rank = jnp.zeros((nb, BL), jnp.int32)
        for j2 in range(nb):
            g2 = gate[j2:j2 + 1, :]
            ahead = (g2 > gate) | ((g2 == gate) & (j2 < blk))
            rank = rank + ahead.astype(jnp.int32)
        sel_ref[h] = (past & (rank < MOBA_TOPK)).astype(F32)

    qh = [qb[:, h * Dh:(h + 1) * Dh] for h in range(H)]

    def masked_scores(h, j, keep):
        k0 = pl.multiple_of(j * BL, BL)
        st = lax.dot_general(ks_ref[h, pl.ds(k0, BL), :], qh[h], nt,
                             preferred_element_type=F32)
        st = jnp.where(keep, st, -jnp.inf)
        s_ref[h, j] = st
        return jnp.max(st, axis=0, keepdims=True)

    own_mask = (lax.broadcasted_iota(jnp.int32, (BL, BL), 0) <=
                lax.broadcasted_iota(jnp.int32, (BL, BL), 1))
    m_own = tuple(masked_scores(h, i, own_mask) for h in range(H))

    def pass1(j, ms):
        return tuple(
            jnp.maximum(ms[h], masked_scores(
                h, j, sel_ref[h, pl.ds(j, 1), :] > 0.5)) for h in range(H))

    ms = lax.fori_loop(0, i, pass1, m_own)

    def pass2(j, carry):
        k0 = pl.multiple_of(j * BL, BL)
        out = []
        for h in range(H):
            p = jnp.exp(s_ref[h, j] - ms[h])
            l = carry[2 * h] + jnp.sum(p, axis=0, keepdims=True)
            acc = carry[2 * h + 1] + jnp.dot(
                vt_ref[h * Dh:(h + 1) * Dh, pl.ds(k0, BL)], p.astype(BF16),
                preferred_element_type=F32)
            out.extend((l, acc))
        return tuple(out)

    init = (jnp.zeros((1, BL), F32), jnp.zeros((Dh, BL), F32)) * H
    carry = lax.fori_loop(0, i + 1, pass2, init)
    out_t = jnp.concatenate(
        [carry[2 * h + 1] / carry[2 * h] for h in range(H)], axis=0)
    o_ref[...] = out_t.T


def _moba(q, k, v, tables, batch, seq):
    nb = seq // MOBA_BLOCK
    full = lambda b, i: (b, 0)
    qrow = lambda b, i: (b * nb + i, 0)
    trow = lambda b, i: (i, 0)
    const = lambda b, i: (0, 0)
    qspec = pl.BlockSpec((MOBA_BLOCK, ATTN_WIDTH), qrow)
    tq = pl.BlockSpec((MOBA_BLOCK, ATTN_WIDTH), trow)
    tk = pl.BlockSpec((seq, ATTN_WIDTH), const)
    kv = pl.BlockSpec((seq, ATTN_WIDTH), full)
    return pl.pallas_call(
        _moba_body,
        grid=(batch, nb),
        in_specs=[qspec, kv, kv, tq, tq, tq, tk, tk, tk],
        out_specs=qspec,
        out_shape=jax.ShapeDtypeStruct((batch * seq, ATTN_WIDTH), F32),
        scratch_shapes=[
            pltpu.VMEM((ATTN_HEADS, seq, ATTN_HEAD_DIM), BF16),
            pltpu.VMEM((ATTN_WIDTH, seq), BF16),
            pltpu.VMEM((ATTN_HEADS * nb, ATTN_WIDTH), F32),
            pltpu.VMEM((ATTN_HEADS, nb, MOBA_BLOCK), F32),
            pltpu.VMEM((ATTN_HEADS, nb, MOBA_BLOCK, MOBA_BLOCK), F32),
        ],
        compiler_params=_params("parallel", "arbitrary"),
        name="moba",
    )(q, k, v, *tables, *tables)


def _outproj_ffn_body(x_ref, ys_ref, yp_ref, ya_ref, w_ref, g_ref,
                      gpre_ref, wg_ref, wu_ref, wd_ref, gpost_ref, o_ref, h_ref):
    o1 = SSD_INNER
    o2 = SSD_INNER + POOL_WIDTH
    tiles = _sub_tiles()
    xs = []
    for rows in tiles:
        m = jnp.dot(ys_ref[rows, :].astype(BF16), w_ref[0:o1, :],
                    preferred_element_type=F32)
        m = m + jnp.dot(yp_ref[rows, :].astype(BF16), w_ref[o1:o2, :],
                        preferred_element_type=F32)
        m = m + jnp.dot(ya_ref[rows, :].astype(BF16), w_ref[o2:, :],
                        preferred_element_type=F32)
        xs.append(x_ref[rows, :] + _rms(m) * g_ref[...])
    for rows, x in zip(tiles, xs):
        _ffn_gate_up(x, gpre_ref, wg_ref, wu_ref, h_ref.at[rows])
        o_ref[rows, :] = _ffn_down(x, wd_ref, gpost_ref, h_ref.at[rows])


def _outproj_ffn(x, y_ssd, y_pool, y_attn, w, g, gpre, wg, wu, wd, gpost, layer):
    m = x.shape[0]
    row = lambda i: (i, 0)
    return pl.pallas_call(
        _outproj_ffn_body,
        grid=(m // TOKEN_TILE,),
        in_specs=[
            pl.BlockSpec((TOKEN_TILE, D_MODEL), row),
            pl.BlockSpec((TOKEN_TILE, SSD_INNER), row),
            pl.BlockSpec((TOKEN_TILE, POOL_WIDTH), row),
            pl.BlockSpec((TOKEN_TILE, ATTN_WIDTH), row),
            _layer_spec((D_MODEL, D_MODEL), layer),
            _layer_spec((1, D_MODEL), layer),
            _layer_spec((1, D_MODEL), layer),
            _layer_spec((D_MODEL, D_FF), layer),
            _layer_spec((D_MODEL, D_FF), layer),
            _layer_spec((D_FF, D_MODEL), layer),
            _layer_spec((1, D_MODEL), layer),
        ],
        out_specs=pl.BlockSpec((TOKEN_TILE, D_MODEL), row),
        out_shape=jax.ShapeDtypeStruct((m, D_MODEL), F32),
        scratch_shapes=[pltpu.VMEM((TOKEN_TILE, D_FF), BF16)],
        compiler_params=_params("parallel"),
        name="outproj_ffn",
    )(x, y_ssd, y_pool, y_attn, w, g, gpre, wg, wu, wd, gpost)


def _rope_tables(seq):
    half = ROPE_DIM // 2
    inv_freq = ROPE_THETA ** (-jnp.arange(0, ROPE_DIM, 2, dtype=F32) / ROPE_DIM)
    ang = jnp.arange(seq, dtype=F32)[:, None] * inv_freq[None, :]
    rest = ATTN_HEAD_DIM - ROPE_DIM
    one = jnp.ones((seq, rest), F32)
    zero = jnp.zeros((seq, rest), F32)
    zh = jnp.zeros((seq, half), F32)
    cos = jnp.concatenate([jnp.cos(ang), jnp.cos(ang), one], axis=-1)
    sin_up = jnp.concatenate([zh, jnp.sin(ang), zero], axis=-1)
    sin_dn = jnp.concatenate([-jnp.sin(ang), zh, zero], axis=-1)
    return tuple(jnp.tile(t, (1, ATTN_HEADS)) for t in (cos, sin_up, sin_dn))


def _arrange_w_in(w):
    pad = [(0, 0)] * (w.ndim - 1) + [(0, DT_PAD - SSD_HEADS)]
    dt_cols = jnp.pad(w[..., OFF_DT:OFF_POOL], pad)
    return jnp.concatenate(
        [w[..., OFF_XBC:OFF_DT], w[..., OFF_POOL:OFF_Q], w[..., OFF_Z:OFF_XBC],
         w[..., OFF_Q:OFF_K], w[..., OFF_K:OFF_V], w[..., OFF_V:IN_COLS],
         dt_cols], axis=-1).astype(BF16)


def _per_head_lanes(p):
    return jnp.repeat(p, SSD_HEAD_DIM)[None, :]


def _per_head_cols(p):
    pad = [(0, 0)] * (p.ndim - 1) + [(0, DT_PAD - SSD_HEADS)]
    return jnp.pad(p, pad)[..., None, :]


def _pool_blockdiag(w):
    out = jnp.zeros(w.shape[:-3] + (POOL_WIDTH, POOL_WIDTH), F32)
    for g in range(POOL_GROUPS):
        sl = slice(g * POOL_GROUP_DIM, (g + 1) * POOL_GROUP_DIM)
        out = out.at[..., sl, sl].set(w[..., g, :, :])
    return out.astype(BF16)


def kernel(x, ff1_norm_pre, ff1_w_gate, ff1_w_up, ff1_w_down, ff1_norm_post,
           mix_norm_pre, w_in, conv_w, conv_b, dt_bias, a_log, d_skip, ssd_norm,
           pool_w, pool_scale, w_out, mix_norm_post,
           ff2_norm_pre, ff2_w_gate, ff2_w_up, ff2_w_down, ff2_norm_post):
    batch, seq, d = x.shape
    depth = w_in.shape[0]
    h = x.reshape(batch * seq, d)
    rope_tables = _rope_tables(seq)
    row = lambda p: p[None, :]
    stack_row = lambda p: p[:, None, :]
    bf = lambda w: w.astype(BF16)
    ff1 = (stack_row(ff1_norm_pre), bf(ff1_w_gate), bf(ff1_w_up), bf(ff1_w_down),
           stack_row(ff1_norm_post))
    ff2 = (stack_row(ff2_norm_pre), bf(ff2_w_gate), bf(ff2_w_up), bf(ff2_w_down),
           stack_row(ff2_norm_post))
    w_in_b = _arrange_w_in(w_in)
    w_out_b = bf(w_out)
    g_mix_pre = stack_row(mix_norm_pre)
    g_mix_post = stack_row(mix_norm_post)
    mix_front = (g_mix_pre, w_in_b, _pool_blockdiag(pool_w), stack_row(pool_scale))
    for l in range(depth):
        h, xbc, y_pool, z, q, k, v, dt = _ffn_inproj(h, *ff1, *mix_front, l, seq)
        y_ssd = _ssd(xbc, z, dt, conv_w[l], row(conv_b[l]),
                     _per_head_cols(dt_bias[l]), _per_head_cols(a_log[l]),
                     _per_head_lanes(d_skip[l]), row(ssd_norm[l]), batch, seq)
        y_attn = _moba(q, k, v, rope_tables, batch, seq)
        h = _outproj_ffn(h, y_ssd, y_pool, y_attn, w_out_b, g_mix_post, *ff2, l)
    return h.reshape(batch, seq, d)
```

```python
import functools
import math

import jax
import jax.numpy as jnp
from jax import lax
from jax.experimental import pallas as pl
from jax.experimental.pallas import tpu as pltpu

F32 = jnp.float32
BF16 = jnp.bfloat16
HIGHEST = lax.Precision.HIGHEST

D_MODEL = 1024
D_FF = 2816
SSD_INNER = 512
SSD_HEAD_DIM = 64
SSD_HEADS = SSD_INNER // SSD_HEAD_DIM
SSD_GROUPS = 2
SSD_STATE = 128
SSD_CONV = 4
SSD_CHUNK = 128
SSD_BC = SSD_GROUPS * SSD_STATE
SSD_XBC = SSD_INNER + 2 * SSD_BC
POOL_WIDTH = 256
POOL_GROUPS = 4
POOL_GROUP_DIM = POOL_WIDTH // POOL_GROUPS
ATTN_WIDTH = 256
ATTN_HEAD_DIM = 64
ATTN_HEADS = ATTN_WIDTH // ATTN_HEAD_DIM
ROPE_DIM = ATTN_HEAD_DIM // 4
ROPE_THETA = 500000.0
MOBA_BLOCK = 256
MOBA_TOPK = 3
RMS_EPS = 1e-6

OFF_Z = 0
OFF_XBC = OFF_Z + SSD_INNER
OFF_DT = OFF_XBC + SSD_XBC
OFF_POOL = OFF_DT + SSD_HEADS
OFF_Q = OFF_POOL + POOL_WIDTH
OFF_K = OFF_Q + ATTN_WIDTH
OFF_V = OFF_K + ATTN_WIDTH
IN_COLS = OFF_V + ATTN_WIDTH

LANES = 128
SUBLANES = 8
MXU_DIM = 256
VMEM_LIMIT_BYTES = 56 * 1024 * 1024

DT_PAD = LANES
TOKEN_TILE = 512
OUT_TOKEN_TILE = 1024
SUB_TILE = 256
FF_CHUNK = 2 * MXU_DIM
SSD_STEP_ROWS = 4 * SSD_CHUNK
MOBA_UNROLLS = (4, 2, 1)


def _rms(x):
    return x * lax.rsqrt(jnp.mean(x * x, axis=-1, keepdims=True) + RMS_EPS)


def _silu(x):
    return x * jax.nn.sigmoid(x)


def _softplus(x):
    return jnp.maximum(x, 0.0) + jnp.log1p(jnp.exp(-jnp.abs(x)))


def _const_spec(shape):
    zeros = (0,) * len(shape)
    return pl.BlockSpec(shape, lambda *_: zeros, pipeline_mode=pl.Buffered(1))


def _layer_spec(shape, layer):
    zeros = (0,) * len(shape)
    return pl.BlockSpec((None,) + shape, lambda *_: (layer,) + zeros,
                        pipeline_mode=pl.Buffered(1))


def _params(*sem):
    return pltpu.CompilerParams(dimension_semantics=sem,
                                vmem_limit_bytes=VMEM_LIMIT_BYTES)


def _ffn_gate_up(x, gpre_ref, wg_ref, wu_ref, h_ref):
    xb = (_rms(x) * gpre_ref[...]).astype(BF16)
    for c in range(0, D_FF, FF_CHUNK):
        sl = slice(c, min(c + FF_CHUNK, D_FF))
        g = jnp.dot(xb, wg_ref[:, sl], preferred_element_type=F32)
        u = jnp.dot(xb, wu_ref[:, sl], preferred_element_type=F32)
        h_ref[:, sl] = (_silu(g) * u).astype(BF16)


def _ffn_down(x, wd_ref, gpost_ref, h_ref):
    f = jnp.dot(h_ref[...], wd_ref[...], preferred_element_type=F32)
    return x + 0.5 * (_rms(f) * gpost_ref[...])


def _sub_tiles(tile_rows):
    return [slice(r0, r0 + SUB_TILE) for r0 in range(0, tile_rows, SUB_TILE)]


_PROJ_WIDTHS = (SSD_XBC, POOL_WIDTH, SSD_INNER, ATTN_WIDTH, ATTN_WIDTH,
                ATTN_WIDTH, DT_PAD)
_PROJ_COLS = sum(_PROJ_WIDTHS)
POOL_HALO = 2 ** POOL_GROUPS


def _pool_tile(u, tail, pos, w_ref, scale_ref):
    group = lax.broadcasted_iota(jnp.int32, u.shape, 1) // POOL_GROUP_DIM
    win_sum = jnp.concatenate([tail, u], axis=0)
    mean = jnp.zeros_like(u)
    for g in range(POOL_GROUPS):
        half = 2 ** g
        win_sum = win_sum + pltpu.roll(win_sum, half, 0)
        count = jnp.minimum(pos + 1, 2 * half).astype(F32)
        mean = jnp.where(group == g, win_sum[POOL_HALO:, :] / count, mean)
    d = (mean - u).astype(BF16)
    return jnp.dot(d, w_ref[...], preferred_element_type=F32) * scale_ref[...]


def _ffn_inproj_body(tiles_per_seq, x_ref, gpre_ref, wg_ref, wu_ref, wd_ref,
                     gpost_ref, g_ref, w_ref, poolw_ref, pscale_ref,
                     o_ref, xbc_ref, yp_ref, z_ref, q_ref, k_ref, v_ref, dt_ref,
                     h_ref, utail_ref):
    tile_in_seq = pl.program_id(0) % tiles_per_seq

    @pl.when(tile_in_seq == 0)
    def _():
        utail_ref[...] = jnp.zeros_like(utail_ref)

    offs = [0]
    for width in _PROJ_WIDTHS:
        offs.append(offs[-1] + width)
    tiles = _sub_tiles(TOKEN_TILE)

    def down(rows):
        x = _ffn_down(x_ref[rows, :], wd_ref, gpost_ref, h_ref.at[rows])
        o_ref[rows, :] = x
        return (_rms(x) * g_ref[...]).astype(BF16)

    def project(rows, xb, utail):
        def proj(i):
            return jnp.dot(xb, w_ref[:, offs[i]:offs[i + 1]],
                           preferred_element_type=F32)

        u = proj(1)
        xbc_ref[rows, :] = proj(0)
        for i, ref in enumerate((z_ref, q_ref, k_ref, v_ref, dt_ref), start=2):
            ref[rows, :] = proj(i)
        pos = (tile_in_seq * TOKEN_TILE + rows.start
               + lax.broadcasted_iota(jnp.int32, u.shape, 0))
        yp_ref[rows, :] = _pool_tile(u, utail, pos, poolw_ref, pscale_ref)
        return u[SUB_TILE - POOL_HALO:, :]

    utail = utail_ref[...]
    _ffn_gate_up(x_ref[tiles[0], :], gpre_ref, wg_ref, wu_ref, h_ref.at[tiles[0]])
    for i, rows in enumerate(tiles):
        xb = down(rows)
        if i + 1 < len(tiles):
            nxt = tiles[i + 1]
            _ffn_gate_up(x_ref[nxt, :], gpre_ref, wg_ref, wu_ref, h_ref.at[nxt])
        utail = project(rows, xb, utail)
    utail_ref[...] = utail


def _ffn_inproj(x, gpre, wg, wu, wd, gpost, g, w, poolw, pscale, layer, seq):
    m = x.shape[0]
    row = lambda i: (i, 0)
    widths = (D_MODEL,) + _PROJ_WIDTHS
    return pl.pallas_call(
        functools.partial(_ffn_inproj_body, seq // TOKEN_TILE),
        grid=(m // TOKEN_TILE,),
        in_specs=[
            pl.BlockSpec((TOKEN_TILE, D_MODEL), row),
            _layer_spec((1, D_MODEL), layer),
            _layer_spec((D_MODEL, D_FF), layer),
            _layer_spec((D_MODEL, D_FF), layer),
            _layer_spec((D_FF, D_MODEL), layer),
            _layer_spec((1, D_MODEL), layer),
            _layer_spec((1, D_MODEL), layer),
            _layer_spec((D_MODEL, _PROJ_COLS), layer),
            _layer_spec((POOL_WIDTH, POOL_WIDTH), layer),
            _layer_spec((1, POOL_WIDTH), layer),
        ],
        out_specs=[pl.BlockSpec((TOKEN_TILE, width), row) for width in widths],
        out_shape=[jax.ShapeDtypeStruct((m, width), F32) for width in widths],
        scratch_shapes=[
            pltpu.VMEM((TOKEN_TILE, D_FF), BF16),
            pltpu.VMEM((POOL_HALO, POOL_WIDTH), F32),
        ],
        compiler_params=_params("arbitrary"),
        name="ffn_inproj",
    )(x, gpre, wg, wu, wd, gpost, g, w, poolw, pscale)


def _causal_conv4(x, tail, w_ref, b_ref):
    x_ext = jnp.concatenate([tail, x], axis=0)
    prev_ext = pltpu.roll(x_ext, 1, 0)
    near = w_ref[3:4, :] * x + w_ref[2:3, :] * prev_ext[SUBLANES:, :]
    far_ext = w_ref[1:2, :] * x_ext + w_ref[0:1, :] * prev_ext
    return near + pltpu.roll(far_ext, 2, 0)[SUBLANES:, :] + b_ref[...]


def _heads_to_lanes(xc):
    rows = xc.shape[0]
    lane = lax.broadcasted_iota(jnp.int32, (rows, LANES), 1)
    per_tile = LANES // SSD_HEAD_DIM
    parts = []
    for t in range(SSD_INNER // LANES):
        tile = jnp.broadcast_to(xc[:, t * per_tile:t * per_tile + 1], (rows, LANES))
        for i in range(1, per_tile):
            h = t * per_tile + i
            tile = jnp.where(lane < i * SSD_HEAD_DIM, tile,
                             jnp.broadcast_to(xc[:, h:h + 1], (rows, LANES)))
        parts.append(tile)
    return jnp.concatenate(parts, axis=1)


def _ssd_body(xbc_ref, z_ref, dt_ref, convw_ref, convb_ref, dtb_ref, alog_ref,
              dskip_ref, gn_ref, o_ref, tail_ref, state_ref):
    L = SSD_CHUNK
    HP = SSD_INNER
    P = SSD_HEAD_DIM
    N = SSD_STATE
    GW = HP // SSD_GROUPS
    HPG = SSD_HEADS // SSD_GROUPS

    @pl.when(pl.program_id(1) == 0)
    def _():
        tail_ref[...] = jnp.zeros_like(tail_ref)
        state_ref[...] = jnp.zeros_like(state_ref)

    r = lax.broadcasted_iota(jnp.int32, (L, L), 0)
    s = lax.broadcasted_iota(jnp.int32, (L, L), 1)
    causal = s <= r
    tril = causal.astype(F32)
    pair_lane = lax.broadcasted_iota(jnp.int32, (L, 2 * P), 1)
    neg_a = -jnp.exp(alog_ref[...])

    def chunk(c, tail):
        r0 = pl.multiple_of(c * L, L)
        rows = pl.ds(r0, L)
        x = xbc_ref[rows, :]
        xc = _silu(_causal_conv4(x, tail, convw_ref, convb_ref))
        xs = xc[:, :HP]
        bm = xc[:, HP:HP + SSD_BC].astype(BF16)
        cm = xc[:, HP + SSD_BC:].astype(BF16)

        dt_c = _softplus(dt_ref[rows, :] + dtb_ref[...])
        acs_c = jnp.dot(tril, dt_c * neg_a, precision=HIGHEST,
                        preferred_element_type=F32)
        acs_t = acs_c.T
        dt = _heads_to_lanes(dt_c)
        acs = _heads_to_lanes(acs_c)
        a_last = acs[L - 1:L, :]
        decay_out = jnp.exp(acs)
        decay_in = jnp.exp(a_last - acs)
        chunk_decay = jnp.exp(a_last)

        xdt = xs * dt
        xdt_b = xdt.astype(BF16)
        xw_b = (xdt * decay_in).astype(BF16)

        y_parts = []
        for g in range(SSD_GROUPS):
            bg = bm[:, g * N:(g + 1) * N]
            cg = cm[:, g * N:(g + 1) * N]
            gsl = slice(g * GW, (g + 1) * GW)
            cb = lax.dot_general(cg, bg, (((1,), (1,)), ((), ())),
                                 preferred_element_type=F32)
            st = state_ref[:, gsl]
            y_off = jnp.dot(cg, st.astype(BF16), preferred_element_type=F32)
            new_st = lax.dot_general(bg, xw_b[:, gsl], (((0,), (0,)), ((), ())),
                                     preferred_element_type=F32)
            state_ref[:, gsl] = st * chunk_decay[:, gsl] + new_st
            for pair in range(HPG // 2):
                lo = g * GW + pair * 2 * P
                x_pair = xdt_b[:, lo:lo + 2 * P]
                ys = []
                for i in range(2):
                    h = lo // P + i
                    seg = jnp.exp(jnp.where(
                        causal, acs_c[:, h:h + 1] - acs_t[h:h + 1, :], -jnp.inf))
                    ys.append(jnp.dot((cb * seg).astype(BF16), x_pair,
                                      preferred_element_type=F32))
                y_diag = jnp.where(pair_lane < P, ys[0], ys[1])
                psl = slice(pair * 2 * P, (pair + 1) * 2 * P)
                y_parts.append(
                    y_diag + y_off[:, psl] * decay_out[:, lo:lo + 2 * P])
        y = jnp.concatenate(y_parts, axis=-1) + dskip_ref[...] * xs
        o_ref[rows, :] = _rms(y * _silu(z_ref[rows, :])) * gn_ref[...]
        return x[L - SUBLANES:, :]

    n_chunks = xbc_ref.shape[0] // L
    tail_ref[...] = lax.fori_loop(0, n_chunks, chunk, tail_ref[...], unroll=True)


def _ssd(xbc, z, dt, convw, convb, dtb, alog, dskip, gn, batch, seq):
    nblk = seq // SSD_STEP_ROWS
    row = lambda b, c: (b * nblk + c, 0)
    return pl.pallas_call(
        _ssd_body,
        grid=(batch, nblk),
        in_specs=[
            pl.BlockSpec((SSD_STEP_ROWS, SSD_XBC), row),
            pl.BlockSpec((SSD_STEP_ROWS, SSD_INNER), row),
            pl.BlockSpec((SSD_STEP_ROWS, DT_PAD), row),
            _const_spec((SSD_CONV, SSD_XBC)),
            _const_spec((1, SSD_XBC)),
            _const_spec((1, DT_PAD)),
            _const_spec((1, DT_PAD)),
            _const_spec((1, SSD_INNER)),
            _const_spec((1, SSD_INNER)),
        ],
        out_specs=pl.BlockSpec((SSD_STEP_ROWS, SSD_INNER), row),
        out_shape=jax.ShapeDtypeStruct((batch * seq, SSD_INNER), F32),
        scratch_shapes=[
            pltpu.VMEM((SUBLANES, SSD_XBC), F32),
            pltpu.VMEM((SSD_STATE, SSD_INNER), F32),
        ],
        compiler_params=_params("parallel", "arbitrary"),
        name="ssd",
    )(xbc, z, dt, convw, convb, dtb, alog, dskip, gn)


def _rope(t, cos, sin_up, sin_dn):
    half = ROPE_DIM // 2
    return (t * cos + pltpu.roll(t, half, 1) * sin_up
            + pltpu.roll(t, ATTN_WIDTH - half, 1) * sin_dn)


def _moba_body(q_ref, k_ref, v_ref, cosq_ref, supq_ref, sdnq_ref,
               cosk_ref, supk_ref, sdnk_ref, o_ref,
               ks_ref, vt_ref, kmean_ref, sel_ref, s_ref):
    seq = k_ref.shape[0]
    nb = seq // MOBA_BLOCK
    Dh = ATTN_HEAD_DIM
    BL = MOBA_BLOCK
    H = ATTN_HEADS
    i = pl.program_id(1)
    nt = (((1,), (1,)), ((), ()))

    head_of_lane = lax.broadcasted_iota(jnp.int32, (1, ATTN_WIDTH), 1) // Dh

    @pl.when(i == 0)
    def _():
        for j in range(nb):
            rows = slice(j * BL, (j + 1) * BL)
            kj = _rope(k_ref[rows, :], cosk_ref[rows, :], supk_ref[rows, :],
                       sdnk_ref[rows, :])
            kmean = jnp.mean(kj, axis=0, keepdims=True)
            for h in range(H):
                kmean_ref[h * nb + j:h * nb + j + 1, :] = jnp.where(
                    head_of_lane == h, kmean, 0.0)
            kb = kj.astype(BF16)
            for h in range(H):
                ks_ref[h, rows, :] = kb[:, h * Dh:(h + 1) * Dh]
            vt_ref[:, rows] = v_ref[rows, :].T.astype(BF16)

    qf = _rope(q_ref[...], cosq_ref[...], supq_ref[...], sdnq_ref[...])
    qb = (qf * (Dh ** -0.5)).astype(BF16)
    blk = lax.broadcasted_iota(jnp.int32, (nb, BL), 0)
    past = blk < i
    gates = lax.dot_general(kmean_ref[...], qf, nt, precision=HIGHEST,
                            preferred_element_type=F32)
    for h in range(H):
        gate = jnp.where(past, gates[h * nb:(h + 1) * nb, :], -jnp.inf)
        rank = jnp.zeros((nb, BL), jnp.int32)
        for j2 in range(nb):
            g2 = gate[j2:j2 + 1, :]
            ahead = (g2 > gate) | ((g2 == gate) & (j2 < blk))
            rank = rank + ahead.astype(jnp.int32)
        sel_ref[h] = (past & (rank < MOBA_TOPK)).astype(F32)

    qh = [qb[:, h * Dh:(h + 1) * Dh] for h in range(H)]

    def masked_scores(h, j, keep):
        k0 = pl.multiple_of(j * BL, BL)
        st = lax.dot_general(ks_ref[h, pl.ds(k0, BL), :], qh[h], nt,
                             preferred_element_type=F32)
        st = jnp.where(keep, st, -jnp.inf)
        s_ref[h, j] = st
        return jnp.max(st, axis=0, keepdims=True)

    own_mask = (lax.broadcasted_iota(jnp.int32, (BL, BL), 0) <=
                lax.broadcasted_iota(jnp.int32, (BL, BL), 1))
    m_own = tuple(masked_scores(h, i, own_mask) for h in range(H))

    def pass1(j, ms):
        return tuple(
            jnp.maximum(ms[h], masked_scores(
                h, j, sel_ref[h, pl.ds(j, 1), :] > 0.5)) for h in range(H))

    def over_blocks(step, n, carry):
        start = 0
        for width in MOBA_UNROLLS:
            def body(jj, carry, start=start, width=width):
                for t in range(width):
                    carry = step(start + width * jj + t, carry)
                return carry
            groups = lax.div(n - start, width)
            carry = lax.fori_loop(0, groups, body, carry)
            start = start + groups * width
        return carry

    ms = over_blocks(pass1, i, m_own)

    def pass2(j, carry):
        k0 = pl.multiple_of(j * BL, BL)
        out = []
        for h in range(H):
            p = jnp.exp(s_ref[h, j] - ms[h])
            l = carry[2 * h] + jnp.sum(p, axis=0, keepdims=True)
            acc = carry[2 * h + 1] + jnp.dot(
                vt_ref[h * Dh:(h + 1) * Dh, pl.ds(k0, BL)], p.astype(BF16),
                preferred_element_type=F32)
            out.extend((l, acc))
        return tuple(out)

    init = (jnp.zeros((1, BL), F32), jnp.zeros((Dh, BL), F32)) * H
    carry = over_blocks(pass2, i + 1, init)
    out_t = jnp.concatenate(
        [carry[2 * h + 1] / carry[2 * h] for h in range(H)], axis=0)
    o_ref[...] = out_t.T


def _moba(q, k, v, tables, batch, seq):
    nb = seq // MOBA_BLOCK
    full = lambda b, i: (b, 0)
    qrow = lambda b, i: (b * nb + i, 0)
    trow = lambda b, i: (i, 0)
    const = lambda b, i: (0, 0)
    qspec = pl.BlockSpec((MOBA_BLOCK, ATTN_WIDTH), qrow)
    tq = pl.BlockSpec((MOBA_BLOCK, ATTN_WIDTH), trow)
    tk = pl.BlockSpec((seq, ATTN_WIDTH), const)
    kv = pl.BlockSpec((seq, ATTN_WIDTH), full)
    return pl.pallas_call(
        _moba_body,
        grid=(batch, nb),
        in_specs=[qspec, kv, kv, tq, tq, tq, tk, tk, tk],
        out_specs=qspec,
        out_shape=jax.ShapeDtypeStruct((batch * seq, ATTN_WIDTH), F32),
        scratch_shapes=[
            pltpu.VMEM((ATTN_HEADS, seq, ATTN_HEAD_DIM), BF16),
            pltpu.VMEM((ATTN_WIDTH, seq), BF16),
            pltpu.VMEM((ATTN_HEADS * nb, ATTN_WIDTH), F32),
            pltpu.VMEM((ATTN_HEADS, nb, MOBA_BLOCK), F32),
            pltpu.VMEM((ATTN_HEADS, nb, MOBA_BLOCK, MOBA_BLOCK), F32),
        ],
        compiler_params=_params("parallel", "arbitrary"),
        name="moba",
    )(q, k, v, *tables, *tables)


def _outproj_ffn_body(x_ref, ys_ref, yp_ref, ya_ref, w_ref, g_ref,
                      gpre_ref, wg_ref, wu_ref, wd_ref, gpost_ref, o_ref, h_ref):
    o1 = SSD_INNER
    o2 = SSD_INNER + POOL_WIDTH
    tiles = _sub_tiles(OUT_TOKEN_TILE)
    xs = []
    for rows in tiles:
        m = jnp.dot(ys_ref[rows, :].astype(BF16), w_ref[0:o1, :],
                    preferred_element_type=F32)
        m = m + jnp.dot(yp_ref[rows, :].astype(BF16), w_ref[o1:o2, :],
                        preferred_element_type=F32)
        m = m + jnp.dot(ya_ref[rows, :].astype(BF16), w_ref[o2:, :],
                        preferred_element_type=F32)
        xs.append(x_ref[rows, :] + _rms(m) * g_ref[...])
    for rows, x in zip(tiles, xs):
        _ffn_gate_up(x, gpre_ref, wg_ref, wu_ref, h_ref.at[rows])
        o_ref[rows, :] = _ffn_down(x, wd_ref, gpost_ref, h_ref.at[rows])


def _outproj_ffn(x, y_ssd, y_pool, y_attn, w, g, gpre, wg, wu, wd, gpost, layer):
    m = x.shape[0]
    row = lambda i: (i, 0)
    return pl.pallas_call(
        _outproj_ffn_body,
        grid=(m // OUT_TOKEN_TILE,),
        in_specs=[
            pl.BlockSpec((OUT_TOKEN_TILE, D_MODEL), row),
            pl.BlockSpec((OUT_TOKEN_TILE, SSD_INNER), row),
            pl.BlockSpec((OUT_TOKEN_TILE, POOL_WIDTH), row),
            pl.BlockSpec((OUT_TOKEN_TILE, ATTN_WIDTH), row),
            _layer_spec((D_MODEL, D_MODEL), layer),
            _layer_spec((1, D_MODEL), layer),
            _layer_spec((1, D_MODEL), layer),
            _layer_spec((D_MODEL, D_FF), layer),
            _layer_spec((D_MODEL, D_FF), layer),
            _layer_spec((D_FF, D_MODEL), layer),
            _layer_spec((1, D_MODEL), layer),
        ],
        out_specs=pl.BlockSpec((OUT_TOKEN_TILE, D_MODEL), row),
        out_shape=jax.ShapeDtypeStruct((m, D_MODEL), F32),
        scratch_shapes=[pltpu.VMEM((OUT_TOKEN_TILE, D_FF), BF16)],
        compiler_params=_params("parallel"),
        name="outproj_ffn",
    )(x, y_ssd, y_pool, y_attn, w, g, gpre, wg, wu, wd, gpost)


def _rope_tables(seq):
    half = ROPE_DIM // 2
    inv_freq = ROPE_THETA ** (-jnp.arange(0, ROPE_DIM, 2, dtype=F32) / ROPE_DIM)
    ang = jnp.arange(seq, dtype=F32)[:, None] * inv_freq[None, :]
    rest = ATTN_HEAD_DIM - ROPE_DIM
    one = jnp.ones((seq, rest), F32)
    zero = jnp.zeros((seq, rest), F32)
    zh = jnp.zeros((seq, half), F32)
    cos = jnp.concatenate([jnp.cos(ang), jnp.cos(ang), one], axis=-1)
    sin_up = jnp.concatenate([zh, jnp.sin(ang), zero], axis=-1)
    sin_dn = jnp.concatenate([-jnp.sin(ang), zh, zero], axis=-1)
    return tuple(jnp.tile(t, (1, ATTN_HEADS)) for t in (cos, sin_up, sin_dn))


def _arrange_w_in(w):
    pad = [(0, 0)] * (w.ndim - 1) + [(0, DT_PAD - SSD_HEADS)]
    dt_cols = jnp.pad(w[..., OFF_DT:OFF_POOL], pad)
    return jnp.concatenate(
        [w[..., OFF_XBC:OFF_DT], w[..., OFF_POOL:OFF_Q], w[..., OFF_Z:OFF_XBC],
         w[..., OFF_Q:OFF_K], w[..., OFF_K:OFF_V], w[..., OFF_V:IN_COLS],
         dt_cols], axis=-1).astype(BF16)


def _per_head_lanes(p):
    return jnp.repeat(p, SSD_HEAD_DIM)[None, :]


def _per_head_cols(p):
    pad = [(0, 0)] * (p.ndim - 1) + [(0, DT_PAD - SSD_HEADS)]
    return jnp.pad(p, pad)[..., None, :]


def _pool_blockdiag(w):
    out = jnp.zeros(w.shape[:-3] + (POOL_WIDTH, POOL_WIDTH), F32)
    for g in range(POOL_GROUPS):
        sl = slice(g * POOL_GROUP_DIM, (g + 1) * POOL_GROUP_DIM)
        out = out.at[..., sl, sl].set(w[..., g, :, :])
    return out.astype(BF16)


def kernel(x, ff1_norm_pre, ff1_w_gate, ff1_w_up, ff1_w_down, ff1_norm_post,
           mix_norm_pre, w_in, conv_w, conv_b, dt_bias, a_log, d_skip, ssd_norm,
           pool_w, pool_scale, w_out, mix_norm_post,
           ff2_norm_pre, ff2_w_gate, ff2_w_up, ff2_w_down, ff2_norm_post):
    batch, seq, d = x.shape
    depth = w_in.shape[0]
    h = x.reshape(batch * seq, d)
    rope_tables = _rope_tables(seq)
    row = lambda p: p[None, :]
    stack_row = lambda p: p[:, None, :]
    bf = lambda w: w.astype(BF16)
    ff1 = (stack_row(ff1_norm_pre), bf(ff1_w_gate), bf(ff1_w_up), bf(ff1_w_down),
           stack_row(ff1_norm_post))
    ff2 = (stack_row(ff2_norm_pre), bf(ff2_w_gate), bf(ff2_w_up), bf(ff2_w_down),
           stack_row(ff2_norm_post))
    w_in_b = _arrange_w_in(w_in)
    w_out_b = bf(w_out)
    g_mix_pre = stack_row(mix_norm_pre)
    g_mix_post = stack_row(mix_norm_post)
    mix_front = (g_mix_pre, w_in_b, _pool_blockdiag(pool_w), stack_row(pool_scale))
    for l in range(depth):
        h, xbc, y_pool, z, q, k, v, dt = _ffn_inproj(h, *ff1, *mix_front, l, seq)
        y_ssd = _ssd(xbc, z, dt, conv_w[l], row(conv_b[l]),
                     _per_head_cols(dt_bias[l]), _per_head_cols(a_log[l]),
                     _per_head_lanes(d_skip[l]), row(ssd_norm[l]), batch, seq)
        y_attn = _moba(q, k, v, rope_tables, batch, seq)
        h = _outproj_ffn(h, y_ssd, y_pool, y_attn, w_out_b, g_mix_post, *ff2, l)
    return h.reshape(batch, seq, d)
```

```python
import functools
import math

import jax
import jax.numpy as jnp
from jax import lax
from jax.experimental import pallas as pl
from jax.experimental.pallas import tpu as pltpu

F32 = jnp.float32
BF16 = jnp.bfloat16
HIGHEST = lax.Precision.HIGHEST

D_MODEL = 1024
D_FF = 2816
SSD_INNER = 512
SSD_HEAD_DIM = 64
SSD_HEADS = SSD_INNER // SSD_HEAD_DIM
SSD_GROUPS = 2
SSD_STATE = 128
SSD_CONV = 4
SSD_CHUNK = 128
SSD_BC = SSD_GROUPS * SSD_STATE
SSD_XBC = SSD_INNER + 2 * SSD_BC
POOL_WIDTH = 256
POOL_GROUPS = 4
POOL_GROUP_DIM = POOL_WIDTH // POOL_GROUPS
ATTN_WIDTH = 256
ATTN_HEAD_DIM = 64
ATTN_HEADS = ATTN_WIDTH // ATTN_HEAD_DIM
ROPE_DIM = ATTN_HEAD_DIM // 4
ROPE_THETA = 500000.0
MOBA_BLOCK = 256
MOBA_TOPK = 3
RMS_EPS = 1e-6

OFF_Z = 0
OFF_XBC = OFF_Z + SSD_INNER
OFF_DT = OFF_XBC + SSD_XBC
OFF_POOL = OFF_DT + SSD_HEADS
OFF_Q = OFF_POOL + POOL_WIDTH
OFF_K = OFF_Q + ATTN_WIDTH
OFF_V = OFF_K + ATTN_WIDTH
IN_COLS = OFF_V + ATTN_WIDTH

LANES = 128
SUBLANES = 8
MXU_DIM = 256
VMEM_LIMIT_BYTES = 56 * 1024 * 1024

DT_PAD = LANES
TOKEN_TILE = 512
OUT_TOKEN_TILE = 1024
SUB_TILE = 256
FF_CHUNK = MXU_DIM
SSD_STEP_ROWS = 8 * SSD_CHUNK
MOBA_UNROLLS = (4, 2, 1)
MOBA_Q_PER_STEP = 2


def _rms(x):
    return x * lax.rsqrt(jnp.mean(x * x, axis=-1, keepdims=True) + RMS_EPS)


def _silu(x):
    return x * jax.nn.sigmoid(x)


def _softplus(x):
    return jnp.maximum(x, 0.0) + jnp.log1p(jnp.exp(-jnp.abs(x)))


def _const_spec(shape):
    zeros = (0,) * len(shape)
    return pl.BlockSpec(shape, lambda *_: zeros, pipeline_mode=pl.Buffered(1))


def _layer_spec(shape, layer):
    zeros = (0,) * len(shape)
    return pl.BlockSpec((None,) + shape, lambda *_: (layer,) + zeros,
                        pipeline_mode=pl.Buffered(1))


def _params(*sem):
    return pltpu.CompilerParams(dimension_semantics=sem,
                                vmem_limit_bytes=VMEM_LIMIT_BYTES)


def _ffn_gate_up(x, gpre_ref, wg_ref, wu_ref, h_ref):
    xb = (_rms(x) * gpre_ref[...]).astype(BF16)
    for c in range(0, D_FF, FF_CHUNK):
        sl = slice(c, min(c + FF_CHUNK, D_FF))
        g = jnp.dot(xb, wg_ref[:, sl], preferred_element_type=F32)
        u = jnp.dot(xb, wu_ref[:, sl], preferred_element_type=F32)
        h_ref[:, sl] = (_silu(g) * u).astype(BF16)


def _ffn_down(x, wd_ref, gpost_ref, h_ref):
    f = jnp.dot(h_ref[...], wd_ref[...], preferred_element_type=F32)
    return x + 0.5 * (_rms(f) * gpost_ref[...])


def _sub_tiles(tile_rows):
    return [slice(r0, r0 + SUB_TILE) for r0 in range(0, tile_rows, SUB_TILE)]


_PROJ_WIDTHS = (SSD_XBC, POOL_WIDTH, SSD_INNER, ATTN_WIDTH, ATTN_WIDTH,
                ATTN_WIDTH, DT_PAD)
POOL_HALO = 2 ** POOL_GROUPS


def _pool_tile(u, tail, pos, w_ref, scale_ref):
    group = lax.broadcasted_iota(jnp.int32, u.shape, 1) // POOL_GROUP_DIM
    win_sum = jnp.concatenate([tail, u], axis=0)
    mean = jnp.zeros_like(u)
    for g in range(POOL_GROUPS):
        half = 2 ** g
        win_sum = win_sum + pltpu.roll(win_sum, half, 0)
        count = jnp.minimum(pos + 1, 2 * half).astype(F32)
        mean = jnp.where(group == g, win_sum[POOL_HALO:, :] / count, mean)
    d = (mean - u).astype(BF16)
    return jnp.dot(d, w_ref[...], preferred_element_type=F32) * scale_ref[...]


def _ffn_inproj_body(tiles_per_seq, x_ref, gpre_ref, wg_ref, wu_ref, wd_ref,
                     gpost_ref, g_ref, *rest):
    n_proj = len(_PROJ_WIDTHS)
    w_refs = rest[:n_proj]
    (poolw_ref, pscale_ref, o_ref, xbc_ref, yp_ref, z_ref, q_ref, k_ref, v_ref,
     dt_ref, h_ref, utail_ref) = rest[n_proj:]
    tile_in_seq = pl.program_id(0) % tiles_per_seq

    @pl.when(tile_in_seq == 0)
    def _():
        utail_ref[...] = jnp.zeros_like(utail_ref)

    tiles = _sub_tiles(TOKEN_TILE)

    def down(rows):
        x = _ffn_down(x_ref[rows, :], wd_ref, gpost_ref, h_ref.at[rows])
        o_ref[rows, :] = x
        return (_rms(x) * g_ref[...]).astype(BF16)

    def project(rows, xb, utail):
        def proj(i):
            return jnp.dot(xb, w_refs[i][...], preferred_element_type=F32)

        u = proj(1)
        xbc_ref[rows, :] = proj(0)
        for i, ref in enumerate((z_ref, q_ref, k_ref, v_ref, dt_ref), start=2):
            ref[rows, :] = proj(i)
        pos = (tile_in_seq * TOKEN_TILE + rows.start
               + lax.broadcasted_iota(jnp.int32, u.shape, 0))
        yp_ref[rows, :] = _pool_tile(u, utail, pos, poolw_ref, pscale_ref)
        return u[SUB_TILE - POOL_HALO:, :]

    utail = utail_ref[...]
    _ffn_gate_up(x_ref[tiles[0], :], gpre_ref, wg_ref, wu_ref, h_ref.at[tiles[0]])
    for i, rows in enumerate(tiles):
        xb = down(rows)
        if i + 1 < len(tiles):
            nxt = tiles[i + 1]
            _ffn_gate_up(x_ref[nxt, :], gpre_ref, wg_ref, wu_ref, h_ref.at[nxt])
        utail = project(rows, xb, utail)
    utail_ref[...] = utail


def _ffn_inproj(x, gpre, wg, wu, wd, gpost, g, ws, poolw, pscale, layer, seq):
    m = x.shape[0]
    row = lambda i: (i, 0)
    widths = (D_MODEL,) + _PROJ_WIDTHS
    return pl.pallas_call(
        functools.partial(_ffn_inproj_body, seq // TOKEN_TILE),
        grid=(m // TOKEN_TILE,),
        in_specs=[
            pl.BlockSpec((TOKEN_TILE, D_MODEL), row),
            _layer_spec((1, D_MODEL), layer),
            _layer_spec((D_MODEL, D_FF), layer),
            _layer_spec((D_MODEL, D_FF), layer),
            _layer_spec((D_FF, D_MODEL), layer),
            _layer_spec((1, D_MODEL), layer),
            _layer_spec((1, D_MODEL), layer),
            *[_layer_spec((D_MODEL, width), layer) for width in _PROJ_WIDTHS],
            _layer_spec((POOL_WIDTH, POOL_WIDTH), layer),
            _layer_spec((1, POOL_WIDTH), layer),
        ],
        out_specs=[pl.BlockSpec((TOKEN_TILE, width), row) for width in widths],
        out_shape=[jax.ShapeDtypeStruct((m, width), F32) for width in widths],
        scratch_shapes=[
            pltpu.VMEM((TOKEN_TILE, D_FF), BF16),
            pltpu.VMEM((POOL_HALO, POOL_WIDTH), F32),
        ],
        compiler_params=_params("arbitrary"),
        name="ffn_inproj",
    )(x, gpre, wg, wu, wd, gpost, g, *ws, poolw, pscale)


def _causal_conv4(x, tail, w_ref, b_ref):
    x_ext = jnp.concatenate([tail, x], axis=0)
    prev_ext = pltpu.roll(x_ext, 1, 0)
    near = w_ref[3:4, :] * x + w_ref[2:3, :] * prev_ext[SUBLANES:, :]
    far_ext = w_ref[1:2, :] * x_ext + w_ref[0:1, :] * prev_ext
    return near + pltpu.roll(far_ext, 2, 0)[SUBLANES:, :] + b_ref[...]


def _heads_to_lanes(xc):
    rows = xc.shape[0]
    lane = lax.broadcasted_iota(jnp.int32, (rows, LANES), 1)
    per_tile = LANES // SSD_HEAD_DIM
    parts = []
    for t in range(SSD_INNER // LANES):
        tile = jnp.broadcast_to(xc[:, t * per_tile:t * per_tile + 1], (rows, LANES))
        for i in range(1, per_tile):
            h = t * per_tile + i
            tile = jnp.where(lane < i * SSD_HEAD_DIM, tile,
                             jnp.broadcast_to(xc[:, h:h + 1], (rows, LANES)))
        parts.append(tile)
    return jnp.concatenate(parts, axis=1)


def _ssd_body(xbc_ref, z_ref, dt_ref, convw_ref, convb_ref, dtb_ref, alog_ref,
              dskip_ref, gn_ref, o_ref, tail_ref, state_ref):
    L = SSD_CHUNK
    HP = SSD_INNER
    P = SSD_HEAD_DIM
    N = SSD_STATE
    GW = HP // SSD_GROUPS
    HPG = SSD_HEADS // SSD_GROUPS

    @pl.when(pl.program_id(1) == 0)
    def _():
        tail_ref[...] = jnp.zeros_like(tail_ref)
        state_ref[...] = jnp.zeros_like(state_ref)

    r = lax.broadcasted_iota(jnp.int32, (L, L), 0)
    s = lax.broadcasted_iota(jnp.int32, (L, L), 1)
    causal = s <= r
    tril = causal.astype(F32)
    pair_lane = lax.broadcasted_iota(jnp.int32, (L, 2 * P), 1)
    neg_a = -jnp.exp(alog_ref[...])

    def chunk(c, tail):
        r0 = pl.multiple_of(c * L, L)
        rows = pl.ds(r0, L)
        x = xbc_ref[rows, :]
        xc = _silu(_causal_conv4(x, tail, convw_ref, convb_ref))
        xs = xc[:, :HP]
        bm = xc[:, HP:HP + SSD_BC].astype(BF16)
        cm = xc[:, HP + SSD_BC:].astype(BF16)

        dt_c = _softplus(dt_ref[rows, :] + dtb_ref[...])
        acs_c = jnp.dot(tril, dt_c * neg_a, precision=HIGHEST,
                        preferred_element_type=F32)
        acs_t = acs_c.T
        dt = _heads_to_lanes(dt_c)
        acs = _heads_to_lanes(acs_c)
        a_last = acs[L - 1:L, :]
        decay_out = jnp.exp(acs)
        decay_in = jnp.exp(a_last - acs)
        chunk_decay = jnp.exp(a_last)

        xdt = xs * dt
        xdt_b = xdt.astype(BF16)
        xw_b = (xdt * decay_in).astype(BF16)

        y_parts = []
        for g in range(SSD_GROUPS):
            bg = bm[:, g * N:(g + 1) * N]
            cg = cm[:, g * N:(g + 1) * N]
            gsl = slice(g * GW, (g + 1) * GW)
            cb = lax.dot_general(cg, bg, (((1,), (1,)), ((), ())),
                                 preferred_element_type=F32)
            st = state_ref[:, gsl]
            y_off = jnp.dot(cg, st.astype(BF16), preferred_element_type=F32)
            new_st = lax.dot_general(bg, xw_b[:, gsl], (((0,), (0,)), ((), ())),
                                     preferred_element_type=F32)
            state_ref[:, gsl] = st * chunk_decay[:, gsl] + new_st
            for pair in range(HPG // 2):
                lo = g * GW + pair * 2 * P
                x_pair = xdt_b[:, lo:lo + 2 * P]
                ys = []
                for i in range(2):
                    h = lo // P + i
                    seg = jnp.exp(jnp.where(
                        causal, acs_c[:, h:h + 1] - acs_t[h:h + 1, :], -jnp.inf))
                    ys.append(jnp.dot((cb * seg).astype(BF16), x_pair,
                                      preferred_element_type=F32))
                y_diag = jnp.where(pair_lane < P, ys[0], ys[1])
                psl = slice(pair * 2 * P, (pair + 1) * 2 * P)
                y_parts.append(
                    y_diag + y_off[:, psl] * decay_out[:, lo:lo + 2 * P])
        y = jnp.concatenate(y_parts, axis=-1) + dskip_ref[...] * xs
        o_ref[rows, :] = _rms(y * _silu(z_ref[rows, :])) * gn_ref[...]
        return x[L - SUBLANES:, :]

    n_chunks = xbc_ref.shape[0] // L
    tail_ref[...] = lax.fori_loop(0, n_chunks, chunk, tail_ref[...], unroll=True)


def _ssd(xbc, z, dt, convw, convb, dtb, alog, dskip, gn, batch, seq):
    nblk = seq // SSD_STEP_ROWS
    row = lambda b, c: (b * nblk + c, 0)
    return pl.pallas_call(
        _ssd_body,
        grid=(batch, nblk),
        in_specs=[
            pl.BlockSpec((SSD_STEP_ROWS, SSD_XBC), row),
            pl.BlockSpec((SSD_STEP_ROWS, SSD_INNER), row),
            pl.BlockSpec((SSD_STEP_ROWS, DT_PAD), row),
            _const_spec((SSD_CONV, SSD_XBC)),
            _const_spec((1, SSD_XBC)),
            _const_spec((1, DT_PAD)),
            _const_spec((1, DT_PAD)),
            _const_spec((1, SSD_INNER)),
            _const_spec((1, SSD_INNER)),
        ],
        out_specs=pl.BlockSpec((SSD_STEP_ROWS, SSD_INNER), row),
        out_shape=jax.ShapeDtypeStruct((batch * seq, SSD_INNER), F32),
        scratch_shapes=[
            pltpu.VMEM((SUBLANES, SSD_XBC), F32),
            pltpu.VMEM((SSD_STATE, SSD_INNER), F32),
        ],
        compiler_params=_params("parallel", "arbitrary"),
        name="ssd",
    )(xbc, z, dt, convw, convb, dtb, alog, dskip, gn)


def _rope(t, cos, sin_up, sin_dn):
    half = ROPE_DIM // 2
    return (t * cos + pltpu.roll(t, half, 1) * sin_up
            + pltpu.roll(t, ATTN_WIDTH - half, 1) * sin_dn)


def _moba_body(q_ref, k_ref, v_ref, cosq_ref, supq_ref, sdnq_ref,
               cosk_ref, supk_ref, sdnk_ref, o_ref,
               ks_ref, vt_ref, kmean_ref, sel_ref, s_ref):
    seq = k_ref.shape[0]
    nb = seq // MOBA_BLOCK
    Dh = ATTN_HEAD_DIM
    BL = MOBA_BLOCK
    H = ATTN_HEADS
    step = pl.program_id(1)
    nt = (((1,), (1,)), ((), ()))

    head_of_lane = lax.broadcasted_iota(jnp.int32, (1, ATTN_WIDTH), 1) // Dh

    @pl.when(step == 0)
    def _():
        for j in range(nb):
            rows = slice(j * BL, (j + 1) * BL)
            kj = _rope(k_ref[rows, :], cosk_ref[rows, :], supk_ref[rows, :],
                       sdnk_ref[rows, :])
            kmean = jnp.mean(kj, axis=0, keepdims=True)
            for h in range(H):
                kmean_ref[h * nb + j:h * nb + j + 1, :] = jnp.where(
                    head_of_lane == h, kmean, 0.0)
            kb = kj.astype(BF16)
            for h in range(H):
                ks_ref[h, rows, :] = kb[:, h * Dh:(h + 1) * Dh]
            vt_ref[:, rows] = v_ref[rows, :].T.astype(BF16)

    def over_blocks(fn, n, carry):
        start = 0
        for width in MOBA_UNROLLS:
            def body(jj, carry, start=start, width=width):
                for t in range(width):
                    carry = fn(start + width * jj + t, carry)
                return carry
            groups = lax.div(n - start, width)
            carry = lax.fori_loop(0, groups, body, carry)
            start = start + groups * width
        return carry

    def attend(i, rows):
        qf = _rope(q_ref[rows, :], cosq_ref[rows, :], supq_ref[rows, :],
                   sdnq_ref[rows, :])
        qb = (qf * (Dh ** -0.5)).astype(BF16)
        blk = lax.broadcasted_iota(jnp.int32, (nb, BL), 0)
        past = blk < i
        gates = lax.dot_general(kmean_ref[...], qf, nt, precision=HIGHEST,
                                preferred_element_type=F32)
        for h in range(H):
            gate = jnp.where(past, gates[h * nb:(h + 1) * nb, :], -jnp.inf)
            rank = jnp.zeros((nb, BL), jnp.int32)
            for j2 in range(nb):
                g2 = gate[j2:j2 + 1, :]
                ahead = (g2 > gate) | ((g2 == gate) & (j2 < blk))
                rank = rank + ahead.astype(jnp.int32)
            sel_ref[h] = (past & (rank < MOBA_TOPK)).astype(F32)

        qh = [qb[:, h * Dh:(h + 1) * Dh] for h in range(H)]

        def masked_scores(h, j, keep):
            k0 = pl.multiple_of(j * BL, BL)
            st = lax.dot_general(ks_ref[h, pl.ds(k0, BL), :], qh[h], nt,
                                 preferred_element_type=F32)
            st = jnp.where(keep, st, -jnp.inf)
            s_ref[h, j] = st
            return jnp.max(st, axis=0, keepdims=True)

        own_mask = (lax.broadcasted_iota(jnp.int32, (BL, BL), 0) <=
                    lax.broadcasted_iota(jnp.int32, (BL, BL), 1))
        m_own = tuple(masked_scores(h, i, own_mask) for h in range(H))

        def pass1(j, ms):
            return tuple(
                jnp.maximum(ms[h], masked_scores(
                    h, j, sel_ref[h, pl.ds(j, 1), :] > 0.5)) for h in range(H))

        ms = over_blocks(pass1, i, m_own)

        def pass2(j, carry):
            k0 = pl.multiple_of(j * BL, BL)
            out = []
            for h in range(H):
                p = jnp.exp(s_ref[h, j] - ms[h])
                l = carry[2 * h] + jnp.sum(p, axis=0, keepdims=True)
                acc = carry[2 * h + 1] + jnp.dot(
                    vt_ref[h * Dh:(h + 1) * Dh, pl.ds(k0, BL)], p.astype(BF16),
                    preferred_element_type=F32)
                out.extend((l, acc))
            return tuple(out)

        init = (jnp.zeros((1, BL), F32), jnp.zeros((Dh, BL), F32)) * H
        carry = over_blocks(pass2, i + 1, init)
        out_t = jnp.concatenate(
            [carry[2 * h + 1] / carry[2 * h] for h in range(H)], axis=0)
        o_ref[rows, :] = out_t.T

    for sub in range(MOBA_Q_PER_STEP):
        attend(step * MOBA_Q_PER_STEP + sub, slice(sub * BL, (sub + 1) * BL))


def _moba(q, k, v, tables, batch, seq):
    nb = seq // MOBA_BLOCK
    full = lambda b, i: (b, 0)
    steps = nb // MOBA_Q_PER_STEP
    qrow = lambda b, i: (b * steps + i, 0)
    trow = lambda b, i: (i, 0)
    const = lambda b, i: (0, 0)
    qspec = pl.BlockSpec((MOBA_Q_PER_STEP * MOBA_BLOCK, ATTN_WIDTH), qrow)
    tq = pl.BlockSpec((MOBA_Q_PER_STEP * MOBA_BLOCK, ATTN_WIDTH), trow)
    tk = pl.BlockSpec((seq, ATTN_WIDTH), const)
    kv = pl.BlockSpec((seq, ATTN_WIDTH), full)
    return pl.pallas_call(
        _moba_body,
        grid=(batch, steps),
        in_specs=[qspec, kv, kv, tq, tq, tq, tk, tk, tk],
        out_specs=qspec,
        out_shape=jax.ShapeDtypeStruct((batch * seq, ATTN_WIDTH), F32),
        scratch_shapes=[
            pltpu.VMEM((ATTN_HEADS, seq, ATTN_HEAD_DIM), BF16),
            pltpu.VMEM((ATTN_WIDTH, seq), BF16),
            pltpu.VMEM((ATTN_HEADS * nb, ATTN_WIDTH), F32),
            pltpu.VMEM((ATTN_HEADS, nb, MOBA_BLOCK), F32),
            pltpu.VMEM((ATTN_HEADS, nb, MOBA_BLOCK, MOBA_BLOCK), F32),
        ],
        compiler_params=_params("parallel", "arbitrary"),
        name="moba",
    )(q, k, v, *tables, *tables)


def _outproj_ffn_body(x_ref, ys_ref, yp_ref, ya_ref, w_ref, g_ref,
                      gpre_ref, wg_ref, wu_ref, wd_ref, gpost_ref, o_ref, h_ref):
    o1 = SSD_INNER
    o2 = SSD_INNER + POOL_WIDTH
    tiles = _sub_tiles(OUT_TOKEN_TILE)
    xs = []
    for rows in tiles:
        m = jnp.dot(ys_ref[rows, :].astype(BF16), w_ref[0:o1, :],
                    preferred_element_type=F32)
        m = m + jnp.dot(yp_ref[rows, :].astype(BF16), w_ref[o1:o2, :],
                        preferred_element_type=F32)
        m = m + jnp.dot(ya_ref[rows, :].astype(BF16), w_ref[o2:, :],
                        preferred_element_type=F32)
        xs.append(x_ref[rows, :] + _rms(m) * g_ref[...])
    for rows, x in zip(tiles, xs):
        _ffn_gate_up(x, gpre_ref, wg_ref, wu_ref, h_ref.at[rows])
        o_ref[rows, :] = _ffn_down(x, wd_ref, gpost_ref, h_ref.at[rows])


def _outproj_ffn(x, y_ssd, y_pool, y_attn, w, g, gpre, wg, wu, wd, gpost, layer):
    m = x.shape[0]
    row = lambda i: (i, 0)
    return pl.pallas_call(
        _outproj_ffn_body,
        grid=(m // OUT_TOKEN_TILE,),
        in_specs=[
            pl.BlockSpec((OUT_TOKEN_TILE, D_MODEL), row),
            pl.BlockSpec((OUT_TOKEN_TILE, SSD_INNER), row),
            pl.BlockSpec((OUT_TOKEN_TILE, POOL_WIDTH), row),
            pl.BlockSpec((OUT_TOKEN_TILE, ATTN_WIDTH), row),
            _layer_spec((D_MODEL, D_MODEL), layer),
            _layer_spec((1, D_MODEL), layer),
            _layer_spec((1, D_MODEL), layer),
            _layer_spec((D_MODEL, D_FF), layer),
            _layer_spec((D_MODEL, D_FF), layer),
            _layer_spec((D_FF, D_MODEL), layer),
            _layer_spec((1, D_MODEL), layer),
        ],
        out_specs=pl.BlockSpec((OUT_TOKEN_TILE, D_MODEL), row),
        out_shape=jax.ShapeDtypeStruct((m, D_MODEL), F32),
        scratch_shapes=[pltpu.VMEM((OUT_TOKEN_TILE, D_FF), BF16)],
        compiler_params=_params("parallel"),
        name="outproj_ffn",
    )(x, y_ssd, y_pool, y_attn, w, g, gpre, wg, wu, wd, gpost)


def _rope_tables(seq):
    half = ROPE_DIM // 2
    inv_freq = ROPE_THETA ** (-jnp.arange(0, ROPE_DIM, 2, dtype=F32) / ROPE_DIM)
    ang = jnp.arange(seq, dtype=F32)[:, None] * inv_freq[None, :]
    rest = ATTN_HEAD_DIM - ROPE_DIM
    one = jnp.ones((seq, rest), F32)
    zero = jnp.zeros((seq, rest), F32)
    zh = jnp.zeros((seq, half), F32)
    cos = jnp.concatenate([jnp.cos(ang), jnp.cos(ang), one], axis=-1)
    sin_up = jnp.concatenate([zh, jnp.sin(ang), zero], axis=-1)
    sin_dn = jnp.concatenate([-jnp.sin(ang), zh, zero], axis=-1)
    return tuple(jnp.tile(t, (1, ATTN_HEADS)) for t in (cos, sin_up, sin_dn))


def _split_w_in(w):
    pad = [(0, 0)] * (w.ndim - 1) + [(0, DT_PAD - SSD_HEADS)]
    pieces = [w[..., OFF_XBC:OFF_DT], w[..., OFF_POOL:OFF_Q], w[..., OFF_Z:OFF_XBC],
              w[..., OFF_Q:OFF_K], w[..., OFF_K:OFF_V], w[..., OFF_V:IN_COLS],
              jnp.pad(w[..., OFF_DT:OFF_POOL], pad)]
    return tuple(p.astype(BF16) for p in pieces)


def _per_head_lanes(p):
    return jnp.repeat(p, SSD_HEAD_DIM)[None, :]


def _per_head_cols(p):
    pad = [(0, 0)] * (p.ndim - 1) + [(0, DT_PAD - SSD_HEADS)]
    return jnp.pad(p, pad)[..., None, :]


def _pool_blockdiag(w):
    out = jnp.zeros(w.shape[:-3] + (POOL_WIDTH, POOL_WIDTH), F32)
    for g in range(POOL_GROUPS):
        sl = slice(g * POOL_GROUP_DIM, (g + 1) * POOL_GROUP_DIM)
        out = out.at[..., sl, sl].set(w[..., g, :, :])
    return out.astype(BF16)


def kernel(x, ff1_norm_pre, ff1_w_gate, ff1_w_up, ff1_w_down, ff1_norm_post,
           mix_norm_pre, w_in, conv_w, conv_b, dt_bias, a_log, d_skip, ssd_norm,
           pool_w, pool_scale, w_out, mix_norm_post,
           ff2_norm_pre, ff2_w_gate, ff2_w_up, ff2_w_down, ff2_norm_post):
    batch, seq, d = x.shape
    depth = w_in.shape[0]
    h = x.reshape(batch * seq, d)
    rope_tables = _rope_tables(seq)
    row = lambda p: p[None, :]
    stack_row = lambda p: p[:, None, :]
    bf = lambda w: w.astype(BF16)
    ff1 = (stack_row(ff1_norm_pre), bf(ff1_w_gate), bf(ff1_w_up), bf(ff1_w_down),
           stack_row(ff1_norm_post))
    ff2 = (stack_row(ff2_norm_pre), bf(ff2_w_gate), bf(ff2_w_up), bf(ff2_w_down),
           stack_row(ff2_norm_post))
    w_in_b = _split_w_in(w_in)
    w_out_b = bf(w_out)
    g_mix_pre = stack_row(mix_norm_pre)
    g_mix_post = stack_row(mix_norm_post)
    mix_front = (g_mix_pre, w_in_b, _pool_blockdiag(pool_w), stack_row(pool_scale))
    for l in range(depth):
        h, xbc, y_pool, z, q, k, v, dt = _ffn_inproj(h, *ff1, *mix_front, l, seq)
        y_ssd = _ssd(xbc, z, dt, conv_w[l], row(conv_b[l]),
                     _per_head_cols(dt_bias[l]), _per_head_cols(a_log[l]),
                     _per_head_lanes(d_skip[l]), row(ssd_norm[l]), batch, seq)
        y_attn = _moba(q, k, v, rope_tables, batch, seq)
        h = _outproj_ffn(h, y_ssd, y_pool, y_attn, w_out_b, g_mix_post, *ff2, l)
    return h.reshape(batch, seq, d)
```

```python
import functools
import math

import jax
import jax.numpy as jnp
from jax import lax
from jax.experimental import pallas as pl
from jax.experimental.pallas import tpu as pltpu

F32 = jnp.float32
BF16 = jnp.bfloat16
HIGHEST = lax.Precision.HIGHEST

D_MODEL = 1024
D_FF = 2816
SSD_INNER = 512
SSD_HEAD_DIM = 64
SSD_HEADS = SSD_INNER // SSD_HEAD_DIM
SSD_GROUPS = 2
SSD_STATE = 128
SSD_CONV = 4
SSD_CHUNK = 128
SSD_BC = SSD_GROUPS * SSD_STATE
SSD_XBC = SSD_INNER + 2 * SSD_BC
POOL_WIDTH = 256
POOL_GROUPS = 4
POOL_GROUP_DIM = POOL_WIDTH // POOL_GROUPS
ATTN_WIDTH = 256
ATTN_HEAD_DIM = 64
ATTN_HEADS = ATTN_WIDTH // ATTN_HEAD_DIM
ROPE_DIM = ATTN_HEAD_DIM // 4
ROPE_THETA = 500000.0
MOBA_BLOCK = 256
MOBA_TOPK = 3
RMS_EPS = 1e-6

OFF_Z = 0
OFF_XBC = OFF_Z + SSD_INNER
OFF_DT = OFF_XBC + SSD_XBC
OFF_POOL = OFF_DT + SSD_HEADS
OFF_Q = OFF_POOL + POOL_WIDTH
OFF_K = OFF_Q + ATTN_WIDTH
OFF_V = OFF_K + ATTN_WIDTH
IN_COLS = OFF_V + ATTN_WIDTH

LANES = 128
SUBLANES = 8
MXU_DIM = 256
VMEM_LIMIT_BYTES = 56 * 1024 * 1024

DT_PAD = LANES
TOKEN_TILE = 512
OUT_TOKEN_TILE = 1024
SUB_TILE = 256
FF_CHUNK = MXU_DIM
SSD_STEP_ROWS = 8 * SSD_CHUNK
MOBA_UNROLLS = (4, 2, 1)
MOBA_Q_PER_STEP = 2


def _rms(x):
    return x * lax.rsqrt(jnp.mean(x * x, axis=-1, keepdims=True) + RMS_EPS)


def _silu(x):
    return x * jax.nn.sigmoid(x)


def _softplus(x):
    return jnp.maximum(x, 0.0) + jnp.log1p(jnp.exp(-jnp.abs(x)))


def _const_spec(shape):
    zeros = (0,) * len(shape)
    return pl.BlockSpec(shape, lambda *_: zeros, pipeline_mode=pl.Buffered(1))


def _layer_spec(shape, layer):
    zeros = (0,) * len(shape)
    return pl.BlockSpec((None,) + shape, lambda *_: (layer,) + zeros,
                        pipeline_mode=pl.Buffered(1))


def _params(*sem):
    return pltpu.CompilerParams(dimension_semantics=sem,
                                vmem_limit_bytes=VMEM_LIMIT_BYTES)


def _ffn_gate_up(x, gpre_ref, wg_ref, wu_ref, h_ref):
    xb = (_rms(x) * gpre_ref[...]).astype(BF16)
    for c in range(0, D_FF, FF_CHUNK):
        sl = slice(c, min(c + FF_CHUNK, D_FF))
        g = jnp.dot(xb, wg_ref[:, sl], preferred_element_type=F32)
        u = jnp.dot(xb, wu_ref[:, sl], preferred_element_type=F32)
        h_ref[:, sl] = (_silu(g) * u).astype(BF16)


def _ffn_down(x, wd_ref, gpost_ref, h_ref):
    f = jnp.dot(h_ref[...], wd_ref[...], preferred_element_type=F32)
    return x + 0.5 * (_rms(f) * gpost_ref[...])


def _sub_tiles(tile_rows):
    return [slice(r0, r0 + SUB_TILE) for r0 in range(0, tile_rows, SUB_TILE)]


_PROJ_WIDTHS = (SSD_XBC, POOL_WIDTH, SSD_INNER, ATTN_WIDTH, ATTN_WIDTH,
                ATTN_WIDTH, DT_PAD)
POOL_HALO = 2 ** POOL_GROUPS


def _pool_tile(u, tail, pos, w_ref, scale_ref):
    group = lax.broadcasted_iota(jnp.int32, u.shape, 1) // POOL_GROUP_DIM
    win_sum = jnp.concatenate([tail, u], axis=0)
    mean = jnp.zeros_like(u)
    for g in range(POOL_GROUPS):
        half = 2 ** g
        win_sum = win_sum + pltpu.roll(win_sum, half, 0)
        count = jnp.minimum(pos + 1, 2 * half).astype(F32)
        mean = jnp.where(group == g, win_sum[POOL_HALO:, :] / count, mean)
    d = (mean - u).astype(BF16)
    return jnp.dot(d, w_ref[...], preferred_element_type=F32) * scale_ref[...]


def _ffn_inproj_body(tiles_per_seq, x_ref, gpre_ref, wg_ref, wu_ref, wd_ref,
                     gpost_ref, g_ref, *rest):
    n_proj = len(_PROJ_WIDTHS)
    w_refs = rest[:n_proj]
    (poolw_ref, pscale_ref, o_ref, xbc_ref, yp_ref, z_ref, q_ref, k_ref, v_ref,
     dt_ref, h_ref, utail_ref) = rest[n_proj:]
    tile_in_seq = pl.program_id(0) % tiles_per_seq

    @pl.when(tile_in_seq == 0)
    def _():
        utail_ref[...] = jnp.zeros_like(utail_ref)

    tiles = _sub_tiles(TOKEN_TILE)

    def down(rows):
        x = _ffn_down(x_ref[rows, :], wd_ref, gpost_ref, h_ref.at[rows])
        o_ref[rows, :] = x
        return (_rms(x) * g_ref[...]).astype(BF16)

    def project(rows, xb, utail):
        def proj(i):
            return jnp.dot(xb, w_refs[i][...], preferred_element_type=F32)

        u = proj(1)
        xbc_ref[rows, :] = proj(0)
        for i, ref in enumerate((z_ref, q_ref, k_ref, v_ref, dt_ref), start=2):
            ref[rows, :] = proj(i)
        pos = (tile_in_seq * TOKEN_TILE + rows.start
               + lax.broadcasted_iota(jnp.int32, u.shape, 0))
        yp_ref[rows, :] = _pool_tile(u, utail, pos, poolw_ref, pscale_ref)
        return u[SUB_TILE - POOL_HALO:, :]

    utail = utail_ref[...]
    _ffn_gate_up(x_ref[tiles[0], :], gpre_ref, wg_ref, wu_ref, h_ref.at[tiles[0]])
    for i, rows in enumerate(tiles):
        xb = down(rows)
        if i + 1 < len(tiles):
            nxt = tiles[i + 1]
            _ffn_gate_up(x_ref[nxt, :], gpre_ref, wg_ref, wu_ref, h_ref.at[nxt])
        utail = project(rows, xb, utail)
    utail_ref[...] = utail


def _ffn_inproj(x, gpre, wg, wu, wd, gpost, g, ws, poolw, pscale, layer, seq):
    m = x.shape[0]
    row = lambda i: (i, 0)
    widths = (D_MODEL,) + _PROJ_WIDTHS
    return pl.pallas_call(
        functools.partial(_ffn_inproj_body, seq // TOKEN_TILE),
        grid=(m // TOKEN_TILE,),
        in_specs=[
            pl.BlockSpec((TOKEN_TILE, D_MODEL), row),
            _layer_spec((1, D_MODEL), layer),
            _const_spec((D_MODEL, D_FF)),
            _const_spec((D_MODEL, D_FF)),
            _const_spec((D_FF, D_MODEL)),
            _layer_spec((1, D_MODEL), layer),
            _layer_spec((1, D_MODEL), layer),
            *[_layer_spec((D_MODEL, width), layer) for width in _PROJ_WIDTHS],
            _layer_spec((POOL_WIDTH, POOL_WIDTH), layer),
            _layer_spec((1, POOL_WIDTH), layer),
        ],
        out_specs=[pl.BlockSpec((TOKEN_TILE, width), row) for width in widths],
        out_shape=[jax.ShapeDtypeStruct((m, width), F32) for width in widths],
        scratch_shapes=[
            pltpu.VMEM((TOKEN_TILE, D_FF), BF16),
            pltpu.VMEM((POOL_HALO, POOL_WIDTH), F32),
        ],
        compiler_params=_params("arbitrary"),
        name="ffn_inproj",
    )(x, gpre, wg, wu, wd, gpost, g, *ws, poolw, pscale)


CAST_CHUNKS = 16
_FFN_W_SHAPES = ((D_MODEL, D_FF), (D_MODEL, D_FF), (D_FF, D_MODEL))


def _with_weight_cast(body, n_in, n_out, n_w):
    def wrapped(*refs):
        ins, refs = refs[:n_in], refs[n_in:]
        w_ins, refs = refs[:n_w], refs[n_w:]
        outs, refs = refs[:n_out], refs[n_out:]
        w_outs, scratch = refs[:n_w], refs[n_w:]
        for w_in, w_out in zip(w_ins, w_outs):
            w_out[...] = w_in[...].astype(BF16)
        body(*ins, *outs, *scratch)

    return wrapped


def _weight_cast_specs(layer, n_steps, step_of_grid):
    if layer is None:
        return [], [], []
    n_chunks = math.gcd(n_steps, CAST_CHUNKS)
    steps_per_chunk = n_steps // n_chunks
    chunk_of_step = lambda *g: step_of_grid(*g) // steps_per_chunk
    in_specs, out_specs, out_shapes = [], [], []
    for rows, cols in _FFN_W_SHAPES:
        chunk = rows // n_chunks
        in_specs.append(pl.BlockSpec(
            (None, chunk, cols), lambda *g: (layer, chunk_of_step(*g), 0)))
        out_specs.append(pl.BlockSpec(
            (chunk, cols), lambda *g: (chunk_of_step(*g), 0)))
        out_shapes.append(jax.ShapeDtypeStruct((rows, cols), BF16))
    return in_specs, out_specs, out_shapes


def _causal_conv4(x, tail, w_ref, b_ref):
    x_ext = jnp.concatenate([tail, x], axis=0)
    prev_ext = pltpu.roll(x_ext, 1, 0)
    near = w_ref[3:4, :] * x + w_ref[2:3, :] * prev_ext[SUBLANES:, :]
    far_ext = w_ref[1:2, :] * x_ext + w_ref[0:1, :] * prev_ext
    return near + pltpu.roll(far_ext, 2, 0)[SUBLANES:, :] + b_ref[...]


def _heads_to_lanes(xc):
    rows = xc.shape[0]
    lane = lax.broadcasted_iota(jnp.int32, (rows, LANES), 1)
    per_tile = LANES // SSD_HEAD_DIM
    parts = []
    for t in range(SSD_INNER // LANES):
        tile = jnp.broadcast_to(xc[:, t * per_tile:t * per_tile + 1], (rows, LANES))
        for i in range(1, per_tile):
            h = t * per_tile + i
            tile = jnp.where(lane < i * SSD_HEAD_DIM, tile,
                             jnp.broadcast_to(xc[:, h:h + 1], (rows, LANES)))
        parts.append(tile)
    return jnp.concatenate(parts, axis=1)


def _ssd_body(xbc_ref, z_ref, dt_ref, convw_ref, convb_ref, dtb_ref, alog_ref,
              dskip_ref, gn_ref, o_ref, tail_ref, state_ref):
    L = SSD_CHUNK
    HP = SSD_INNER
    P = SSD_HEAD_DIM
    N = SSD_STATE
    GW = HP // SSD_GROUPS
    HPG = SSD_HEADS // SSD_GROUPS

    @pl.when(pl.program_id(1) == 0)
    def _():
        tail_ref[...] = jnp.zeros_like(tail_ref)
        state_ref[...] = jnp.zeros_like(state_ref)

    r = lax.broadcasted_iota(jnp.int32, (L, L), 0)
    s = lax.broadcasted_iota(jnp.int32, (L, L), 1)
    causal = s <= r
    tril = causal.astype(F32)
    pair_lane = lax.broadcasted_iota(jnp.int32, (L, 2 * P), 1)
    neg_a = -jnp.exp(alog_ref[...])

    def chunk(c, tail):
        r0 = pl.multiple_of(c * L, L)
        rows = pl.ds(r0, L)
        x = xbc_ref[rows, :]
        xc = _silu(_causal_conv4(x, tail, convw_ref, convb_ref))
        xs = xc[:, :HP]
        bm = xc[:, HP:HP + SSD_BC].astype(BF16)
        cm = xc[:, HP + SSD_BC:].astype(BF16)

        dt_c = _softplus(dt_ref[rows, :] + dtb_ref[...])
        acs_c = jnp.dot(tril, dt_c * neg_a, precision=HIGHEST,
                        preferred_element_type=F32)
        acs_t = acs_c.T
        dt = _heads_to_lanes(dt_c)
        acs = _heads_to_lanes(acs_c)
        a_last = acs[L - 1:L, :]
        decay_out = jnp.exp(acs)
        decay_in = jnp.exp(a_last - acs)
        chunk_decay = jnp.exp(a_last)

        xdt = xs * dt
        xdt_b = xdt.astype(BF16)
        xw_b = (xdt * decay_in).astype(BF16)

        y_parts = []
        for g in range(SSD_GROUPS):
            bg = bm[:, g * N:(g + 1) * N]
            cg = cm[:, g * N:(g + 1) * N]
            gsl = slice(g * GW, (g + 1) * GW)
            cb = lax.dot_general(cg, bg, (((1,), (1,)), ((), ())),
                                 preferred_element_type=F32)
            st = state_ref[:, gsl]
            y_off = jnp.dot(cg, st.astype(BF16), preferred_element_type=F32)
            new_st = lax.dot_general(bg, xw_b[:, gsl], (((0,), (0,)), ((), ())),
                                     preferred_element_type=F32)
            state_ref[:, gsl] = st * chunk_decay[:, gsl] + new_st
            for pair in range(HPG // 2):
                lo = g * GW + pair * 2 * P
                x_pair = xdt_b[:, lo:lo + 2 * P]
                ys = []
                for i in range(2):
                    h = lo // P + i
                    seg = jnp.exp(jnp.where(
                        causal, acs_c[:, h:h + 1] - acs_t[h:h + 1, :], -jnp.inf))
                    ys.append(jnp.dot((cb * seg).astype(BF16), x_pair,
                                      preferred_element_type=F32))
                y_diag = jnp.where(pair_lane < P, ys[0], ys[1])
                psl = slice(pair * 2 * P, (pair + 1) * 2 * P)
                y_parts.append(
                    y_diag + y_off[:, psl] * decay_out[:, lo:lo + 2 * P])
        y = jnp.concatenate(y_parts, axis=-1) + dskip_ref[...] * xs
        o_ref[rows, :] = _rms(y * _silu(z_ref[rows, :])) * gn_ref[...]
        return x[L - SUBLANES:, :]

    n_chunks = xbc_ref.shape[0] // L
    tail_ref[...] = lax.fori_loop(0, n_chunks, chunk, tail_ref[...], unroll=True)


def _ssd(xbc, z, dt, convw, convb, dtb, alog, dskip, gn, cast_ws, cast_layer,
         batch, seq):
    nblk = seq // SSD_STEP_ROWS
    row = lambda b, c: (b * nblk + c, 0)
    w_in_specs, w_out_specs, w_out_shapes = _weight_cast_specs(
        cast_layer, batch * nblk, lambda b, c: b * nblk + c)
    return pl.pallas_call(
        _with_weight_cast(_ssd_body, 9, 1, len(w_in_specs)),
        grid=(batch, nblk),
        in_specs=[
            pl.BlockSpec((SSD_STEP_ROWS, SSD_XBC), row),
            pl.BlockSpec((SSD_STEP_ROWS, SSD_INNER), row),
            pl.BlockSpec((SSD_STEP_ROWS, DT_PAD), row),
            _const_spec((SSD_CONV, SSD_XBC)),
            _const_spec((1, SSD_XBC)),
            _const_spec((1, DT_PAD)),
            _const_spec((1, DT_PAD)),
            _const_spec((1, SSD_INNER)),
            _const_spec((1, SSD_INNER)),
            *w_in_specs,
        ],
        out_specs=[pl.BlockSpec((SSD_STEP_ROWS, SSD_INNER), row), *w_out_specs],
        out_shape=[jax.ShapeDtypeStruct((batch * seq, SSD_INNER), F32),
                   *w_out_shapes],
        scratch_shapes=[
            pltpu.VMEM((SUBLANES, SSD_XBC), F32),
            pltpu.VMEM((SSD_STATE, SSD_INNER), F32),
        ],
        compiler_params=_params("arbitrary", "arbitrary"),
        name="ssd",
    )(xbc, z, dt, convw, convb, dtb, alog, dskip, gn, *cast_ws)


def _rope(t, cos, sin_up, sin_dn):
    half = ROPE_DIM // 2
    return (t * cos + pltpu.roll(t, half, 1) * sin_up
            + pltpu.roll(t, ATTN_WIDTH - half, 1) * sin_dn)


def _moba_body(q_ref, k_ref, v_ref, cosq_ref, supq_ref, sdnq_ref,
               cosk_ref, supk_ref, sdnk_ref, o_ref,
               ks_ref, vt_ref, kmean_ref, sel_ref, s_ref):
    seq = k_ref.shape[0]
    nb = seq // MOBA_BLOCK
    Dh = ATTN_HEAD_DIM
    BL = MOBA_BLOCK
    H = ATTN_HEADS
    step = pl.program_id(1)
    nt = (((1,), (1,)), ((), ()))

    head_of_lane = lax.broadcasted_iota(jnp.int32, (1, ATTN_WIDTH), 1) // Dh

    @pl.when(step == 0)
    def _():
        for j in range(nb):
            rows = slice(j * BL, (j + 1) * BL)
            kj = _rope(k_ref[rows, :], cosk_ref[rows, :], supk_ref[rows, :],
                       sdnk_ref[rows, :])
            kmean = jnp.mean(kj, axis=0, keepdims=True)
            for h in range(H):
                kmean_ref[h * nb + j:h * nb + j + 1, :] = jnp.where(
                    head_of_lane == h, kmean, 0.0)
            kb = kj.astype(BF16)
            for h in range(H):
                ks_ref[h, rows, :] = kb[:, h * Dh:(h + 1) * Dh]
            vt_ref[:, rows] = v_ref[rows, :].T.astype(BF16)

    def over_blocks(fn, n, carry):
        start = 0
        for width in MOBA_UNROLLS:
            def body(jj, carry, start=start, width=width):
                for t in range(width):
                    carry = fn(start + width * jj + t, carry)
                return carry
            groups = lax.div(n - start, width)
            carry = lax.fori_loop(0, groups, body, carry)
            start = start + groups * width
        return carry

    def attend(i, rows):
        qf = _rope(q_ref[rows, :], cosq_ref[rows, :], supq_ref[rows, :],
                   sdnq_ref[rows, :])
        qb = (qf * (Dh ** -0.5)).astype(BF16)
        blk = lax.broadcasted_iota(jnp.int32, (nb, BL), 0)
        past = blk < i
        gates = lax.dot_general(kmean_ref[...], qf, nt, precision=HIGHEST,
                                preferred_element_type=F32)
        for h in range(H):
            gate = jnp.where(past, gates[h * nb:(h + 1) * nb, :], -jnp.inf)
            rank = jnp.zeros((nb, BL), jnp.int32)
            for j2 in range(nb):
                g2 = gate[j2:j2 + 1, :]
                ahead = (g2 > gate) | ((g2 == gate) & (j2 < blk))
                rank = rank + ahead.astype(jnp.int32)
            sel_ref[h] = (past & (rank < MOBA_TOPK)).astype(F32)

        qh = [qb[:, h * Dh:(h + 1) * Dh] for h in range(H)]

        def masked_scores(h, j, keep):
            k0 = pl.multiple_of(j * BL, BL)
            st = lax.dot_general(ks_ref[h, pl.ds(k0, BL), :], qh[h], nt,
                                 preferred_element_type=F32)
            st = jnp.where(keep, st, -jnp.inf)
            s_ref[h, j] = st
            return jnp.max(st, axis=0, keepdims=True)

        own_mask = (lax.broadcasted_iota(jnp.int32, (BL, BL), 0) <=
                    lax.broadcasted_iota(jnp.int32, (BL, BL), 1))
        m_own = tuple(masked_scores(h, i, own_mask) for h in range(H))

        def pass1(j, ms):
            return tuple(
                jnp.maximum(ms[h], masked_scores(
                    h, j, sel_ref[h, pl.ds(j, 1), :] > 0.5)) for h in range(H))

        ms = over_blocks(pass1, i, m_own)

        def pass2(j, carry):
            k0 = pl.multiple_of(j * BL, BL)
            out = []
            for h in range(H):
                p = jnp.exp(s_ref[h, j] - ms[h])
                l = carry[2 * h] + jnp.sum(p, axis=0, keepdims=True)
                acc = carry[2 * h + 1] + jnp.dot(
                    vt_ref[h * Dh:(h + 1) * Dh, pl.ds(k0, BL)], p.astype(BF16),
                    preferred_element_type=F32)
                out.extend((l, acc))
            return tuple(out)

        init = (jnp.zeros((1, BL), F32), jnp.zeros((Dh, BL), F32)) * H
        carry = over_blocks(pass2, i + 1, init)
        out_t = jnp.concatenate(
            [carry[2 * h + 1] / carry[2 * h] for h in range(H)], axis=0)
        o_ref[rows, :] = out_t.T

    for sub in range(MOBA_Q_PER_STEP):
        attend(step * MOBA_Q_PER_STEP + sub, slice(sub * BL, (sub + 1) * BL))


def _moba(q, k, v, tables, cast_ws, cast_layer, batch, seq):
    nb = seq // MOBA_BLOCK
    full = lambda b, i: (b, 0)
    steps = nb // MOBA_Q_PER_STEP
    qrow = lambda b, i: (b * steps + i, 0)
    trow = lambda b, i: (i, 0)
    const = lambda b, i: (0, 0)
    qspec = pl.BlockSpec((MOBA_Q_PER_STEP * MOBA_BLOCK, ATTN_WIDTH), qrow)
    tq = pl.BlockSpec((MOBA_Q_PER_STEP * MOBA_BLOCK, ATTN_WIDTH), trow)
    tk = pl.BlockSpec((seq, ATTN_WIDTH), const)
    kv = pl.BlockSpec((seq, ATTN_WIDTH), full)
    w_in_specs, w_out_specs, w_out_shapes = _weight_cast_specs(
        cast_layer, batch * steps, lambda b, i: b * steps + i)
    return pl.pallas_call(
        _with_weight_cast(_moba_body, 9, 1, len(w_in_specs)),
        grid=(batch, steps),
        in_specs=[qspec, kv, kv, tq, tq, tq, tk, tk, tk, *w_in_specs],
        out_specs=[qspec, *w_out_specs],
        out_shape=[jax.ShapeDtypeStruct((batch * seq, ATTN_WIDTH), F32),
                   *w_out_shapes],
        scratch_shapes=[
            pltpu.VMEM((ATTN_HEADS, seq, ATTN_HEAD_DIM), BF16),
            pltpu.VMEM((ATTN_WIDTH, seq), BF16),
            pltpu.VMEM((ATTN_HEADS * nb, ATTN_WIDTH), F32),
            pltpu.VMEM((ATTN_HEADS, nb, MOBA_BLOCK), F32),
            pltpu.VMEM((ATTN_HEADS, nb, MOBA_BLOCK, MOBA_BLOCK), F32),
        ],
        compiler_params=_params("arbitrary", "arbitrary"),
        name="moba",
    )(q, k, v, *tables, *tables, *(cast_ws if w_in_specs else ()))


def _outproj_ffn_body(x_ref, ys_ref, yp_ref, ya_ref, w_ref, g_ref,
                      gpre_ref, wg_ref, wu_ref, wd_ref, gpost_ref, o_ref, h_ref):
    o1 = SSD_INNER
    o2 = SSD_INNER + POOL_WIDTH
    tiles = _sub_tiles(OUT_TOKEN_TILE)
    xs = []
    for rows in tiles:
        m = jnp.dot(ys_ref[rows, :].astype(BF16), w_ref[0:o1, :],
                    preferred_element_type=F32)
        m = m + jnp.dot(yp_ref[rows, :].astype(BF16), w_ref[o1:o2, :],
                        preferred_element_type=F32)
        m = m + jnp.dot(ya_ref[rows, :].astype(BF16), w_ref[o2:, :],
                        preferred_element_type=F32)
        xs.append(x_ref[rows, :] + _rms(m) * g_ref[...])
    for rows, x in zip(tiles, xs):
        _ffn_gate_up(x, gpre_ref, wg_ref, wu_ref, h_ref.at[rows])
        o_ref[rows, :] = _ffn_down(x, wd_ref, gpost_ref, h_ref.at[rows])


def _outproj_ffn(x, y_ssd, y_pool, y_attn, w, g, gpre, wg, wu, wd, gpost, layer):
    m = x.shape[0]
    row = lambda i: (i, 0)
    return pl.pallas_call(
        _outproj_ffn_body,
        grid=(m // OUT_TOKEN_TILE,),
        in_specs=[
            pl.BlockSpec((OUT_TOKEN_TILE, D_MODEL), row),
            pl.BlockSpec((OUT_TOKEN_TILE, SSD_INNER), row),
            pl.BlockSpec((OUT_TOKEN_TILE, POOL_WIDTH), row),
            pl.BlockSpec((OUT_TOKEN_TILE, ATTN_WIDTH), row),
            _layer_spec((D_MODEL, D_MODEL), layer),
            _layer_spec((1, D_MODEL), layer),
            _layer_spec((1, D_MODEL), layer),
            _const_spec((D_MODEL, D_FF)),
            _const_spec((D_MODEL, D_FF)),
            _const_spec((D_FF, D_MODEL)),
            _layer_spec((1, D_MODEL), layer),
        ],
        out_specs=pl.BlockSpec((OUT_TOKEN_TILE, D_MODEL), row),
        out_shape=jax.ShapeDtypeStruct((m, D_MODEL), F32),
        scratch_shapes=[pltpu.VMEM((OUT_TOKEN_TILE, D_FF), BF16)],
        compiler_params=_params("parallel"),
        name="outproj_ffn",
    )(x, y_ssd, y_pool, y_attn, w, g, gpre, wg, wu, wd, gpost)


def _rope_tables(seq):
    half = ROPE_DIM // 2
    inv_freq = ROPE_THETA ** (-jnp.arange(0, ROPE_DIM, 2, dtype=F32) / ROPE_DIM)
    ang = jnp.arange(seq, dtype=F32)[:, None] * inv_freq[None, :]
    rest = ATTN_HEAD_DIM - ROPE_DIM
    one = jnp.ones((seq, rest), F32)
    zero = jnp.zeros((seq, rest), F32)
    zh = jnp.zeros((seq, half), F32)
    cos = jnp.concatenate([jnp.cos(ang), jnp.cos(ang), one], axis=-1)
    sin_up = jnp.concatenate([zh, jnp.sin(ang), zero], axis=-1)
    sin_dn = jnp.concatenate([-jnp.sin(ang), zh, zero], axis=-1)
    return tuple(jnp.tile(t, (1, ATTN_HEADS)) for t in (cos, sin_up, sin_dn))


def _split_w_in(w):
    pad = [(0, 0)] * (w.ndim - 1) + [(0, DT_PAD - SSD_HEADS)]
    pieces = [w[..., OFF_XBC:OFF_DT], w[..., OFF_POOL:OFF_Q], w[..., OFF_Z:OFF_XBC],
              w[..., OFF_Q:OFF_K], w[..., OFF_K:OFF_V], w[..., OFF_V:IN_COLS],
              jnp.pad(w[..., OFF_DT:OFF_POOL], pad)]
    return tuple(p.astype(BF16) for p in pieces)


def _per_head_lanes(p):
    return jnp.repeat(p, SSD_HEAD_DIM)[None, :]


def _per_head_cols(p):
    pad = [(0, 0)] * (p.ndim - 1) + [(0, DT_PAD - SSD_HEADS)]
    return jnp.pad(p, pad)[..., None, :]


def _pool_blockdiag(w):
    out = jnp.zeros(w.shape[:-3] + (POOL_WIDTH, POOL_WIDTH), F32)
    for g in range(POOL_GROUPS):
        sl = slice(g * POOL_GROUP_DIM, (g + 1) * POOL_GROUP_DIM)
        out = out.at[..., sl, sl].set(w[..., g, :, :])
    return out.astype(BF16)


def kernel(x, ff1_norm_pre, ff1_w_gate, ff1_w_up, ff1_w_down, ff1_norm_post,
           mix_norm_pre, w_in, conv_w, conv_b, dt_bias, a_log, d_skip, ssd_norm,
           pool_w, pool_scale, w_out, mix_norm_post,
           ff2_norm_pre, ff2_w_gate, ff2_w_up, ff2_w_down, ff2_norm_post):
    batch, seq, d = x.shape
    depth = w_in.shape[0]
    h = x.reshape(batch * seq, d)
    rope_tables = _rope_tables(seq)
    row = lambda p: p[None, :]
    stack_row = lambda p: p[:, None, :]
    bf = lambda w: w.astype(BF16)
    ff1_w = (ff1_w_gate, ff1_w_up, ff1_w_down)
    ff2_w = (ff2_w_gate, ff2_w_up, ff2_w_down)
    g_ff1 = (stack_row(ff1_norm_pre), stack_row(ff1_norm_post))
    g_ff2 = (stack_row(ff2_norm_pre), stack_row(ff2_norm_post))
    w_in_b = _split_w_in(w_in)
    w_out_b = bf(w_out)
    g_mix_pre = stack_row(mix_norm_pre)
    g_mix_post = stack_row(mix_norm_post)
    mix_front = (g_mix_pre, w_in_b, _pool_blockdiag(pool_w), stack_row(pool_scale))
    ff1_b = tuple(bf(w[0]) for w in ff1_w)
    for l in range(depth):
        h, xbc, y_pool, z, q, k, v, dt = _ffn_inproj(
            h, g_ff1[0], *ff1_b, g_ff1[1], *mix_front, l, seq)
        y_ssd, *ff2_b = _ssd(xbc, z, dt, conv_w[l], row(conv_b[l]),
                             _per_head_cols(dt_bias[l]), _per_head_cols(a_log[l]),
                             _per_head_lanes(d_skip[l]), row(ssd_norm[l]),
                             ff2_w, l, batch, seq)
        nxt = l + 1 if l + 1 < depth else None
        y_attn, *ff1_b = _moba(q, k, v, rope_tables, ff1_w, nxt, batch, seq)
        h = _outproj_ffn(h, y_ssd, y_pool, y_attn, w_out_b, g_mix_post,
                         g_ff2[0], *ff2_b, g_ff2[1], l)
    return h.reshape(batch, seq, d)
```

```python
import functools
import math

import jax
import jax.numpy as jnp
from jax import lax
from jax.experimental import pallas as pl
from jax.experimental.pallas import tpu as pltpu

F32 = jnp.float32
BF16 = jnp.bfloat16
HIGHEST = lax.Precision.HIGHEST

D_MODEL = 1024
D_FF = 2816
SSD_INNER = 512
SSD_HEAD_DIM = 64
SSD_HEADS = SSD_INNER // SSD_HEAD_DIM
SSD_GROUPS = 2
SSD_STATE = 128
SSD_CONV = 4
SSD_CHUNK = 128
SSD_BC = SSD_GROUPS * SSD_STATE
SSD_XBC = SSD_INNER + 2 * SSD_BC
POOL_WIDTH = 256
POOL_GROUPS = 4
POOL_GROUP_DIM = POOL_WIDTH // POOL_GROUPS
ATTN_WIDTH = 256
ATTN_HEAD_DIM = 64
ATTN_HEADS = ATTN_WIDTH // ATTN_HEAD_DIM
ROPE_DIM = ATTN_HEAD_DIM // 4
ROPE_THETA = 500000.0
MOBA_BLOCK = 256
MOBA_TOPK = 3
RMS_EPS = 1e-6

OFF_Z = 0
OFF_XBC = OFF_Z + SSD_INNER
OFF_DT = OFF_XBC + SSD_XBC
OFF_POOL = OFF_DT + SSD_HEADS
OFF_Q = OFF_POOL + POOL_WIDTH
OFF_K = OFF_Q + ATTN_WIDTH
OFF_V = OFF_K + ATTN_WIDTH
IN_COLS = OFF_V + ATTN_WIDTH

LANES = 128
SUBLANES = 8
MXU_DIM = 256
VMEM_LIMIT_BYTES = 56 * 1024 * 1024

DT_PAD = LANES
TOKEN_TILE = 512
OUT_TOKEN_TILE = 1024
SUB_TILE = 256
FF_CHUNK = MXU_DIM
SSD_STEP_ROWS = 8 * SSD_CHUNK
MOBA_UNROLLS = (4, 2, 1)
MOBA_Q_PER_STEP = 2
BF16_SUBLANES = 16
LOG2E = math.log2(math.e)


def _rms(x):
    return x * lax.rsqrt(jnp.mean(x * x, axis=-1, keepdims=True) + RMS_EPS)


def _silu(x):
    return x * jax.nn.sigmoid(x)


def _softplus(x):
    return jnp.maximum(x, 0.0) + jnp.log1p(jnp.exp(-jnp.abs(x)))


def _const_spec(shape):
    zeros = (0,) * len(shape)
    return pl.BlockSpec(shape, lambda *_: zeros, pipeline_mode=pl.Buffered(1))


def _layer_spec(shape, layer):
    zeros = (0,) * len(shape)
    return pl.BlockSpec((None,) + shape, lambda *_: (layer,) + zeros,
                        pipeline_mode=pl.Buffered(1))


def _params(*sem):
    return pltpu.CompilerParams(dimension_semantics=sem,
                                vmem_limit_bytes=VMEM_LIMIT_BYTES)


def _ffn_gate_up(x, gpre_ref, wg_ref, wu_ref, h_ref):
    xb = (_rms(x) * gpre_ref[...]).astype(BF16)
    for c in range(0, D_FF, FF_CHUNK):
        sl = slice(c, min(c + FF_CHUNK, D_FF))
        g = jnp.dot(xb, wg_ref[:, sl], preferred_element_type=F32)
        u = jnp.dot(xb, wu_ref[:, sl], preferred_element_type=F32)
        h_ref[:, sl] = (_silu(g) * u).astype(BF16)


def _ffn_down(x, wd_ref, gpost_ref, h_ref):
    f = jnp.dot(h_ref[...], wd_ref[...], preferred_element_type=F32)
    return x + 0.5 * (_rms(f) * gpost_ref[...])


def _sub_tiles(tile_rows):
    return [slice(r0, r0 + SUB_TILE) for r0 in range(0, tile_rows, SUB_TILE)]


_PROJ_WIDTHS = (SSD_XBC, POOL_WIDTH, SSD_INNER, ATTN_WIDTH, ATTN_WIDTH,
                ATTN_WIDTH, DT_PAD)
POOL_HALO = 2 ** POOL_GROUPS


def _pool_tile(u, tail, pos, w_ref, scale_ref):
    group = lax.broadcasted_iota(jnp.int32, u.shape, 1) // POOL_GROUP_DIM
    win_sum = jnp.concatenate([tail, u], axis=0)
    mean = jnp.zeros_like(u)
    for g in range(POOL_GROUPS):
        half = 2 ** g
        win_sum = win_sum + pltpu.roll(win_sum, half, 0)
        count = jnp.minimum(pos + 1, 2 * half).astype(F32)
        mean = jnp.where(group == g, win_sum[POOL_HALO:, :] / count, mean)
    d = (mean - u).astype(BF16)
    return jnp.dot(d, w_ref[...], preferred_element_type=F32) * scale_ref[...]


def _ffn_inproj_body(tiles_per_seq, x_ref, gpre_ref, wg_ref, wu_ref, wd_ref,
                     gpost_ref, g_ref, *rest):
    n_proj = len(_PROJ_WIDTHS)
    w_refs = rest[:n_proj]
    (poolw_ref, pscale_ref, o_ref, xbc_ref, yp_ref, z_ref, q_ref, k_ref, v_ref,
     dt_ref, h_ref, utail_ref) = rest[n_proj:]
    tile_in_seq = pl.program_id(0) % tiles_per_seq

    @pl.when(tile_in_seq == 0)
    def _():
        utail_ref[...] = jnp.zeros_like(utail_ref)

    tiles = _sub_tiles(TOKEN_TILE)

    def down(rows):
        x = _ffn_down(x_ref[rows, :], wd_ref, gpost_ref, h_ref.at[rows])
        o_ref[rows, :] = x
        return (_rms(x) * g_ref[...]).astype(BF16)

    def project(rows, xb, utail):
        def proj(i):
            return jnp.dot(xb, w_refs[i][...], preferred_element_type=F32)

        u = proj(1)
        xbc_ref[rows, :] = proj(0)
        for i, ref in enumerate((z_ref, q_ref, k_ref, v_ref, dt_ref), start=2):
            ref[rows, :] = proj(i)
        pos = (tile_in_seq * TOKEN_TILE + rows.start
               + lax.broadcasted_iota(jnp.int32, u.shape, 0))
        yp_ref[rows, :] = _pool_tile(u, utail, pos, poolw_ref, pscale_ref)
        return u[SUB_TILE - POOL_HALO:, :]

    utail = utail_ref[...]
    _ffn_gate_up(x_ref[tiles[0], :], gpre_ref, wg_ref, wu_ref, h_ref.at[tiles[0]])
    for i, rows in enumerate(tiles):
        xb = down(rows)
        if i + 1 < len(tiles):
            nxt = tiles[i + 1]
            _ffn_gate_up(x_ref[nxt, :], gpre_ref, wg_ref, wu_ref, h_ref.at[nxt])
        utail = project(rows, xb, utail)
    utail_ref[...] = utail


def _ffn_inproj(x, gpre, wg, wu, wd, gpost, g, ws, poolw, pscale, layer, seq):
    m = x.shape[0]
    row = lambda i: (i, 0)
    widths = (D_MODEL,) + _PROJ_WIDTHS
    return pl.pallas_call(
        functools.partial(_ffn_inproj_body, seq // TOKEN_TILE),
        grid=(m // TOKEN_TILE,),
        in_specs=[
            pl.BlockSpec((TOKEN_TILE, D_MODEL), row),
            _layer_spec((1, D_MODEL), layer),
            _const_spec((D_MODEL, D_FF)),
            _const_spec((D_MODEL, D_FF)),
            _const_spec((D_FF, D_MODEL)),
            _layer_spec((1, D_MODEL), layer),
            _layer_spec((1, D_MODEL), layer),
            *[_layer_spec((D_MODEL, width), layer) for width in _PROJ_WIDTHS],
            _layer_spec((POOL_WIDTH, POOL_WIDTH), layer),
            _layer_spec((1, POOL_WIDTH), layer),
        ],
        out_specs=[pl.BlockSpec((TOKEN_TILE, width), row) for width in widths],
        out_shape=[jax.ShapeDtypeStruct((m, width), F32) for width in widths],
        scratch_shapes=[
            pltpu.VMEM((TOKEN_TILE, D_FF), BF16),
            pltpu.VMEM((POOL_HALO, POOL_WIDTH), F32),
        ],
        compiler_params=_params("arbitrary"),
        name="ffn_inproj",
    )(x, gpre, wg, wu, wd, gpost, g, *ws, poolw, pscale)


CAST_CHUNKS = 16
_FFN_W_SHAPES = ((D_MODEL, D_FF), (D_MODEL, D_FF), (D_FF, D_MODEL))


def _with_weight_cast(body, n_in, n_out, n_w):
    def wrapped(*refs):
        ins, refs = refs[:n_in], refs[n_in:]
        w_ins, refs = refs[:n_w], refs[n_w:]
        outs, refs = refs[:n_out], refs[n_out:]
        w_outs, scratch = refs[:n_w], refs[n_w:]
        body(*ins, *outs, *scratch)
        for w_in, w_out in zip(w_ins, w_outs):
            w_out[...] = w_in[...].astype(BF16)

    return wrapped


def _weight_cast_specs(layer, n_steps, step_of_grid):
    if layer is None:
        return [], [], []
    n_chunks = math.gcd(n_steps, CAST_CHUNKS)
    steps_per_chunk = n_steps // n_chunks
    chunk_of_step = lambda *g: step_of_grid(*g) // steps_per_chunk
    in_specs, out_specs, out_shapes = [], [], []
    for rows, cols in _FFN_W_SHAPES:
        chunk = rows // n_chunks
        in_specs.append(pl.BlockSpec(
            (None, chunk, cols), lambda *g: (layer, chunk_of_step(*g), 0)))
        out_specs.append(pl.BlockSpec(
            (chunk, cols), lambda *g: (chunk_of_step(*g), 0)))
        out_shapes.append(jax.ShapeDtypeStruct((rows, cols), BF16))
    return in_specs, out_specs, out_shapes


def _causal_conv4(x, tail, w_ref, b_ref):
    x_ext = jnp.concatenate([tail, x], axis=0)
    prev_ext = pltpu.roll(x_ext, 1, 0)
    near = w_ref[3:4, :] * x + w_ref[2:3, :] * prev_ext[SUBLANES:, :]
    far_ext = w_ref[1:2, :] * x_ext + w_ref[0:1, :] * prev_ext
    return near + pltpu.roll(far_ext, 2, 0)[SUBLANES:, :] + b_ref[...]


def _heads_to_lanes(xc):
    rows = xc.shape[0]
    lane = lax.broadcasted_iota(jnp.int32, (rows, LANES), 1)
    per_tile = LANES // SSD_HEAD_DIM
    parts = []
    for t in range(SSD_INNER // LANES):
        tile = jnp.broadcast_to(xc[:, t * per_tile:t * per_tile + 1], (rows, LANES))
        for i in range(1, per_tile):
            h = t * per_tile + i
            tile = jnp.where(lane < i * SSD_HEAD_DIM, tile,
                             jnp.broadcast_to(xc[:, h:h + 1], (rows, LANES)))
        parts.append(tile)
    return jnp.concatenate(parts, axis=1)


def _ssd_body(xbc_ref, z_ref, dt_ref, convw_ref, convb_ref, dtb_ref, alog_ref,
              dskip_ref, gn_ref, o_ref, tail_ref, state_ref):
    L = SSD_CHUNK
    HP = SSD_INNER
    P = SSD_HEAD_DIM
    N = SSD_STATE
    GW = HP // SSD_GROUPS
    HPG = SSD_HEADS // SSD_GROUPS

    @pl.when(pl.program_id(1) == 0)
    def _():
        tail_ref[...] = jnp.zeros_like(tail_ref)
        state_ref[...] = jnp.zeros_like(state_ref)

    r = lax.broadcasted_iota(jnp.int32, (L, L), 0)
    s = lax.broadcasted_iota(jnp.int32, (L, L), 1)
    causal = s <= r
    tril = causal.astype(F32)
    pair_lane = lax.broadcasted_iota(jnp.int32, (L, 2 * P), 1)
    neg_a = -jnp.exp(alog_ref[...])

    def chunk(c, tail):
        r0 = pl.multiple_of(c * L, L)
        rows = pl.ds(r0, L)
        x = xbc_ref[rows, :]
        xc = _silu(_causal_conv4(x, tail, convw_ref, convb_ref))
        xs = xc[:, :HP]
        bm = xc[:, HP:HP + SSD_BC].astype(BF16)
        cm = xc[:, HP + SSD_BC:].astype(BF16)

        dt_c = _softplus(dt_ref[rows, :] + dtb_ref[...])
        acs_c = jnp.dot(tril, dt_c * neg_a, precision=HIGHEST,
                        preferred_element_type=F32)
        acs_t = acs_c.T
        dt = _heads_to_lanes(dt_c)
        acs = _heads_to_lanes(acs_c)
        a_last = acs[L - 1:L, :]
        decay_out = jnp.exp(acs)
        decay_in = jnp.exp(a_last - acs)
        chunk_decay = jnp.exp(a_last)

        xdt = xs * dt
        xdt_b = xdt.astype(BF16)
        xw_b = (xdt * decay_in).astype(BF16)

        y_parts = []
        for g in range(SSD_GROUPS):
            bg = bm[:, g * N:(g + 1) * N]
            cg = cm[:, g * N:(g + 1) * N]
            gsl = slice(g * GW, (g + 1) * GW)
            cb = lax.dot_general(cg, bg, (((1,), (1,)), ((), ())),
                                 preferred_element_type=F32)
            st = state_ref[:, gsl]
            y_off = jnp.dot(cg, st.astype(BF16), preferred_element_type=F32)
            new_st = lax.dot_general(bg, xw_b[:, gsl], (((0,), (0,)), ((), ())),
                                     preferred_element_type=F32)
            state_ref[:, gsl] = st * chunk_decay[:, gsl] + new_st
            for pair in range(HPG // 2):
                lo = g * GW + pair * 2 * P
                x_pair = xdt_b[:, lo:lo + 2 * P]
                ys = []
                for i in range(2):
                    h = lo // P + i
                    seg = jnp.exp(jnp.where(
                        causal, acs_c[:, h:h + 1] - acs_t[h:h + 1, :], -jnp.inf))
                    ys.append(jnp.dot((cb * seg).astype(BF16), x_pair,
                                      preferred_element_type=F32))
                y_diag = jnp.where(pair_lane < P, ys[0], ys[1])
                psl = slice(pair * 2 * P, (pair + 1) * 2 * P)
                y_parts.append(
                    y_diag + y_off[:, psl] * decay_out[:, lo:lo + 2 * P])
        y = jnp.concatenate(y_parts, axis=-1) + dskip_ref[...] * xs
        o_ref[rows, :] = _rms(y * _silu(z_ref[rows, :])) * gn_ref[...]
        return x[L - SUBLANES:, :]

    n_chunks = xbc_ref.shape[0] // L
    tail_ref[...] = lax.fori_loop(0, n_chunks, chunk, tail_ref[...], unroll=True)


def _ssd(xbc, z, dt, convw, convb, dtb, alog, dskip, gn, cast_ws, cast_layer,
         batch, seq):
    nblk = seq // SSD_STEP_ROWS
    row = lambda b, c: (b * nblk + c, 0)
    w_in_specs, w_out_specs, w_out_shapes = _weight_cast_specs(
        cast_layer, batch * nblk, lambda b, c: b * nblk + c)
    return pl.pallas_call(
        _with_weight_cast(_ssd_body, 9, 1, len(w_in_specs)),
        grid=(batch, nblk),
        in_specs=[
            pl.BlockSpec((SSD_STEP_ROWS, SSD_XBC), row),
            pl.BlockSpec((SSD_STEP_ROWS, SSD_INNER), row),
            pl.BlockSpec((SSD_STEP_ROWS, DT_PAD), row),
            _const_spec((SSD_CONV, SSD_XBC)),
            _const_spec((1, SSD_XBC)),
            _const_spec((1, DT_PAD)),
            _const_spec((1, DT_PAD)),
            _const_spec((1, SSD_INNER)),
            _const_spec((1, SSD_INNER)),
            *w_in_specs,
        ],
        out_specs=[pl.BlockSpec((SSD_STEP_ROWS, SSD_INNER), row), *w_out_specs],
        out_shape=[jax.ShapeDtypeStruct((batch * seq, SSD_INNER), F32),
                   *w_out_shapes],
        scratch_shapes=[
            pltpu.VMEM((SUBLANES, SSD_XBC), F32),
            pltpu.VMEM((SSD_STATE, SSD_INNER), F32),
        ],
        compiler_params=_params("arbitrary", "arbitrary"),
        name="ssd",
    )(xbc, z, dt, convw, convb, dtb, alog, dskip, gn, *cast_ws)


def _rope(t, cos, sin_up, sin_dn):
    half = ROPE_DIM // 2
    return (t * cos + pltpu.roll(t, half, 1) * sin_up
            + pltpu.roll(t, ATTN_WIDTH - half, 1) * sin_dn)


def _moba_body(q_ref, k_ref, v_ref, cosq_ref, supq_ref, sdnq_ref,
               cosk_ref, supk_ref, sdnk_ref, o_ref,
               ks_ref, vt_ref, kmean_ref, sel_ref, s_ref):
    seq = k_ref.shape[0]
    nb = seq // MOBA_BLOCK
    Dh = ATTN_HEAD_DIM
    BL = MOBA_BLOCK
    H = ATTN_HEADS
    step = pl.program_id(1)
    nt = (((1,), (1,)), ((), ()))

    head_of_lane = lax.broadcasted_iota(jnp.int32, (1, ATTN_WIDTH), 1) // Dh

    @pl.when(step == 0)
    def _():
        for j in range(nb):
            rows = slice(j * BL, (j + 1) * BL)
            kj = _rope(k_ref[rows, :], cosk_ref[rows, :], supk_ref[rows, :],
                       sdnk_ref[rows, :])
            kmean = jnp.mean(kj, axis=0, keepdims=True)
            for h in range(H):
                kmean_ref[h * nb + j:h * nb + j + 1, :] = jnp.where(
                    head_of_lane == h, kmean, 0.0)
            kb = kj.astype(BF16)
            for h in range(H):
                ks_ref[h, rows, :] = kb[:, h * Dh:(h + 1) * Dh]
            vt = v_ref[rows, :].T.astype(BF16)
            for h in range(H):
                vt_ref[h, 0:Dh, rows] = vt[h * Dh:(h + 1) * Dh, :]
        vt_ref[:, Dh:, :] = jnp.ones((H, BF16_SUBLANES, seq), BF16)

    def over_blocks(fn, n, carry):
        start = 0
        for width in MOBA_UNROLLS:
            def body(jj, carry, start=start, width=width):
                for t in range(width):
                    carry = fn(start + width * jj + t, carry)
                return carry
            groups = lax.div(n - start, width)
            carry = lax.fori_loop(0, groups, body, carry)
            start = start + groups * width
        return carry

    def attend(i, rows):
        qf = _rope(q_ref[rows, :], cosq_ref[rows, :], supq_ref[rows, :],
                   sdnq_ref[rows, :])
        qb = (qf * (Dh ** -0.5)).astype(BF16)
        blk = lax.broadcasted_iota(jnp.int32, (nb, BL), 0)
        past = blk < i
        gates = lax.dot_general(kmean_ref[...], qf, nt, precision=HIGHEST,
                                preferred_element_type=F32)
        for h in range(H):
            gate = jnp.where(past, gates[h * nb:(h + 1) * nb, :], -jnp.inf)
            rank = jnp.zeros((nb, BL), jnp.int32)
            for j2 in range(nb):
                g2 = gate[j2:j2 + 1, :]
                ahead = (g2 > gate) | ((g2 == gate) & (j2 < blk))
                rank = rank + ahead.astype(jnp.int32)
            sel_ref[h] = (past & (rank < MOBA_TOPK)).astype(F32)

        qh = [qb[:, h * Dh:(h + 1) * Dh] for h in range(H)]

        def masked_scores(h, j, keep):
            k0 = pl.multiple_of(j * BL, BL)
            st = lax.dot_general(ks_ref[h, pl.ds(k0, BL), :], qh[h], nt,
                                 preferred_element_type=F32)
            st = jnp.where(keep, st * LOG2E, -jnp.inf)
            s_ref[h, j] = st
            return jnp.max(st, axis=0, keepdims=True)

        own_mask = (lax.broadcasted_iota(jnp.int32, (BL, BL), 0) <=
                    lax.broadcasted_iota(jnp.int32, (BL, BL), 1))
        m_own = tuple(masked_scores(h, i, own_mask) for h in range(H))

        def pass1(j, ms):
            return tuple(
                jnp.maximum(ms[h], masked_scores(
                    h, j, sel_ref[h, pl.ds(j, 1), :] > 0.5)) for h in range(H))

        ms = over_blocks(pass1, i, m_own)

        def pass2(j, accs):
            k0 = pl.multiple_of(j * BL, BL)
            return tuple(
                accs[h] + jnp.dot(
                    vt_ref[h, :, pl.ds(k0, BL)],
                    jnp.exp2(s_ref[h, j] - ms[h]).astype(BF16),
                    preferred_element_type=F32)
                for h in range(H))

        init = (jnp.zeros((Dh + BF16_SUBLANES, BL), F32),) * H
        accs = over_blocks(pass2, i + 1, init)
        out_t = jnp.concatenate(
            [acc[:Dh] / acc[Dh:Dh + 1] for acc in accs], axis=0)
        o_ref[rows, :] = out_t.T

    for sub in range(MOBA_Q_PER_STEP):
        attend(step * MOBA_Q_PER_STEP + sub, slice(sub * BL, (sub + 1) * BL))


def _moba(q, k, v, tables, cast_ws, cast_layer, batch, seq):
    nb = seq // MOBA_BLOCK
    full = lambda b, i: (b, 0)
    steps = nb // MOBA_Q_PER_STEP
    qrow = lambda b, i: (b * steps + i, 0)
    trow = lambda b, i: (i, 0)
    const = lambda b, i: (0, 0)
    qspec = pl.BlockSpec((MOBA_Q_PER_STEP * MOBA_BLOCK, ATTN_WIDTH), qrow)
    tq = pl.BlockSpec((MOBA_Q_PER_STEP * MOBA_BLOCK, ATTN_WIDTH), trow)
    tk = pl.BlockSpec((seq, ATTN_WIDTH), const)
    kv = pl.BlockSpec((seq, ATTN_WIDTH), full)
    w_in_specs, w_out_specs, w_out_shapes = _weight_cast_specs(
        cast_layer, batch * steps, lambda b, i: b * steps + i)
    return pl.pallas_call(
        _with_weight_cast(_moba_body, 9, 1, len(w_in_specs)),
        grid=(batch, steps),
        in_specs=[qspec, kv, kv, tq, tq, tq, tk, tk, tk, *w_in_specs],
        out_specs=[qspec, *w_out_specs],
        out_shape=[jax.ShapeDtypeStruct((batch * seq, ATTN_WIDTH), F32),
                   *w_out_shapes],
        scratch_shapes=[
            pltpu.VMEM((ATTN_HEADS, seq, ATTN_HEAD_DIM), BF16),
            pltpu.VMEM((ATTN_HEADS, ATTN_HEAD_DIM + BF16_SUBLANES, seq), BF16),
            pltpu.VMEM((ATTN_HEADS * nb, ATTN_WIDTH), F32),
            pltpu.VMEM((ATTN_HEADS, nb, MOBA_BLOCK), F32),
            pltpu.VMEM((ATTN_HEADS, nb, MOBA_BLOCK, MOBA_BLOCK), F32),
        ],
        compiler_params=_params("arbitrary", "arbitrary"),
        name="moba",
    )(q, k, v, *tables, *tables, *(cast_ws if w_in_specs else ()))


def _outproj_ffn_body(x_ref, ys_ref, yp_ref, ya_ref, w_ref, g_ref,
                      gpre_ref, wg_ref, wu_ref, wd_ref, gpost_ref, o_ref, h_ref):
    o1 = SSD_INNER
    o2 = SSD_INNER + POOL_WIDTH
    tiles = _sub_tiles(OUT_TOKEN_TILE)
    xs = []
    for rows in tiles:
        m = jnp.dot(ys_ref[rows, :].astype(BF16), w_ref[0:o1, :],
                    preferred_element_type=F32)
        m = m + jnp.dot(yp_ref[rows, :].astype(BF16), w_ref[o1:o2, :],
                        preferred_element_type=F32)
        m = m + jnp.dot(ya_ref[rows, :].astype(BF16), w_ref[o2:, :],
                        preferred_element_type=F32)
        xs.append(x_ref[rows, :] + _rms(m) * g_ref[...])
    for rows, x in zip(tiles, xs):
        _ffn_gate_up(x, gpre_ref, wg_ref, wu_ref, h_ref.at[rows])
        o_ref[rows, :] = _ffn_down(x, wd_ref, gpost_ref, h_ref.at[rows])


def _outproj_ffn(x, y_ssd, y_pool, y_attn, w, g, gpre, wg, wu, wd, gpost, layer):
    m = x.shape[0]
    row = lambda i: (i, 0)
    return pl.pallas_call(
        _outproj_ffn_body,
        grid=(m // OUT_TOKEN_TILE,),
        in_specs=[
            pl.BlockSpec((OUT_TOKEN_TILE, D_MODEL), row),
            pl.BlockSpec((OUT_TOKEN_TILE, SSD_INNER), row),
            pl.BlockSpec((OUT_TOKEN_TILE, POOL_WIDTH), row),
            pl.BlockSpec((OUT_TOKEN_TILE, ATTN_WIDTH), row),
            _layer_spec((D_MODEL, D_MODEL), layer),
            _layer_spec((1, D_MODEL), layer),
            _layer_spec((1, D_MODEL), layer),
            _const_spec((D_MODEL, D_FF)),
            _const_spec((D_MODEL, D_FF)),
            _const_spec((D_FF, D_MODEL)),
            _layer_spec((1, D_MODEL), layer),
        ],
        out_specs=pl.BlockSpec((OUT_TOKEN_TILE, D_MODEL), row),
        out_shape=jax.ShapeDtypeStruct((m, D_MODEL), F32),
        scratch_shapes=[pltpu.VMEM((OUT_TOKEN_TILE, D_FF), BF16)],
        compiler_params=_params("parallel"),
        name="outproj_ffn",
    )(x, y_ssd, y_pool, y_attn, w, g, gpre, wg, wu, wd, gpost)


def _rope_tables(seq):
    half = ROPE_DIM // 2
    inv_freq = ROPE_THETA ** (-jnp.arange(0, ROPE_DIM, 2, dtype=F32) / ROPE_DIM)
    ang = jnp.arange(seq, dtype=F32)[:, None] * inv_freq[None, :]
    rest = ATTN_HEAD_DIM - ROPE_DIM
    one = jnp.ones((seq, rest), F32)
    zero = jnp.zeros((seq, rest), F32)
    zh = jnp.zeros((seq, half), F32)
    cos = jnp.concatenate([jnp.cos(ang), jnp.cos(ang), one], axis=-1)
    sin_up = jnp.concatenate([zh, jnp.sin(ang), zero], axis=-1)
    sin_dn = jnp.concatenate([-jnp.sin(ang), zh, zero], axis=-1)
    return tuple(jnp.tile(t, (1, ATTN_HEADS)) for t in (cos, sin_up, sin_dn))


def _split_w_in(w):
    pad = [(0, 0)] * (w.ndim - 1) + [(0, DT_PAD - SSD_HEADS)]
    pieces = [w[..., OFF_XBC:OFF_DT], w[..., OFF_POOL:OFF_Q], w[..., OFF_Z:OFF_XBC],
              w[..., OFF_Q:OFF_K], w[..., OFF_K:OFF_V], w[..., OFF_V:IN_COLS],
              jnp.pad(w[..., OFF_DT:OFF_POOL], pad)]
    return tuple(p.astype(BF16) for p in pieces)


def _per_head_lanes(p):
    return jnp.repeat(p, SSD_HEAD_DIM)[None, :]


def _per_head_cols(p):
    pad = [(0, 0)] * (p.ndim - 1) + [(0, DT_PAD - SSD_HEADS)]
    return jnp.pad(p, pad)[..., None, :]


def _pool_blockdiag(w):
    out = jnp.zeros(w.shape[:-3] + (POOL_WIDTH, POOL_WIDTH), F32)
    for g in range(POOL_GROUPS):
        sl = slice(g * POOL_GROUP_DIM, (g + 1) * POOL_GROUP_DIM)
        out = out.at[..., sl, sl].set(w[..., g, :, :])
    return out.astype(BF16)


def kernel(x, ff1_norm_pre, ff1_w_gate, ff1_w_up, ff1_w_down, ff1_norm_post,
           mix_norm_pre, w_in, conv_w, conv_b, dt_bias, a_log, d_skip, ssd_norm,
           pool_w, pool_scale, w_out, mix_norm_post,
           ff2_norm_pre, ff2_w_gate, ff2_w_up, ff2_w_down, ff2_norm_post):
    batch, seq, d = x.shape
    depth = w_in.shape[0]
    h = x.reshape(batch * seq, d)
    rope_tables = _rope_tables(seq)
    row = lambda p: p[None, :]
    stack_row = lambda p: p[:, None, :]
    bf = lambda w: w.astype(BF16)
    ff1_w = (ff1_w_gate, ff1_w_up, ff1_w_down)
    ff2_w = (ff2_w_gate, ff2_w_up, ff2_w_down)
    g_ff1 = (stack_row(ff1_norm_pre), stack_row(ff1_norm_post))
    g_ff2 = (stack_row(ff2_norm_pre), stack_row(ff2_norm_post))
    w_in_b = _split_w_in(w_in)
    w_out_b = bf(w_out)
    g_mix_pre = stack_row(mix_norm_pre)
    g_mix_post = stack_row(mix_norm_post)
    mix_front = (g_mix_pre, w_in_b, _pool_blockdiag(pool_w), stack_row(pool_scale))
    ff1_b = tuple(bf(w[0]) for w in ff1_w)
    for l in range(depth):
        h, xbc, y_pool, z, q, k, v, dt = _ffn_inproj(
            h, g_ff1[0], *ff1_b, g_ff1[1], *mix_front, l, seq)
        y_ssd, *ff2_b = _ssd(xbc, z, dt, conv_w[l], row(conv_b[l]),
                             _per_head_cols(dt_bias[l]), _per_head_cols(a_log[l]),
                             _per_head_lanes(d_skip[l]), row(ssd_norm[l]),
                             ff2_w, l, batch, seq)
        nxt = l + 1 if l + 1 < depth else None
        y_attn, *ff1_b = _moba(q, k, v, rope_tables, ff1_w, nxt, batch, seq)
        h = _outproj_ffn(h, y_ssd, y_pool, y_attn, w_out_b, g_mix_post,
                         g_ff2[0], *ff2_b, g_ff2[1], l)
    return h.reshape(batch, seq, d)
```

```python
import functools
import math

import jax
import jax.numpy as jnp
from jax import lax
from jax.experimental import pallas as pl
from jax.experimental.pallas import tpu as pltpu

F32 = jnp.float32
BF16 = jnp.bfloat16
HIGHEST = lax.Precision.HIGHEST

D_MODEL = 1024
D_FF = 2816
SSD_INNER = 512
SSD_HEAD_DIM = 64
SSD_HEADS = SSD_INNER // SSD_HEAD_DIM
SSD_GROUPS = 2
SSD_STATE = 128
SSD_CONV = 4
SSD_CHUNK = 128
SSD_BC = SSD_GROUPS * SSD_STATE
SSD_XBC = SSD_INNER + 2 * SSD_BC
POOL_WIDTH = 256
POOL_GROUPS = 4
POOL_GROUP_DIM = POOL_WIDTH // POOL_GROUPS
ATTN_WIDTH = 256
ATTN_HEAD_DIM = 64
ATTN_HEADS = ATTN_WIDTH // ATTN_HEAD_DIM
ROPE_DIM = ATTN_HEAD_DIM // 4
ROPE_THETA = 500000.0
MOBA_BLOCK = 256
MOBA_TOPK = 3
RMS_EPS = 1e-6

OFF_Z = 0
OFF_XBC = OFF_Z + SSD_INNER
OFF_DT = OFF_XBC + SSD_XBC
OFF_POOL = OFF_DT + SSD_HEADS
OFF_Q = OFF_POOL + POOL_WIDTH
OFF_K = OFF_Q + ATTN_WIDTH
OFF_V = OFF_K + ATTN_WIDTH
IN_COLS = OFF_V + ATTN_WIDTH

LANES = 128
SUBLANES = 8
MXU_DIM = 256
VMEM_LIMIT_BYTES = 56 * 1024 * 1024

DT_PAD = LANES
TOKEN_TILE = 512
OUT_TOKEN_TILE = 1024
SUB_TILE = 256
FF_CHUNK = MXU_DIM
SSD_STEP_ROWS = 8 * SSD_CHUNK
MOBA_UNROLLS = (4, 2, 1)
MOBA_Q_PER_STEP = 2
BF16_SUBLANES = 16
LOG2E = math.log2(math.e)


def _rms(x):
    return x * lax.rsqrt(jnp.mean(x * x, axis=-1, keepdims=True) + RMS_EPS)


def _silu(x):
    return x * jax.nn.sigmoid(x)


def _softplus(x):
    return jnp.maximum(x, 0.0) + jnp.log1p(jnp.exp(-jnp.abs(x)))


def _const_spec(shape):
    zeros = (0,) * len(shape)
    return pl.BlockSpec(shape, lambda *_: zeros, pipeline_mode=pl.Buffered(1))


def _layer_spec(shape, layer):
    zeros = (0,) * len(shape)
    return pl.BlockSpec((None,) + shape, lambda *_: (layer,) + zeros,
                        pipeline_mode=pl.Buffered(1))


def _params(*sem):
    return pltpu.CompilerParams(dimension_semantics=sem,
                                vmem_limit_bytes=VMEM_LIMIT_BYTES)


def _ffn_gate_up(x, gpre_ref, wg_ref, wu_ref, h_ref):
    xb = (_rms(x) * gpre_ref[...]).astype(BF16)
    for c in range(0, D_FF, FF_CHUNK):
        sl = slice(c, min(c + FF_CHUNK, D_FF))
        g = jnp.dot(xb, wg_ref[:, sl], preferred_element_type=F32)
        u = jnp.dot(xb, wu_ref[:, sl], preferred_element_type=F32)
        h_ref[:, sl] = (_silu(g) * u).astype(BF16)


def _ffn_down(x, wd_ref, gpost_ref, h_ref):
    f = jnp.dot(h_ref[...], wd_ref[...], preferred_element_type=F32)
    return x + 0.5 * (_rms(f) * gpost_ref[...])


def _sub_tiles(tile_rows):
    return [slice(r0, r0 + SUB_TILE) for r0 in range(0, tile_rows, SUB_TILE)]


_PROJ_WIDTHS = (SSD_XBC, POOL_WIDTH, SSD_INNER, 3 * ATTN_WIDTH, DT_PAD)
_PROJ_OUT_WIDTHS = (SSD_XBC, POOL_WIDTH, SSD_INNER, ATTN_WIDTH, ATTN_WIDTH,
                    ATTN_WIDTH, DT_PAD)
POOL_HALO = 2 ** POOL_GROUPS


def _pool_tile(u, tail, pos, w_ref, scale_ref):
    group = lax.broadcasted_iota(jnp.int32, u.shape, 1) // POOL_GROUP_DIM
    win_sum = jnp.concatenate([tail, u], axis=0)
    mean = jnp.zeros_like(u)
    for g in range(POOL_GROUPS):
        half = 2 ** g
        win_sum = win_sum + pltpu.roll(win_sum, half, 0)
        count = jnp.minimum(pos + 1, 2 * half).astype(F32)
        mean = jnp.where(group == g, win_sum[POOL_HALO:, :] / count, mean)
    d = (mean - u).astype(BF16)
    return jnp.dot(d, w_ref[...], preferred_element_type=F32) * scale_ref[...]


def _ffn_inproj_body(tiles_per_seq, x_ref, gpre_ref, wg_ref, wu_ref, wd_ref,
                     gpost_ref, g_ref, *rest):
    n_proj = len(_PROJ_WIDTHS)
    w_refs = rest[:n_proj]
    (poolw_ref, pscale_ref, o_ref, xbc_ref, yp_ref, z_ref, q_ref, k_ref, v_ref,
     dt_ref, h_ref, utail_ref) = rest[n_proj:]
    tile_in_seq = pl.program_id(0) % tiles_per_seq

    @pl.when(tile_in_seq == 0)
    def _():
        utail_ref[...] = jnp.zeros_like(utail_ref)

    tiles = _sub_tiles(TOKEN_TILE)

    def down(rows):
        x = _ffn_down(x_ref[rows, :], wd_ref, gpost_ref, h_ref.at[rows])
        o_ref[rows, :] = x
        return (_rms(x) * g_ref[...]).astype(BF16)

    def project(rows, xb, utail):
        def proj(i):
            return jnp.dot(xb, w_refs[i][...], preferred_element_type=F32)

        u = proj(1)
        xbc_ref[rows, :] = proj(0)
        z_ref[rows, :] = proj(2)
        qkv = proj(3)
        for i, ref in enumerate((q_ref, k_ref, v_ref)):
            ref[rows, :] = qkv[:, i * ATTN_WIDTH:(i + 1) * ATTN_WIDTH]
        dt_ref[rows, :] = proj(4)
        pos = (tile_in_seq * TOKEN_TILE + rows.start
               + lax.broadcasted_iota(jnp.int32, u.shape, 0))
        yp_ref[rows, :] = _pool_tile(u, utail, pos, poolw_ref, pscale_ref)
        return u[SUB_TILE - POOL_HALO:, :]

    utail = utail_ref[...]
    pending = None
    for rows in tiles:
        _ffn_gate_up(x_ref[rows, :], gpre_ref, wg_ref, wu_ref, h_ref.at[rows])
        xb = down(rows)
        if pending is not None:
            utail = project(*pending, utail)
        pending = (rows, xb)
    utail_ref[...] = project(*pending, utail)


def _ffn_inproj(x, gpre, wg, wu, wd, gpost, g, ws, poolw, pscale, layer, seq):
    m = x.shape[0]
    row = lambda i: (i, 0)
    widths = (D_MODEL,) + _PROJ_OUT_WIDTHS
    return pl.pallas_call(
        functools.partial(_ffn_inproj_body, seq // TOKEN_TILE),
        grid=(m // TOKEN_TILE,),
        in_specs=[
            pl.BlockSpec((TOKEN_TILE, D_MODEL), row),
            _layer_spec((1, D_MODEL), layer),
            _const_spec((D_MODEL, D_FF)),
            _const_spec((D_MODEL, D_FF)),
            _const_spec((D_FF, D_MODEL)),
            _layer_spec((1, D_MODEL), layer),
            _layer_spec((1, D_MODEL), layer),
            *[_layer_spec((D_MODEL, width), layer) for width in _PROJ_WIDTHS],
            _layer_spec((POOL_WIDTH, POOL_WIDTH), layer),
            _layer_spec((1, POOL_WIDTH), layer),
        ],
        out_specs=[pl.BlockSpec((TOKEN_TILE, width), row) for width in widths],
        out_shape=[jax.ShapeDtypeStruct((m, width), F32) for width in widths],
        scratch_shapes=[
            pltpu.VMEM((TOKEN_TILE, D_FF), BF16),
            pltpu.VMEM((POOL_HALO, POOL_WIDTH), F32),
        ],
        compiler_params=_params("arbitrary"),
        name="ffn_inproj",
    )(x, gpre, wg, wu, wd, gpost, g, *ws, poolw, pscale)


CAST_CHUNKS = 16
_FFN_W_SHAPES = ((D_MODEL, D_FF), (D_MODEL, D_FF), (D_FF, D_MODEL))


def _with_weight_cast(body, n_in, n_out, n_w):
    def wrapped(*refs):
        ins, refs = refs[:n_in], refs[n_in:]
        w_ins, refs = refs[:n_w], refs[n_w:]
        outs, refs = refs[:n_out], refs[n_out:]
        w_outs, scratch = refs[:n_w], refs[n_w:]
        body(*ins, *outs, *scratch)
        for w_in, w_out in zip(w_ins, w_outs):
            w_out[...] = w_in[...].astype(BF16)

    return wrapped


def _weight_cast_specs(layer, n_steps, step_of_grid):
    if layer is None:
        return [], [], []
    n_chunks = math.gcd(n_steps, CAST_CHUNKS)
    steps_per_chunk = n_steps // n_chunks
    chunk_of_step = lambda *g: step_of_grid(*g) // steps_per_chunk
    in_specs, out_specs, out_shapes = [], [], []
    for rows, cols in _FFN_W_SHAPES:
        chunk = rows // n_chunks
        in_specs.append(pl.BlockSpec(
            (None, chunk, cols), lambda *g: (layer, chunk_of_step(*g), 0)))
        out_specs.append(pl.BlockSpec(
            (chunk, cols), lambda *g: (chunk_of_step(*g), 0)))
        out_shapes.append(jax.ShapeDtypeStruct((rows, cols), BF16))
    return in_specs, out_specs, out_shapes


def _causal_conv4(x, tail, w_ref, b_ref):
    x_ext = jnp.concatenate([tail, x], axis=0)
    prev_ext = pltpu.roll(x_ext, 1, 0)
    near = w_ref[3:4, :] * x + w_ref[2:3, :] * prev_ext[SUBLANES:, :]
    far_ext = w_ref[1:2, :] * x_ext + w_ref[0:1, :] * prev_ext
    return near + pltpu.roll(far_ext, 2, 0)[SUBLANES:, :] + b_ref[...]


def _heads_to_lanes(xc):
    rows = xc.shape[0]
    lane = lax.broadcasted_iota(jnp.int32, (rows, LANES), 1)
    per_tile = LANES // SSD_HEAD_DIM
    parts = []
    for t in range(SSD_INNER // LANES):
        tile = jnp.broadcast_to(xc[:, t * per_tile:t * per_tile + 1], (rows, LANES))
        for i in range(1, per_tile):
            h = t * per_tile + i
            tile = jnp.where(lane < i * SSD_HEAD_DIM, tile,
                             jnp.broadcast_to(xc[:, h:h + 1], (rows, LANES)))
        parts.append(tile)
    return jnp.concatenate(parts, axis=1)


def _ssd_body(xbc_ref, z_ref, dt_ref, convw_ref, convb_ref, dtb_ref, alog_ref,
              dskip_ref, gn_ref, o_ref, tail_ref, state_ref):
    L = SSD_CHUNK
    HP = SSD_INNER
    P = SSD_HEAD_DIM
    N = SSD_STATE
    GW = HP // SSD_GROUPS
    HPG = SSD_HEADS // SSD_GROUPS

    @pl.when(pl.program_id(1) == 0)
    def _():
        tail_ref[...] = jnp.zeros_like(tail_ref)
        state_ref[...] = jnp.zeros_like(state_ref)

    r = lax.broadcasted_iota(jnp.int32, (L, L), 0)
    s = lax.broadcasted_iota(jnp.int32, (L, L), 1)
    causal = s <= r
    tril = causal.astype(F32)
    pair_lane = lax.broadcasted_iota(jnp.int32, (L, 2 * P), 1)
    neg_a = -jnp.exp(alog_ref[...])

    def chunk(c, tail):
        r0 = pl.multiple_of(c * L, L)
        rows = pl.ds(r0, L)
        x = xbc_ref[rows, :]
        xc = _silu(_causal_conv4(x, tail, convw_ref, convb_ref))
        xs = xc[:, :HP]
        bm = xc[:, HP:HP + SSD_BC].astype(BF16)
        cm = xc[:, HP + SSD_BC:].astype(BF16)

        dt_c = _softplus(dt_ref[rows, :] + dtb_ref[...])
        acs_c = jnp.dot(tril, dt_c * neg_a, precision=HIGHEST,
                        preferred_element_type=F32)
        acs_t = acs_c.T
        dt = _heads_to_lanes(dt_c)
        acs = _heads_to_lanes(acs_c)
        a_last = acs[L - 1:L, :]
        decay_out = jnp.exp(acs)
        decay_in = jnp.exp(a_last - acs)
        chunk_decay = jnp.exp(a_last)

        xdt = xs * dt
        xdt_b = xdt.astype(BF16)
        xw_b = (xdt * decay_in).astype(BF16)

        y_parts = []
        for g in range(SSD_GROUPS):
            bg = bm[:, g * N:(g + 1) * N]
            cg = cm[:, g * N:(g + 1) * N]
            gsl = slice(g * GW, (g + 1) * GW)
            cb = lax.dot_general(cg, bg, (((1,), (1,)), ((), ())),
                                 preferred_element_type=F32)
            st = state_ref[:, gsl]
            y_off = jnp.dot(cg, st.astype(BF16), preferred_element_type=F32)
            new_st = lax.dot_general(bg, xw_b[:, gsl], (((0,), (0,)), ((), ())),
                                     preferred_element_type=F32)
            state_ref[:, gsl] = st * chunk_decay[:, gsl] + new_st
            for pair in range(HPG // 2):
                lo = g * GW + pair * 2 * P
                x_pair = xdt_b[:, lo:lo + 2 * P]
                ys = []
                for i in range(2):
                    h = lo // P + i
                    seg = jnp.exp(jnp.where(
                        causal, acs_c[:, h:h + 1] - acs_t[h:h + 1, :], -jnp.inf))
                    ys.append(jnp.dot((cb * seg).astype(BF16), x_pair,
                                      preferred_element_type=F32))
                y_diag = jnp.where(pair_lane < P, ys[0], ys[1])
                psl = slice(pair * 2 * P, (pair + 1) * 2 * P)
                y_parts.append(
                    y_diag + y_off[:, psl] * decay_out[:, lo:lo + 2 * P])
        y = jnp.concatenate(y_parts, axis=-1) + dskip_ref[...] * xs
        o_ref[rows, :] = _rms(y * _silu(z_ref[rows, :])) * gn_ref[...]
        return x[L - SUBLANES:, :]

    n_chunks = xbc_ref.shape[0] // L
    tail_ref[...] = lax.fori_loop(0, n_chunks, chunk, tail_ref[...], unroll=True)


def _ssd(xbc, z, dt, convw, convb, dtb, alog, dskip, gn, cast_ws, cast_layer,
         batch, seq):
    nblk = seq // SSD_STEP_ROWS
    row = lambda b, c: (b * nblk + c, 0)
    w_in_specs, w_out_specs, w_out_shapes = _weight_cast_specs(
        cast_layer, batch * nblk, lambda b, c: b * nblk + c)
    return pl.pallas_call(
        _with_weight_cast(_ssd_body, 9, 1, len(w_in_specs)),
        grid=(batch, nblk),
        in_specs=[
            pl.BlockSpec((SSD_STEP_ROWS, SSD_XBC), row),
            pl.BlockSpec((SSD_STEP_ROWS, SSD_INNER), row),
            pl.BlockSpec((SSD_STEP_ROWS, DT_PAD), row),
            _const_spec((SSD_CONV, SSD_XBC)),
            _const_spec((1, SSD_XBC)),
            _const_spec((1, DT_PAD)),
            _const_spec((1, DT_PAD)),
            _const_spec((1, SSD_INNER)),
            _const_spec((1, SSD_INNER)),
            *w_in_specs,
        ],
        out_specs=[pl.BlockSpec((SSD_STEP_ROWS, SSD_INNER), row), *w_out_specs],
        out_shape=[jax.ShapeDtypeStruct((batch * seq, SSD_INNER), F32),
                   *w_out_shapes],
        scratch_shapes=[
            pltpu.VMEM((SUBLANES, SSD_XBC), F32),
            pltpu.VMEM((SSD_STATE, SSD_INNER), F32),
        ],
        compiler_params=_params("arbitrary", "arbitrary"),
        name="ssd",
    )(xbc, z, dt, convw, convb, dtb, alog, dskip, gn, *cast_ws)


def _rope(t, cos, sin_up, sin_dn):
    half = ROPE_DIM // 2
    return (t * cos + pltpu.roll(t, half, 1) * sin_up
            + pltpu.roll(t, ATTN_WIDTH - half, 1) * sin_dn)


def _moba_body(q_ref, k_ref, v_ref, cosq_ref, supq_ref, sdnq_ref,
               cosk_ref, supk_ref, sdnk_ref, o_ref,
               ks_ref, vt_ref, kmean_ref, sel_ref, s_ref):
    seq = k_ref.shape[0]
    nb = seq // MOBA_BLOCK
    Dh = ATTN_HEAD_DIM
    BL = MOBA_BLOCK
    H = ATTN_HEADS
    step = pl.program_id(1)
    nt = (((1,), (1,)), ((), ()))

    head_of_lane = lax.broadcasted_iota(jnp.int32, (1, ATTN_WIDTH), 1) // Dh

    @pl.when(step == 0)
    def _():
        for j in range(nb):
            rows = slice(j * BL, (j + 1) * BL)
            kj = _rope(k_ref[rows, :], cosk_ref[rows, :], supk_ref[rows, :],
                       sdnk_ref[rows, :])
            kmean = jnp.mean(kj, axis=0, keepdims=True)
            for h in range(H):
                kmean_ref[h * nb + j:h * nb + j + 1, :] = jnp.where(
                    head_of_lane == h, kmean, 0.0)
            kb = kj.astype(BF16)
            for h in range(H):
                ks_ref[h, rows, :] = kb[:, h * Dh:(h + 1) * Dh]
            vt = v_ref[rows, :].T.astype(BF16)
            for h in range(H):
                vt_ref[h, 0:Dh, rows] = vt[h * Dh:(h + 1) * Dh, :]
        vt_ref[:, Dh:, :] = jnp.ones((H, BF16_SUBLANES, seq), BF16)

    def over_blocks(fn, n, carry):
        start = 0
        for width in MOBA_UNROLLS:
            def body(jj, carry, start=start, width=width):
                for t in range(width):
                    carry = fn(start + width * jj + t, carry)
                return carry
            groups = lax.div(n - start, width)
            carry = lax.fori_loop(0, groups, body, carry)
            start = start + groups * width
        return carry

    def attend(i, rows):
        qf = _rope(q_ref[rows, :], cosq_ref[rows, :], supq_ref[rows, :],
                   sdnq_ref[rows, :])
        qb = (qf * (Dh ** -0.5)).astype(BF16)
        blk = lax.broadcasted_iota(jnp.int32, (nb, BL), 0)
        past = blk < i
        gates = lax.dot_general(kmean_ref[...], qf, nt, precision=HIGHEST,
                                preferred_element_type=F32)
        for h in range(H):
            gate = jnp.where(past, gates[h * nb:(h + 1) * nb, :], -jnp.inf)
            rank = jnp.zeros((nb, BL), jnp.int32)
            for j2 in range(nb):
                g2 = gate[j2:j2 + 1, :]
                ahead = (g2 > gate) | ((g2 == gate) & (j2 < blk))
                rank = rank + ahead.astype(jnp.int32)
            sel_ref[h] = (past & (rank < MOBA_TOPK)).astype(F32)

        qh = [qb[:, h * Dh:(h + 1) * Dh] for h in range(H)]

        def masked_scores(h, j, keep):
            k0 = pl.multiple_of(j * BL, BL)
            st = lax.dot_general(ks_ref[h, pl.ds(k0, BL), :], qh[h], nt,
                                 preferred_element_type=F32)
            st = jnp.where(keep, st * LOG2E, -jnp.inf)
            s_ref[h, j] = st
            return jnp.max(st, axis=0, keepdims=True)

        own_mask = (lax.broadcasted_iota(jnp.int32, (BL, BL), 0) <=
                    lax.broadcasted_iota(jnp.int32, (BL, BL), 1))
        m_own = tuple(masked_scores(h, i, own_mask) for h in range(H))

        def pass1(j, ms):
            return tuple(
                jnp.maximum(ms[h], masked_scores(
                    h, j, sel_ref[h, pl.ds(j, 1), :] > 0.5)) for h in range(H))

        ms = over_blocks(pass1, i, m_own)

        def pass2(j, accs):
            k0 = pl.multiple_of(j * BL, BL)
            return tuple(
                accs[h] + jnp.dot(
                    vt_ref[h, :, pl.ds(k0, BL)],
                    jnp.exp2(s_ref[h, j] - ms[h]).astype(BF16),
                    preferred_element_type=F32)
                for h in range(H))

        init = (jnp.zeros((Dh + BF16_SUBLANES, BL), F32),) * H
        accs = over_blocks(pass2, i + 1, init)
        out_t = jnp.concatenate(
            [acc[:Dh] / acc[Dh:Dh + 1] for acc in accs], axis=0)
        o_ref[rows, :] = out_t.T

    for sub in range(MOBA_Q_PER_STEP):
        attend(step * MOBA_Q_PER_STEP + sub, slice(sub * BL, (sub + 1) * BL))


def _moba(q, k, v, tables, cast_ws, cast_layer, batch, seq):
    nb = seq // MOBA_BLOCK
    full = lambda b, i: (b, 0)
    steps = nb // MOBA_Q_PER_STEP
    qrow = lambda b, i: (b * steps + i, 0)
    trow = lambda b, i: (i, 0)
    const = lambda b, i: (0, 0)
    qspec = pl.BlockSpec((MOBA_Q_PER_STEP * MOBA_BLOCK, ATTN_WIDTH), qrow)
    tq = pl.BlockSpec((MOBA_Q_PER_STEP * MOBA_BLOCK, ATTN_WIDTH), trow)
    tk = pl.BlockSpec((seq, ATTN_WIDTH), const)
    kv = pl.BlockSpec((seq, ATTN_WIDTH), full)
    w_in_specs, w_out_specs, w_out_shapes = _weight_cast_specs(
        cast_layer, batch * steps, lambda b, i: b * steps + i)
    return pl.pallas_call(
        _with_weight_cast(_moba_body, 9, 1, len(w_in_specs)),
        grid=(batch, steps),
        in_specs=[qspec, kv, kv, tq, tq, tq, tk, tk, tk, *w_in_specs],
        out_specs=[qspec, *w_out_specs],
        out_shape=[jax.ShapeDtypeStruct((batch * seq, ATTN_WIDTH), F32),
                   *w_out_shapes],
        scratch_shapes=[
            pltpu.VMEM((ATTN_HEADS, seq, ATTN_HEAD_DIM), BF16),
            pltpu.VMEM((ATTN_HEADS, ATTN_HEAD_DIM + BF16_SUBLANES, seq), BF16),
            pltpu.VMEM((ATTN_HEADS * nb, ATTN_WIDTH), F32),
            pltpu.VMEM((ATTN_HEADS, nb, MOBA_BLOCK), F32),
            pltpu.VMEM((ATTN_HEADS, nb, MOBA_BLOCK, MOBA_BLOCK), F32),
        ],
        compiler_params=_params("arbitrary", "arbitrary"),
        name="moba",
    )(q, k, v, *tables, *tables, *(cast_ws if w_in_specs else ()))


def _outproj_ffn_body(x_ref, ys_ref, yp_ref, ya_ref, w_ref, g_ref,
                      gpre_ref, wg_ref, wu_ref, wd_ref, gpost_ref, o_ref, h_ref):
    o1 = SSD_INNER
    o2 = SSD_INNER + POOL_WIDTH
    tiles = _sub_tiles(OUT_TOKEN_TILE)
    xs = []
    for rows in tiles:
        m = jnp.dot(ys_ref[rows, :].astype(BF16), w_ref[0:o1, :],
                    preferred_element_type=F32)
        m = m + jnp.dot(yp_ref[rows, :].astype(BF16), w_ref[o1:o2, :],
                        preferred_element_type=F32)
        m = m + jnp.dot(ya_ref[rows, :].astype(BF16), w_ref[o2:, :],
                        preferred_element_type=F32)
        xs.append(x_ref[rows, :] + _rms(m) * g_ref[...])
    for rows, x in zip(tiles, xs):
        _ffn_gate_up(x, gpre_ref, wg_ref, wu_ref, h_ref.at[rows])
        o_ref[rows, :] = _ffn_down(x, wd_ref, gpost_ref, h_ref.at[rows])


def _outproj_ffn(x, y_ssd, y_pool, y_attn, w, g, gpre, wg, wu, wd, gpost, layer):
    m = x.shape[0]
    row = lambda i: (i, 0)
    return pl.pallas_call(
        _outproj_ffn_body,
        grid=(m // OUT_TOKEN_TILE,),
        in_specs=[
            pl.BlockSpec((OUT_TOKEN_TILE, D_MODEL), row),
            pl.BlockSpec((OUT_TOKEN_TILE, SSD_INNER), row),
            pl.BlockSpec((OUT_TOKEN_TILE, POOL_WIDTH), row),
            pl.BlockSpec((OUT_TOKEN_TILE, ATTN_WIDTH), row),
            _layer_spec((D_MODEL, D_MODEL), layer),
            _layer_spec((1, D_MODEL), layer),
            _layer_spec((1, D_MODEL), layer),
            _const_spec((D_MODEL, D_FF)),
            _const_spec((D_MODEL, D_FF)),
            _const_spec((D_FF, D_MODEL)),
            _layer_spec((1, D_MODEL), layer),
        ],
        out_specs=pl.BlockSpec((OUT_TOKEN_TILE, D_MODEL), row),
        out_shape=jax.ShapeDtypeStruct((m, D_MODEL), F32),
        scratch_shapes=[pltpu.VMEM((OUT_TOKEN_TILE, D_FF), BF16)],
        compiler_params=_params("parallel"),
        name="outproj_ffn",
    )(x, y_ssd, y_pool, y_attn, w, g, gpre, wg, wu, wd, gpost)


def _rope_tables(seq):
    half = ROPE_DIM // 2
    inv_freq = ROPE_THETA ** (-jnp.arange(0, ROPE_DIM, 2, dtype=F32) / ROPE_DIM)
    ang = jnp.arange(seq, dtype=F32)[:, None] * inv_freq[None, :]
    rest = ATTN_HEAD_DIM - ROPE_DIM
    one = jnp.ones((seq, rest), F32)
    zero = jnp.zeros((seq, rest), F32)
    zh = jnp.zeros((seq, half), F32)
    cos = jnp.concatenate([jnp.cos(ang), jnp.cos(ang), one], axis=-1)
    sin_up = jnp.concatenate([zh, jnp.sin(ang), zero], axis=-1)
    sin_dn = jnp.concatenate([-jnp.sin(ang), zh, zero], axis=-1)
    return tuple(jnp.tile(t, (1, ATTN_HEADS)) for t in (cos, sin_up, sin_dn))


def _split_w_in(w):
    pad = [(0, 0)] * (w.ndim - 1) + [(0, DT_PAD - SSD_HEADS)]
    pieces = [w[..., OFF_XBC:OFF_DT], w[..., OFF_POOL:OFF_Q], w[..., OFF_Z:OFF_XBC],
              w[..., OFF_Q:IN_COLS], jnp.pad(w[..., OFF_DT:OFF_POOL], pad)]
    return tuple(p.astype(BF16) for p in pieces)


def _per_head_lanes(p):
    return jnp.repeat(p, SSD_HEAD_DIM)[None, :]


def _per_head_cols(p):
    pad = [(0, 0)] * (p.ndim - 1) + [(0, DT_PAD - SSD_HEADS)]
    return jnp.pad(p, pad)[..., None, :]


def _pool_blockdiag(w):
    out = jnp.zeros(w.shape[:-3] + (POOL_WIDTH, POOL_WIDTH), F32)
    for g in range(POOL_GROUPS):
        sl = slice(g * POOL_GROUP_DIM, (g + 1) * POOL_GROUP_DIM)
        out = out.at[..., sl, sl].set(w[..., g, :, :])
    return out.astype(BF16)


def kernel(x, ff1_norm_pre, ff1_w_gate, ff1_w_up, ff1_w_down, ff1_norm_post,
           mix_norm_pre, w_in, conv_w, conv_b, dt_bias, a_log, d_skip, ssd_norm,
           pool_w, pool_scale, w_out, mix_norm_post,
           ff2_norm_pre, ff2_w_gate, ff2_w_up, ff2_w_down, ff2_norm_post):
    batch, seq, d = x.shape
    depth = w_in.shape[0]
    h = x.reshape(batch * seq, d)
    rope_tables = _rope_tables(seq)
    row = lambda p: p[None, :]
    stack_row = lambda p: p[:, None, :]
    bf = lambda w: w.astype(BF16)
    ff1_w = (ff1_w_gate, ff1_w_up, ff1_w_down)
    ff2_w = (ff2_w_gate, ff2_w_up, ff2_w_down)
    g_ff1 = (stack_row(ff1_norm_pre), stack_row(ff1_norm_post))
    g_ff2 = (stack_row(ff2_norm_pre), stack_row(ff2_norm_post))
    w_in_b = _split_w_in(w_in)
    w_out_b = bf(w_out)
    g_mix_pre = stack_row(mix_norm_pre)
    g_mix_post = stack_row(mix_norm_post)
    mix_front = (g_mix_pre, w_in_b, _pool_blockdiag(pool_w), stack_row(pool_scale))
    ff1_b = tuple(bf(w[0]) for w in ff1_w)
    for l in range(depth):
        h, xbc, y_pool, z, q, k, v, dt = _ffn_inproj(
            h, g_ff1[0], *ff1_b, g_ff1[1], *mix_front, l, seq)
        y_ssd, *ff2_b = _ssd(xbc, z, dt, conv_w[l], row(conv_b[l]),
                             _per_head_cols(dt_bias[l]), _per_head_cols(a_log[l]),
                             _per_head_lanes(d_skip[l]), row(ssd_norm[l]),
                             ff2_w, l, batch, seq)
        nxt = l + 1 if l + 1 < depth else None
        y_attn, *ff1_b = _moba(q, k, v, rope_tables, ff1_w, nxt, batch, seq)
        h = _outproj_ffn(h, y_ssd, y_pool, y_attn, w_out_b, g_mix_post,
                         g_ff2[0], *ff2_b, g_ff2[1], l)
    return h.reshape(batch, seq, d)
```

```python
import functools
import math

import jax
import jax.numpy as jnp
from jax import lax
from jax.experimental import pallas as pl
from jax.experimental.pallas import tpu as pltpu

F32 = jnp.float32
BF16 = jnp.bfloat16
HIGHEST = lax.Precision.HIGHEST

D_MODEL = 1024
D_FF = 2816
SSD_INNER = 512
SSD_HEAD_DIM = 64
SSD_HEADS = SSD_INNER // SSD_HEAD_DIM
SSD_GROUPS = 2
SSD_STATE = 128
SSD_CONV = 4
SSD_CHUNK = 128
SSD_BC = SSD_GROUPS * SSD_STATE
SSD_XBC = SSD_INNER + 2 * SSD_BC
POOL_WIDTH = 256
POOL_GROUPS = 4
POOL_GROUP_DIM = POOL_WIDTH // POOL_GROUPS
ATTN_WIDTH = 256
ATTN_HEAD_DIM = 64
ATTN_HEADS = ATTN_WIDTH // ATTN_HEAD_DIM
ROPE_DIM = ATTN_HEAD_DIM // 4
ROPE_THETA = 500000.0
MOBA_BLOCK = 256
MOBA_TOPK = 3
RMS_EPS = 1e-6

OFF_Z = 0
OFF_XBC = OFF_Z + SSD_INNER
OFF_DT = OFF_XBC + SSD_XBC
OFF_POOL = OFF_DT + SSD_HEADS
OFF_Q = OFF_POOL + POOL_WIDTH
OFF_K = OFF_Q + ATTN_WIDTH
OFF_V = OFF_K + ATTN_WIDTH
IN_COLS = OFF_V + ATTN_WIDTH

LANES = 128
SUBLANES = 8
MXU_DIM = 256
VMEM_LIMIT_BYTES = 56 * 1024 * 1024

DT_PAD = LANES
TOKEN_TILE = 512
OUT_TOKEN_TILE = 1024
SUB_TILE = 256
FF_CHUNK = MXU_DIM
SSD_STEP_ROWS = 8 * SSD_CHUNK
MOBA_UNROLLS = (4, 2, 1)
MOBA_Q_PER_STEP = 4
BF16_SUBLANES = 16
LOG2E = math.log2(math.e)


def _rms(x):
    return x * lax.rsqrt(jnp.mean(x * x, axis=-1, keepdims=True) + RMS_EPS)


def _silu(x):
    return x * jax.nn.sigmoid(x)


def _softplus(x):
    return jnp.maximum(x, 0.0) + jnp.log1p(jnp.exp(-jnp.abs(x)))


def _const_spec(shape):
    zeros = (0,) * len(shape)
    return pl.BlockSpec(shape, lambda *_: zeros, pipeline_mode=pl.Buffered(1))


def _layer_spec(shape, layer):
    zeros = (0,) * len(shape)
    return pl.BlockSpec((None,) + shape, lambda *_: (layer,) + zeros,
                        pipeline_mode=pl.Buffered(1))


def _params(*sem):
    return pltpu.CompilerParams(dimension_semantics=sem,
                                vmem_limit_bytes=VMEM_LIMIT_BYTES)


def _ffn_gate_up(x, gpre_ref, wg_ref, wu_ref, h_ref):
    xb = (_rms(x) * gpre_ref[...]).astype(BF16)
    for c in range(0, D_FF, FF_CHUNK):
        sl = slice(c, min(c + FF_CHUNK, D_FF))
        g = jnp.dot(xb, wg_ref[:, sl], preferred_element_type=F32)
        u = jnp.dot(xb, wu_ref[:, sl], preferred_element_type=F32)
        h_ref[:, sl] = (_silu(g) * u).astype(BF16)


def _ffn_down(x, wd_ref, gpost_ref, h_ref):
    f = jnp.dot(h_ref[...], wd_ref[...], preferred_element_type=F32)
    return x + 0.5 * (_rms(f) * gpost_ref[...])


def _sub_tiles(tile_rows):
    return [slice(r0, r0 + SUB_TILE) for r0 in range(0, tile_rows, SUB_TILE)]


_PROJ_WIDTHS = (SSD_XBC, POOL_WIDTH, SSD_INNER, 3 * ATTN_WIDTH, DT_PAD)
_PROJ_OUT_WIDTHS = (SSD_XBC, POOL_WIDTH, SSD_INNER, ATTN_WIDTH, ATTN_WIDTH,
                    ATTN_WIDTH, DT_PAD)
POOL_HALO = 2 ** POOL_GROUPS


def _pool_tile(u, tail, pos, w_ref, scale_ref):
    group = lax.broadcasted_iota(jnp.int32, u.shape, 1) // POOL_GROUP_DIM
    win_sum = jnp.concatenate([tail, u], axis=0)
    mean = jnp.zeros_like(u)
    for g in range(POOL_GROUPS):
        half = 2 ** g
        win_sum = win_sum + pltpu.roll(win_sum, half, 0)
        count = jnp.minimum(pos + 1, 2 * half).astype(F32)
        mean = jnp.where(group == g, win_sum[POOL_HALO:, :] / count, mean)
    d = (mean - u).astype(BF16)
    return jnp.dot(d, w_ref[...], preferred_element_type=F32) * scale_ref[...]


def _ffn_inproj_body(tiles_per_seq, x_ref, gpre_ref, wg_ref, wu_ref, wd_ref,
                     gpost_ref, g_ref, *rest):
    n_proj = len(_PROJ_WIDTHS)
    w_refs = rest[:n_proj]
    (poolw_ref, pscale_ref, o_ref, xbc_ref, yp_ref, z_ref, q_ref, k_ref, v_ref,
     dt_ref, h_ref, utail_ref) = rest[n_proj:]
    tile_in_seq = pl.program_id(0) % tiles_per_seq

    @pl.when(tile_in_seq == 0)
    def _():
        utail_ref[...] = jnp.zeros_like(utail_ref)

    tiles = _sub_tiles(TOKEN_TILE)

    def down(rows):
        x = _ffn_down(x_ref[rows, :], wd_ref, gpost_ref, h_ref.at[rows])
        o_ref[rows, :] = x
        return (_rms(x) * g_ref[...]).astype(BF16)

    def project(rows, xb, utail):
        def proj(i):
            return jnp.dot(xb, w_refs[i][...], preferred_element_type=F32)

        u = proj(1)
        xbc_ref[rows, :] = proj(0)
        z_ref[rows, :] = proj(2)
        qkv = proj(3)
        for i, ref in enumerate((q_ref, k_ref, v_ref)):
            ref[rows, :] = qkv[:, i * ATTN_WIDTH:(i + 1) * ATTN_WIDTH]
        dt_ref[rows, :] = proj(4)
        pos = (tile_in_seq * TOKEN_TILE + rows.start
               + lax.broadcasted_iota(jnp.int32, u.shape, 0))
        yp_ref[rows, :] = _pool_tile(u, utail, pos, poolw_ref, pscale_ref)
        return u[SUB_TILE - POOL_HALO:, :]

    utail = utail_ref[...]
    pending = None
    for rows in tiles:
        _ffn_gate_up(x_ref[rows, :], gpre_ref, wg_ref, wu_ref, h_ref.at[rows])
        xb = down(rows)
        if pending is not None:
            utail = project(*pending, utail)
        pending = (rows, xb)
    utail_ref[...] = project(*pending, utail)


def _ffn_inproj(x, gpre, wg, wu, wd, gpost, g, ws, poolw, pscale, layer, seq):
    m = x.shape[0]
    row = lambda i: (i, 0)
    widths = (D_MODEL,) + _PROJ_OUT_WIDTHS
    return pl.pallas_call(
        functools.partial(_ffn_inproj_body, seq // TOKEN_TILE),
        grid=(m // TOKEN_TILE,),
        in_specs=[
            pl.BlockSpec((TOKEN_TILE, D_MODEL), row),
            _layer_spec((1, D_MODEL), layer),
            _const_spec((D_MODEL, D_FF)),
            _const_spec((D_MODEL, D_FF)),
            _const_spec((D_FF, D_MODEL)),
            _layer_spec((1, D_MODEL), layer),
            _layer_spec((1, D_MODEL), layer),
            *[_layer_spec((D_MODEL, width), layer) for width in _PROJ_WIDTHS],
            _layer_spec((POOL_WIDTH, POOL_WIDTH), layer),
            _layer_spec((1, POOL_WIDTH), layer),
        ],
        out_specs=[pl.BlockSpec((TOKEN_TILE, width), row) for width in widths],
        out_shape=[jax.ShapeDtypeStruct((m, width), F32) for width in widths],
        scratch_shapes=[
            pltpu.VMEM((TOKEN_TILE, D_FF), BF16),
            pltpu.VMEM((POOL_HALO, POOL_WIDTH), F32),
        ],
        compiler_params=_params("arbitrary"),
        name="ffn_inproj",
    )(x, gpre, wg, wu, wd, gpost, g, *ws, poolw, pscale)


CAST_CHUNKS = 16
_FFN_W_SHAPES = ((D_MODEL, D_FF), (D_MODEL, D_FF), (D_FF, D_MODEL))


def _with_weight_cast(body, n_in, n_out, n_w):
    def wrapped(*refs):
        ins, refs = refs[:n_in], refs[n_in:]
        w_ins, refs = refs[:n_w], refs[n_w:]
        outs, refs = refs[:n_out], refs[n_out:]
        w_outs, scratch = refs[:n_w], refs[n_w:]
        body(*ins, *outs, *scratch)
        for w_in, w_out in zip(w_ins, w_outs):
            w_out[...] = w_in[...].astype(BF16)

    return wrapped


def _weight_cast_specs(layer, n_steps, step_of_grid):
    if layer is None:
        return [], [], []
    n_chunks = math.gcd(n_steps, CAST_CHUNKS)
    steps_per_chunk = n_steps // n_chunks
    chunk_of_step = lambda *g: step_of_grid(*g) // steps_per_chunk
    in_specs, out_specs, out_shapes = [], [], []
    for rows, cols in _FFN_W_SHAPES:
        chunk = rows // n_chunks
        in_specs.append(pl.BlockSpec(
            (None, chunk, cols), lambda *g: (layer, chunk_of_step(*g), 0)))
        out_specs.append(pl.BlockSpec(
            (chunk, cols), lambda *g: (chunk_of_step(*g), 0)))
        out_shapes.append(jax.ShapeDtypeStruct((rows, cols), BF16))
    return in_specs, out_specs, out_shapes


def _causal_conv4(x, tail, w_ref, b_ref):
    x_ext = jnp.concatenate([tail, x], axis=0)
    prev_ext = pltpu.roll(x_ext, 1, 0)
    near = w_ref[3:4, :] * x + w_ref[2:3, :] * prev_ext[SUBLANES:, :]
    far_ext = w_ref[1:2, :] * x_ext + w_ref[0:1, :] * prev_ext
    return near + pltpu.roll(far_ext, 2, 0)[SUBLANES:, :] + b_ref[...]


def _heads_to_lanes(xc):
    rows = xc.shape[0]
    lane = lax.broadcasted_iota(jnp.int32, (rows, LANES), 1)
    per_tile = LANES // SSD_HEAD_DIM
    parts = []
    for t in range(SSD_INNER // LANES):
        tile = jnp.broadcast_to(xc[:, t * per_tile:t * per_tile + 1], (rows, LANES))
        for i in range(1, per_tile):
            h = t * per_tile + i
            tile = jnp.where(lane < i * SSD_HEAD_DIM, tile,
                             jnp.broadcast_to(xc[:, h:h + 1], (rows, LANES)))
        parts.append(tile)
    return jnp.concatenate(parts, axis=1)


def _ssd_body(xbc_ref, z_ref, dt_ref, convw_ref, convb_ref, dtb_ref, alog_ref,
              dskip_ref, gn_ref, o_ref, tail_ref, state_ref):
    L = SSD_CHUNK
    HP = SSD_INNER
    P = SSD_HEAD_DIM
    N = SSD_STATE
    GW = HP // SSD_GROUPS
    HPG = SSD_HEADS // SSD_GROUPS

    @pl.when(pl.program_id(1) == 0)
    def _():
        tail_ref[...] = jnp.zeros_like(tail_ref)
        state_ref[...] = jnp.zeros_like(state_ref)

    r = lax.broadcasted_iota(jnp.int32, (L, L), 0)
    s = lax.broadcasted_iota(jnp.int32, (L, L), 1)
    causal = s <= r
    tril = causal.astype(F32)
    pair_lane = lax.broadcasted_iota(jnp.int32, (L, 2 * P), 1)
    neg_a = -jnp.exp(alog_ref[...])

    def chunk(c, tail):
        r0 = pl.multiple_of(c * L, L)
        rows = pl.ds(r0, L)
        x = xbc_ref[rows, :]
        xc = _silu(_causal_conv4(x, tail, convw_ref, convb_ref))
        xs = xc[:, :HP]
        bm = xc[:, HP:HP + SSD_BC].astype(BF16)
        cm = xc[:, HP + SSD_BC:].astype(BF16)

        dt_c = _softplus(dt_ref[rows, :] + dtb_ref[...])
        acs_c = jnp.dot(tril, dt_c * neg_a, precision=HIGHEST,
                        preferred_element_type=F32)
        acs_t = acs_c.T
        dt = _heads_to_lanes(dt_c)
        acs = _heads_to_lanes(acs_c)
        a_last = acs[L - 1:L, :]
        decay_out = jnp.exp(acs)
        decay_in = jnp.exp(a_last - acs)
        chunk_decay = jnp.exp(a_last)

        xdt = xs * dt
        xdt_b = xdt.astype(BF16)
        xw_b = (xdt * decay_in).astype(BF16)

        y_parts = []
        for g in range(SSD_GROUPS):
            bg = bm[:, g * N:(g + 1) * N]
            cg = cm[:, g * N:(g + 1) * N]
            gsl = slice(g * GW, (g + 1) * GW)
            cb = lax.dot_general(cg, bg, (((1,), (1,)), ((), ())),
                                 preferred_element_type=F32)
            st = state_ref[:, gsl]
            y_off = jnp.dot(cg, st.astype(BF16), preferred_element_type=F32)
            new_st = lax.dot_general(bg, xw_b[:, gsl], (((0,), (0,)), ((), ())),
                                     preferred_element_type=F32)
            state_ref[:, gsl] = st * chunk_decay[:, gsl] + new_st
            for pair in range(HPG // 2):
                lo = g * GW + pair * 2 * P
                x_pair = xdt_b[:, lo:lo + 2 * P]
                ys = []
                for i in range(2):
                    h = lo // P + i
                    seg = jnp.exp(jnp.where(
                        causal, acs_c[:, h:h + 1] - acs_t[h:h + 1, :], -jnp.inf))
                    ys.append(jnp.dot((cb * seg).astype(BF16), x_pair,
                                      preferred_element_type=F32))
                y_diag = jnp.where(pair_lane < P, ys[0], ys[1])
                psl = slice(pair * 2 * P, (pair + 1) * 2 * P)
                y_parts.append(
                    y_diag + y_off[:, psl] * decay_out[:, lo:lo + 2 * P])
        y = jnp.concatenate(y_parts, axis=-1) + dskip_ref[...] * xs
        o_ref[rows, :] = _rms(y * _silu(z_ref[rows, :])) * gn_ref[...]
        return x[L - SUBLANES:, :]

    n_chunks = xbc_ref.shape[0] // L
    tail_ref[...] = lax.fori_loop(0, n_chunks, chunk, tail_ref[...], unroll=True)


def _ssd(xbc, z, dt, convw, convb, dtb, alog, dskip, gn, cast_ws, cast_layer,
         batch, seq):
    nblk = seq // SSD_STEP_ROWS
    row = lambda b, c: (b * nblk + c, 0)
    w_in_specs, w_out_specs, w_out_shapes = _weight_cast_specs(
        cast_layer, batch * nblk, lambda b, c: b * nblk + c)
    return pl.pallas_call(
        _with_weight_cast(_ssd_body, 9, 1, len(w_in_specs)),
        grid=(batch, nblk),
        in_specs=[
            pl.BlockSpec((SSD_STEP_ROWS, SSD_XBC), row),
            pl.BlockSpec((SSD_STEP_ROWS, SSD_INNER), row),
            pl.BlockSpec((SSD_STEP_ROWS, DT_PAD), row),
            _const_spec((SSD_CONV, SSD_XBC)),
            _const_spec((1, SSD_XBC)),
            _const_spec((1, DT_PAD)),
            _const_spec((1, DT_PAD)),
            _const_spec((1, SSD_INNER)),
            _const_spec((1, SSD_INNER)),
            *w_in_specs,
        ],
        out_specs=[pl.BlockSpec((SSD_STEP_ROWS, SSD_INNER), row), *w_out_specs],
        out_shape=[jax.ShapeDtypeStruct((batch * seq, SSD_INNER), F32),
                   *w_out_shapes],
        scratch_shapes=[
            pltpu.VMEM((SUBLANES, SSD_XBC), F32),
            pltpu.VMEM((SSD_STATE, SSD_INNER), F32),
        ],
        compiler_params=_params("arbitrary", "arbitrary"),
        name="ssd",
    )(xbc, z, dt, convw, convb, dtb, alog, dskip, gn, *cast_ws)


def _rope(t, cos, sin_up, sin_dn):
    half = ROPE_DIM // 2
    return (t * cos + pltpu.roll(t, half, 1) * sin_up
            + pltpu.roll(t, ATTN_WIDTH - half, 1) * sin_dn)


def _moba_body(q_ref, k_ref, v_ref, cosq_ref, supq_ref, sdnq_ref,
               cosk_ref, supk_ref, sdnk_ref, o_ref,
               ks_ref, vt_ref, kmean_ref, sel_ref, s_ref):
    seq = k_ref.shape[0]
    nb = seq // MOBA_BLOCK
    Dh = ATTN_HEAD_DIM
    BL = MOBA_BLOCK
    H = ATTN_HEADS
    step = pl.program_id(1)
    nt = (((1,), (1,)), ((), ()))

    head_of_lane = lax.broadcasted_iota(jnp.int32, (1, ATTN_WIDTH), 1) // Dh

    @pl.when(step == 0)
    def _():
        for j in range(nb):
            rows = slice(j * BL, (j + 1) * BL)
            kj = _rope(k_ref[rows, :], cosk_ref[rows, :], supk_ref[rows, :],
                       sdnk_ref[rows, :])
            kmean = jnp.mean(kj, axis=0, keepdims=True)
            for h in range(H):
                kmean_ref[h * nb + j:h * nb + j + 1, :] = jnp.where(
                    head_of_lane == h, kmean, 0.0)
            kb = kj.astype(BF16)
            for h in range(H):
                ks_ref[h, rows, :] = kb[:, h * Dh:(h + 1) * Dh]
            vt = v_ref[rows, :].T.astype(BF16)
            for h in range(H):
                vt_ref[h, 0:Dh, rows] = vt[h * Dh:(h + 1) * Dh, :]
        vt_ref[:, Dh:, :] = jnp.ones((H, BF16_SUBLANES, seq), BF16)

    def over_blocks(fn, n, carry):
        start = 0
        for width in MOBA_UNROLLS:
            def body(jj, carry, start=start, width=width):
                for t in range(width):
                    carry = fn(start + width * jj + t, carry)
                return carry
            groups = lax.div(n - start, width)
            carry = lax.fori_loop(0, groups, body, carry)
            start = start + groups * width
        return carry

    def attend(i, rows):
        qf = _rope(q_ref[rows, :], cosq_ref[rows, :], supq_ref[rows, :],
                   sdnq_ref[rows, :])
        qb = (qf * (Dh ** -0.5)).astype(BF16)
        blk = lax.broadcasted_iota(jnp.int32, (nb, BL), 0)
        past = blk < i
        gates = lax.dot_general(kmean_ref[...], qf, nt, precision=HIGHEST,
                                preferred_element_type=F32)
        for h in range(H):
            gate = jnp.where(past, gates[h * nb:(h + 1) * nb, :], -jnp.inf)
            rank = jnp.zeros((nb, BL), jnp.int32)
            for j2 in range(nb):
                g2 = gate[j2:j2 + 1, :]
                ahead = (g2 > gate) | ((g2 == gate) & (j2 < blk))
                rank = rank + ahead.astype(jnp.int32)
            sel_ref[h] = (past & (rank < MOBA_TOPK)).astype(F32)

        qh = [qb[:, h * Dh:(h + 1) * Dh] for h in range(H)]

        def masked_scores(h, j, keep):
            k0 = pl.multiple_of(j * BL, BL)
            st = lax.dot_general(ks_ref[h, pl.ds(k0, BL), :], qh[h], nt,
                                 preferred_element_type=F32)
            st = jnp.where(keep, st * LOG2E, -jnp.inf)
            s_ref[h, j] = st
            return jnp.max(st, axis=0, keepdims=True)

        own_mask = (lax.broadcasted_iota(jnp.int32, (BL, BL), 0) <=
                    lax.broadcasted_iota(jnp.int32, (BL, BL), 1))
        m_own = tuple(masked_scores(h, i, own_mask) for h in range(H))

        def pass1(j, ms):
            return tuple(
                jnp.maximum(ms[h], masked_scores(
                    h, j, sel_ref[h, pl.ds(j, 1), :] > 0.5)) for h in range(H))

        ms = over_blocks(pass1, i, m_own)

        def pass2(j, accs):
            k0 = pl.multiple_of(j * BL, BL)
            return tuple(
                accs[h] + jnp.dot(
                    vt_ref[h, :, pl.ds(k0, BL)],
                    jnp.exp2(s_ref[h, j] - ms[h]).astype(BF16),
                    preferred_element_type=F32)
                for h in range(H))

        init = (jnp.zeros((Dh + BF16_SUBLANES, BL), F32),) * H
        accs = over_blocks(pass2, i + 1, init)
        out_t = jnp.concatenate(
            [acc[:Dh] / acc[Dh:Dh + 1] for acc in accs], axis=0)
        o_ref[rows, :] = out_t.T

    for sub in range(MOBA_Q_PER_STEP):
        attend(step * MOBA_Q_PER_STEP + sub, slice(sub * BL, (sub + 1) * BL))


def _moba(q, k, v, tables, cast_ws, cast_layer, batch, seq):
    nb = seq // MOBA_BLOCK
    full = lambda b, i: (b, 0)
    steps = nb // MOBA_Q_PER_STEP
    qrow = lambda b, i: (b * steps + i, 0)
    trow = lambda b, i: (i, 0)
    const = lambda b, i: (0, 0)
    qspec = pl.BlockSpec((MOBA_Q_PER_STEP * MOBA_BLOCK, ATTN_WIDTH), qrow)
    tq = pl.BlockSpec((MOBA_Q_PER_STEP * MOBA_BLOCK, ATTN_WIDTH), trow)
    tk = pl.BlockSpec((seq, ATTN_WIDTH), const)
    kv = pl.BlockSpec((seq, ATTN_WIDTH), full)
    w_in_specs, w_out_specs, w_out_shapes = _weight_cast_specs(
        cast_layer, batch * steps, lambda b, i: b * steps + i)
    return pl.pallas_call(
        _with_weight_cast(_moba_body, 9, 1, len(w_in_specs)),
        grid=(batch, steps),
        in_specs=[qspec, kv, kv, tq, tq, tq, tk, tk, tk, *w_in_specs],
        out_specs=[qspec, *w_out_specs],
        out_shape=[jax.ShapeDtypeStruct((batch * seq, ATTN_WIDTH), F32),
                   *w_out_shapes],
        scratch_shapes=[
            pltpu.VMEM((ATTN_HEADS, seq, ATTN_HEAD_DIM), BF16),
            pltpu.VMEM((ATTN_HEADS, ATTN_HEAD_DIM + BF16_SUBLANES, seq), BF16),
            pltpu.VMEM((ATTN_HEADS * nb, ATTN_WIDTH), F32),
            pltpu.VMEM((ATTN_HEADS, nb, MOBA_BLOCK), F32),
            pltpu.VMEM((ATTN_HEADS, nb, MOBA_BLOCK, MOBA_BLOCK), F32),
        ],
        compiler_params=_params("arbitrary", "arbitrary"),
        name="moba",
    )(q, k, v, *tables, *tables, *(cast_ws if w_in_specs else ()))


def _outproj_ffn_body(x_ref, ys_ref, yp_ref, ya_ref, w_ref, g_ref,
                      gpre_ref, wg_ref, wu_ref, wd_ref, gpost_ref, o_ref, h_ref):
    o1 = SSD_INNER
    o2 = SSD_INNER + POOL_WIDTH
    tiles = _sub_tiles(OUT_TOKEN_TILE)
    xs = []
    for rows in tiles:
        m = jnp.dot(ys_ref[rows, :].astype(BF16), w_ref[0:o1, :],
                    preferred_element_type=F32)
        m = m + jnp.dot(yp_ref[rows, :].astype(BF16), w_ref[o1:o2, :],
                        preferred_element_type=F32)
        m = m + jnp.dot(ya_ref[rows, :].astype(BF16), w_ref[o2:, :],
                        preferred_element_type=F32)
        xs.append(x_ref[rows, :] + _rms(m) * g_ref[...])
    for rows, x in zip(tiles, xs):
        _ffn_gate_up(x, gpre_ref, wg_ref, wu_ref, h_ref.at[rows])
        o_ref[rows, :] = _ffn_down(x, wd_ref, gpost_ref, h_ref.at[rows])


def _outproj_ffn(x, y_ssd, y_pool, y_attn, w, g, gpre, wg, wu, wd, gpost, layer):
    m = x.shape[0]
    row = lambda i: (i, 0)
    return pl.pallas_call(
        _outproj_ffn_body,
        grid=(m // OUT_TOKEN_TILE,),
        in_specs=[
            pl.BlockSpec((OUT_TOKEN_TILE, D_MODEL), row),
            pl.BlockSpec((OUT_TOKEN_TILE, SSD_INNER), row),
            pl.BlockSpec((OUT_TOKEN_TILE, POOL_WIDTH), row),
            pl.BlockSpec((OUT_TOKEN_TILE, ATTN_WIDTH), row),
            _layer_spec((D_MODEL, D_MODEL), layer),
            _layer_spec((1, D_MODEL), layer),
            _layer_spec((1, D_MODEL), layer),
            _const_spec((D_MODEL, D_FF)),
            _const_spec((D_MODEL, D_FF)),
            _const_spec((D_FF, D_MODEL)),
            _layer_spec((1, D_MODEL), layer),
        ],
        out_specs=pl.BlockSpec((OUT_TOKEN_TILE, D_MODEL), row),
        out_shape=jax.ShapeDtypeStruct((m, D_MODEL), F32),
        scratch_shapes=[pltpu.VMEM((OUT_TOKEN_TILE, D_FF), BF16)],
        compiler_params=_params("parallel"),
        name="outproj_ffn",
    )(x, y_ssd, y_pool, y_attn, w, g, gpre, wg, wu, wd, gpost)


def _rope_tables(seq):
    half = ROPE_DIM // 2
    inv_freq = ROPE_THETA ** (-jnp.arange(0, ROPE_DIM, 2, dtype=F32) / ROPE_DIM)
    ang = jnp.arange(seq, dtype=F32)[:, None] * inv_freq[None, :]
    rest = ATTN_HEAD_DIM - ROPE_DIM
    one = jnp.ones((seq, rest), F32)
    zero = jnp.zeros((seq, rest), F32)
    zh = jnp.zeros((seq, half), F32)
    cos = jnp.concatenate([jnp.cos(ang), jnp.cos(ang), one], axis=-1)
    sin_up = jnp.concatenate([zh, jnp.sin(ang), zero], axis=-1)
    sin_dn = jnp.concatenate([-jnp.sin(ang), zh, zero], axis=-1)
    return tuple(jnp.tile(t, (1, ATTN_HEADS)) for t in (cos, sin_up, sin_dn))


def _split_w_in(w):
    pad = [(0, 0)] * (w.ndim - 1) + [(0, DT_PAD - SSD_HEADS)]
    pieces = [w[..., OFF_XBC:OFF_DT], w[..., OFF_POOL:OFF_Q], w[..., OFF_Z:OFF_XBC],
              w[..., OFF_Q:IN_COLS], jnp.pad(w[..., OFF_DT:OFF_POOL], pad)]
    return tuple(p.astype(BF16) for p in pieces)


def _per_head_lanes(p):
    return jnp.repeat(p, SSD_HEAD_DIM)[None, :]


def _per_head_cols(p):
    pad = [(0, 0)] * (p.ndim - 1) + [(0, DT_PAD - SSD_HEADS)]
    return jnp.pad(p, pad)[..., None, :]


def _pool_blockdiag(w):
    out = jnp.zeros(w.shape[:-3] + (POOL_WIDTH, POOL_WIDTH), F32)
    for g in range(POOL_GROUPS):
        sl = slice(g * POOL_GROUP_DIM, (g + 1) * POOL_GROUP_DIM)
        out = out.at[..., sl, sl].set(w[..., g, :, :])
    return out.astype(BF16)


def kernel(x, ff1_norm_pre, ff1_w_gate, ff1_w_up, ff1_w_down, ff1_norm_post,
           mix_norm_pre, w_in, conv_w, conv_b, dt_bias, a_log, d_skip, ssd_norm,
           pool_w, pool_scale, w_out, mix_norm_post,
           ff2_norm_pre, ff2_w_gate, ff2_w_up, ff2_w_down, ff2_norm_post):
    batch, seq, d = x.shape
    depth = w_in.shape[0]
    h = x.reshape(batch * seq, d)
    rope_tables = _rope_tables(seq)
    row = lambda p: p[None, :]
    stack_row = lambda p: p[:, None, :]
    bf = lambda w: w.astype(BF16)
    ff1_w = (ff1_w_gate, ff1_w_up, ff1_w_down)
    ff2_w = (ff2_w_gate, ff2_w_up, ff2_w_down)
    g_ff1 = (stack_row(ff1_norm_pre), stack_row(ff1_norm_post))
    g_ff2 = (stack_row(ff2_norm_pre), stack_row(ff2_norm_post))
    w_in_b = _split_w_in(w_in)
    w_out_b = bf(w_out)
    g_mix_pre = stack_row(mix_norm_pre)
    g_mix_post = stack_row(mix_norm_post)
    mix_front = (g_mix_pre, w_in_b, _pool_blockdiag(pool_w), stack_row(pool_scale))
    ff1_b = tuple(bf(w[0]) for w in ff1_w)
    for l in range(depth):
        h, xbc, y_pool, z, q, k, v, dt = _ffn_inproj(
            h, g_ff1[0], *ff1_b, g_ff1[1], *mix_front, l, seq)
        y_ssd, *ff2_b = _ssd(xbc, z, dt, conv_w[l], row(conv_b[l]),
                             _per_head_cols(dt_bias[l]), _per_head_cols(a_log[l]),
                             _per_head_lanes(d_skip[l]), row(ssd_norm[l]),
                             ff2_w, l, batch, seq)
        nxt = l + 1 if l + 1 < depth else None
        y_attn, *ff1_b = _moba(q, k, v, rope_tables, ff1_w, nxt, batch, seq)
        h = _outproj_ffn(h, y_ssd, y_pool, y_attn, w_out_b, g_mix_post,
                         g_ff2[0], *ff2_b, g_ff2[1], l)
    return h.reshape(batch, seq, d)
```

```python
import functools
import math

import jax
import jax.numpy as jnp
from jax import lax
from jax.experimental import pallas as pl
from jax.experimental.pallas import tpu as pltpu

F32 = jnp.float32
BF16 = jnp.bfloat16
HIGHEST = lax.Precision.HIGHEST

D_MODEL = 1024
D_FF = 2816
SSD_INNER = 512
SSD_HEAD_DIM = 64
SSD_HEADS = SSD_INNER // SSD_HEAD_DIM
SSD_GROUPS = 2
SSD_STATE = 128
SSD_CONV = 4
SSD_CHUNK = 128
SSD_BC = SSD_GROUPS * SSD_STATE
SSD_XBC = SSD_INNER + 2 * SSD_BC
POOL_WIDTH = 256
POOL_GROUPS = 4
POOL_GROUP_DIM = POOL_WIDTH // POOL_GROUPS
ATTN_WIDTH = 256
ATTN_HEAD_DIM = 64
ATTN_HEADS = ATTN_WIDTH // ATTN_HEAD_DIM
ROPE_DIM = ATTN_HEAD_DIM // 4
ROPE_THETA = 500000.0
MOBA_BLOCK = 256
MOBA_TOPK = 3
RMS_EPS = 1e-6

OFF_Z = 0
OFF_XBC = OFF_Z + SSD_INNER
OFF_DT = OFF_XBC + SSD_XBC
OFF_POOL = OFF_DT + SSD_HEADS
OFF_Q = OFF_POOL + POOL_WIDTH
OFF_K = OFF_Q + ATTN_WIDTH
OFF_V = OFF_K + ATTN_WIDTH
IN_COLS = OFF_V + ATTN_WIDTH

LANES = 128
SUBLANES = 8
MXU_DIM = 256
VMEM_LIMIT_BYTES = 56 * 1024 * 1024

DT_PAD = LANES
TOKEN_TILE = 512
OUT_TOKEN_TILE = 1024
SUB_TILE = 256
FF_CHUNK = MXU_DIM
SSD_STEP_ROWS = 8 * SSD_CHUNK
MOBA_UNROLLS = (4, 2, 1)
MOBA_Q_PER_STEP = 8
BF16_SUBLANES = 16
LOG2E = math.log2(math.e)


def _rms(x):
    return x * lax.rsqrt(jnp.mean(x * x, axis=-1, keepdims=True) + RMS_EPS)


def _silu(x):
    return x * jax.nn.sigmoid(x)


def _softplus(x):
    return jnp.maximum(x, 0.0) + jnp.log1p(jnp.exp(-jnp.abs(x)))


def _const_spec(shape):
    zeros = (0,) * len(shape)
    return pl.BlockSpec(shape, lambda *_: zeros, pipeline_mode=pl.Buffered(1))


def _layer_spec(shape, layer):
    zeros = (0,) * len(shape)
    return pl.BlockSpec((None,) + shape, lambda *_: (layer,) + zeros,
                        pipeline_mode=pl.Buffered(1))


def _params(*sem):
    return pltpu.CompilerParams(dimension_semantics=sem,
                                vmem_limit_bytes=VMEM_LIMIT_BYTES)


def _ffn_gate_up(x, gpre_ref, wg_ref, wu_ref, h_ref):
    xb = (_rms(x) * gpre_ref[...]).astype(BF16)
    for c in range(0, D_FF, FF_CHUNK):
        sl = slice(c, min(c + FF_CHUNK, D_FF))
        g = jnp.dot(xb, wg_ref[:, sl], preferred_element_type=F32)
        u = jnp.dot(xb, wu_ref[:, sl], preferred_element_type=F32)
        h_ref[:, sl] = (_silu(g) * u).astype(BF16)


def _ffn_down(x, wd_ref, gpost_ref, h_ref):
    f = jnp.dot(h_ref[...], wd_ref[...], preferred_element_type=F32)
    return x + 0.5 * (_rms(f) * gpost_ref[...])


def _sub_tiles(tile_rows):
    return [slice(r0, r0 + SUB_TILE) for r0 in range(0, tile_rows, SUB_TILE)]


_PROJ_WIDTHS = (SSD_XBC, POOL_WIDTH, SSD_INNER, 3 * ATTN_WIDTH, DT_PAD)
_PROJ_OUT_WIDTHS = (SSD_XBC, POOL_WIDTH, SSD_INNER, ATTN_WIDTH, ATTN_WIDTH,
                    ATTN_WIDTH, DT_PAD)
POOL_HALO = 2 ** POOL_GROUPS


def _pool_tile(u, tail, pos, w_ref, scale_ref):
    group = lax.broadcasted_iota(jnp.int32, u.shape, 1) // POOL_GROUP_DIM
    win_sum = jnp.concatenate([tail, u], axis=0)
    mean = jnp.zeros_like(u)
    for g in range(POOL_GROUPS):
        half = 2 ** g
        win_sum = win_sum + pltpu.roll(win_sum, half, 0)
        count = jnp.minimum(pos + 1, 2 * half).astype(F32)
        mean = jnp.where(group == g, win_sum[POOL_HALO:, :] / count, mean)
    d = (mean - u).astype(BF16)
    return jnp.dot(d, w_ref[...], preferred_element_type=F32) * scale_ref[...]


def _ffn_inproj_body(tiles_per_seq, x_ref, gpre_ref, wg_ref, wu_ref, wd_ref,
                     gpost_ref, g_ref, *rest):
    n_proj = len(_PROJ_WIDTHS)
    w_refs = rest[:n_proj]
    (poolw_ref, pscale_ref, o_ref, xbc_ref, yp_ref, z_ref, q_ref, k_ref, v_ref,
     dt_ref, h_ref, utail_ref) = rest[n_proj:]
    tile_in_seq = pl.program_id(0) % tiles_per_seq

    @pl.when(tile_in_seq == 0)
    def _():
        utail_ref[...] = jnp.zeros_like(utail_ref)

    tiles = _sub_tiles(TOKEN_TILE)

    def down(rows):
        x = _ffn_down(x_ref[rows, :], wd_ref, gpost_ref, h_ref.at[rows])
        o_ref[rows, :] = x
        return (_rms(x) * g_ref[...]).astype(BF16)

    def project(rows, xb, utail):
        def proj(i):
            return jnp.dot(xb, w_refs[i][...], preferred_element_type=F32)

        u = proj(1)
        xbc_ref[rows, :] = proj(0)
        z_ref[rows, :] = proj(2)
        qkv = proj(3)
        for i, ref in enumerate((q_ref, k_ref, v_ref)):
            ref[rows, :] = qkv[:, i * ATTN_WIDTH:(i + 1) * ATTN_WIDTH]
        dt_ref[rows, :] = proj(4)
        pos = (tile_in_seq * TOKEN_TILE + rows.start
               + lax.broadcasted_iota(jnp.int32, u.shape, 0))
        yp_ref[rows, :] = _pool_tile(u, utail, pos, poolw_ref, pscale_ref)
        return u[SUB_TILE - POOL_HALO:, :]

    utail = utail_ref[...]
    pending = None
    for rows in tiles:
        _ffn_gate_up(x_ref[rows, :], gpre_ref, wg_ref, wu_ref, h_ref.at[rows])
        xb = down(rows)
        if pending is not None:
            utail = project(*pending, utail)
        pending = (rows, xb)
    utail_ref[...] = project(*pending, utail)


def _ffn_inproj(x, gpre, wg, wu, wd, gpost, g, ws, poolw, pscale, layer, seq):
    m = x.shape[0]
    row = lambda i: (i, 0)
    widths = (D_MODEL,) + _PROJ_OUT_WIDTHS
    return pl.pallas_call(
        functools.partial(_ffn_inproj_body, seq // TOKEN_TILE),
        grid=(m // TOKEN_TILE,),
        in_specs=[
            pl.BlockSpec((TOKEN_TILE, D_MODEL), row),
            _layer_spec((1, D_MODEL), layer),
            _const_spec((D_MODEL, D_FF)),
            _const_spec((D_MODEL, D_FF)),
            _const_spec((D_FF, D_MODEL)),
            _layer_spec((1, D_MODEL), layer),
            _layer_spec((1, D_MODEL), layer),
            *[_layer_spec((D_MODEL, width), layer) for width in _PROJ_WIDTHS],
            _layer_spec((POOL_WIDTH, POOL_WIDTH), layer),
            _layer_spec((1, POOL_WIDTH), layer),
        ],
        out_specs=[pl.BlockSpec((TOKEN_TILE, width), row) for width in widths],
        out_shape=[jax.ShapeDtypeStruct((m, width), F32) for width in widths],
        scratch_shapes=[
            pltpu.VMEM((TOKEN_TILE, D_FF), BF16),
            pltpu.VMEM((POOL_HALO, POOL_WIDTH), F32),
        ],
        compiler_params=_params("arbitrary"),
        name="ffn_inproj",
    )(x, gpre, wg, wu, wd, gpost, g, *ws, poolw, pscale)


CAST_CHUNKS = 16
_FFN_W_SHAPES = ((D_MODEL, D_FF), (D_MODEL, D_FF), (D_FF, D_MODEL))


def _with_weight_cast(body, n_in, n_out, n_w):
    def wrapped(*refs):
        ins, refs = refs[:n_in], refs[n_in:]
        w_ins, refs = refs[:n_w], refs[n_w:]
        outs, refs = refs[:n_out], refs[n_out:]
        w_outs, scratch = refs[:n_w], refs[n_w:]
        body(*ins, *outs, *scratch)
        for w_in, w_out in zip(w_ins, w_outs):
            w_out[...] = w_in[...].astype(BF16)

    return wrapped


def _weight_cast_specs(layer, n_steps, step_of_grid):
    if layer is None:
        return [], [], []
    n_chunks = math.gcd(n_steps, CAST_CHUNKS)
    steps_per_chunk = n_steps // n_chunks
    chunk_of_step = lambda *g: step_of_grid(*g) // steps_per_chunk
    in_specs, out_specs, out_shapes = [], [], []
    for rows, cols in _FFN_W_SHAPES:
        chunk = rows // n_chunks
        in_specs.append(pl.BlockSpec(
            (None, chunk, cols), lambda *g: (layer, chunk_of_step(*g), 0)))
        out_specs.append(pl.BlockSpec(
            (chunk, cols), lambda *g: (chunk_of_step(*g), 0)))
        out_shapes.append(jax.ShapeDtypeStruct((rows, cols), BF16))
    return in_specs, out_specs, out_shapes


def _causal_conv4(x, tail, w_ref, b_ref):
    x_ext = jnp.concatenate([tail, x], axis=0)
    prev_ext = pltpu.roll(x_ext, 1, 0)
    near = w_ref[3:4, :] * x + w_ref[2:3, :] * prev_ext[SUBLANES:, :]
    far_ext = w_ref[1:2, :] * x_ext + w_ref[0:1, :] * prev_ext
    return near + pltpu.roll(far_ext, 2, 0)[SUBLANES:, :] + b_ref[...]


def _heads_to_lanes(xc):
    rows = xc.shape[0]
    lane = lax.broadcasted_iota(jnp.int32, (rows, LANES), 1)
    per_tile = LANES // SSD_HEAD_DIM
    parts = []
    for t in range(SSD_INNER // LANES):
        tile = jnp.broadcast_to(xc[:, t * per_tile:t * per_tile + 1], (rows, LANES))
        for i in range(1, per_tile):
            h = t * per_tile + i
            tile = jnp.where(lane < i * SSD_HEAD_DIM, tile,
                             jnp.broadcast_to(xc[:, h:h + 1], (rows, LANES)))
        parts.append(tile)
    return jnp.concatenate(parts, axis=1)


def _ssd_body(xbc_ref, z_ref, dt_ref, convw_ref, convb_ref, dtb_ref, alog_ref,
              dskip_ref, gn_ref, o_ref, tail_ref, state_ref):
    L = SSD_CHUNK
    HP = SSD_INNER
    P = SSD_HEAD_DIM
    N = SSD_STATE
    GW = HP // SSD_GROUPS
    HPG = SSD_HEADS // SSD_GROUPS

    @pl.when(pl.program_id(1) == 0)
    def _():
        tail_ref[...] = jnp.zeros_like(tail_ref)
        state_ref[...] = jnp.zeros_like(state_ref)

    r = lax.broadcasted_iota(jnp.int32, (L, L), 0)
    s = lax.broadcasted_iota(jnp.int32, (L, L), 1)
    causal = s <= r
    tril = causal.astype(F32)
    pair_lane = lax.broadcasted_iota(jnp.int32, (L, 2 * P), 1)
    neg_a = -jnp.exp(alog_ref[...])

    def chunk(c, tail):
        r0 = pl.multiple_of(c * L, L)
        rows = pl.ds(r0, L)
        x = xbc_ref[rows, :]
        xc = _silu(_causal_conv4(x, tail, convw_ref, convb_ref))
        xs = xc[:, :HP]
        bm = xc[:, HP:HP + SSD_BC].astype(BF16)
        cm = xc[:, HP + SSD_BC:].astype(BF16)

        dt_c = _softplus(dt_ref[rows, :] + dtb_ref[...])
        acs_c = jnp.dot(tril, dt_c * neg_a, precision=HIGHEST,
                        preferred_element_type=F32)
        acs_t = acs_c.T
        dt = _heads_to_lanes(dt_c)
        acs = _heads_to_lanes(acs_c)
        a_last = acs[L - 1:L, :]
        decay_out = jnp.exp(acs)
        decay_in = jnp.exp(a_last - acs)
        chunk_decay = jnp.exp(a_last)

        xdt = xs * dt
        xdt_b = xdt.astype(BF16)
        xw_b = (xdt * decay_in).astype(BF16)

        y_parts = []
        for g in range(SSD_GROUPS):
            bg = bm[:, g * N:(g + 1) * N]
            cg = cm[:, g * N:(g + 1) * N]
            gsl = slice(g * GW, (g + 1) * GW)
            cb = lax.dot_general(cg, bg, (((1,), (1,)), ((), ())),
                                 preferred_element_type=F32)
            st = state_ref[:, gsl]
            y_off = jnp.dot(cg, st.astype(BF16), preferred_element_type=F32)
            new_st = lax.dot_general(bg, xw_b[:, gsl], (((0,), (0,)), ((), ())),
                                     preferred_element_type=F32)
            state_ref[:, gsl] = st * chunk_decay[:, gsl] + new_st
            for pair in range(HPG // 2):
                lo = g * GW + pair * 2 * P
                x_pair = xdt_b[:, lo:lo + 2 * P]
                ys = []
                for i in range(2):
                    h = lo // P + i
                    seg = jnp.exp(jnp.where(
                        causal, acs_c[:, h:h + 1] - acs_t[h:h + 1, :], -jnp.inf))
                    ys.append(jnp.dot((cb * seg).astype(BF16), x_pair,
                                      preferred_element_type=F32))
                y_diag = jnp.where(pair_lane < P, ys[0], ys[1])
                psl = slice(pair * 2 * P, (pair + 1) * 2 * P)
                y_parts.append(
                    y_diag + y_off[:, psl] * decay_out[:, lo:lo + 2 * P])
        y = jnp.concatenate(y_parts, axis=-1) + dskip_ref[...] * xs
        o_ref[rows, :] = _rms(y * _silu(z_ref[rows, :])) * gn_ref[...]
        return x[L - SUBLANES:, :]

    n_chunks = xbc_ref.shape[0] // L
    tail_ref[...] = lax.fori_loop(0, n_chunks, chunk, tail_ref[...], unroll=True)


def _ssd(xbc, z, dt, convw, convb, dtb, alog, dskip, gn, cast_ws, cast_layer,
         batch, seq):
    nblk = seq // SSD_STEP_ROWS
    row = lambda b, c: (b * nblk + c, 0)
    w_in_specs, w_out_specs, w_out_shapes = _weight_cast_specs(
        cast_layer, batch * nblk, lambda b, c: b * nblk + c)
    return pl.pallas_call(
        _with_weight_cast(_ssd_body, 9, 1, len(w_in_specs)),
        grid=(batch, nblk),
        in_specs=[
            pl.BlockSpec((SSD_STEP_ROWS, SSD_XBC), row),
            pl.BlockSpec((SSD_STEP_ROWS, SSD_INNER), row),
            pl.BlockSpec((SSD_STEP_ROWS, DT_PAD), row),
            _const_spec((SSD_CONV, SSD_XBC)),
            _const_spec((1, SSD_XBC)),
            _const_spec((1, DT_PAD)),
            _const_spec((1, DT_PAD)),
            _const_spec((1, SSD_INNER)),
            _const_spec((1, SSD_INNER)),
            *w_in_specs,
        ],
        out_specs=[pl.BlockSpec((SSD_STEP_ROWS, SSD_INNER), row), *w_out_specs],
        out_shape=[jax.ShapeDtypeStruct((batch * seq, SSD_INNER), F32),
                   *w_out_shapes],
        scratch_shapes=[
            pltpu.VMEM((SUBLANES, SSD_XBC), F32),
            pltpu.VMEM((SSD_STATE, SSD_INNER), F32),
        ],
        compiler_params=_params("arbitrary", "arbitrary"),
        name="ssd",
    )(xbc, z, dt, convw, convb, dtb, alog, dskip, gn, *cast_ws)


def _rope(t, cos, sin_up, sin_dn):
    half = ROPE_DIM // 2
    return (t * cos + pltpu.roll(t, half, 1) * sin_up
            + pltpu.roll(t, ATTN_WIDTH - half, 1) * sin_dn)


def _moba_body(q_ref, k_ref, v_ref, cosq_ref, supq_ref, sdnq_ref,
               cosk_ref, supk_ref, sdnk_ref, o_ref,
               ks_ref, vt_ref, kmean_ref, sel_ref, s_ref):
    seq = k_ref.shape[0]
    nb = seq // MOBA_BLOCK
    Dh = ATTN_HEAD_DIM
    BL = MOBA_BLOCK
    H = ATTN_HEADS
    step = pl.program_id(1)
    nt = (((1,), (1,)), ((), ()))

    head_of_lane = lax.broadcasted_iota(jnp.int32, (1, ATTN_WIDTH), 1) // Dh

    @pl.when(step == 0)
    def _():
        for j in range(nb):
            rows = slice(j * BL, (j + 1) * BL)
            kj = _rope(k_ref[rows, :], cosk_ref[rows, :], supk_ref[rows, :],
                       sdnk_ref[rows, :])
            kmean = jnp.mean(kj, axis=0, keepdims=True)
            for h in range(H):
                kmean_ref[h * nb + j:h * nb + j + 1, :] = jnp.where(
                    head_of_lane == h, kmean, 0.0)
            kb = kj.astype(BF16)
            for h in range(H):
                ks_ref[h, rows, :] = kb[:, h * Dh:(h + 1) * Dh]
            vt = v_ref[rows, :].T.astype(BF16)
            for h in range(H):
                vt_ref[h, 0:Dh, rows] = vt[h * Dh:(h + 1) * Dh, :]
        vt_ref[:, Dh:, :] = jnp.ones((H, BF16_SUBLANES, seq), BF16)

    def over_blocks(fn, n, carry):
        start = 0
        for width in MOBA_UNROLLS:
            def body(jj, carry, start=start, width=width):
                for t in range(width):
                    carry = fn(start + width * jj + t, carry)
                return carry
            groups = lax.div(n - start, width)
            carry = lax.fori_loop(0, groups, body, carry)
            start = start + groups * width
        return carry

    def attend(i, rows):
        qf = _rope(q_ref[rows, :], cosq_ref[rows, :], supq_ref[rows, :],
                   sdnq_ref[rows, :])
        qb = (qf * (Dh ** -0.5)).astype(BF16)
        blk = lax.broadcasted_iota(jnp.int32, (nb, BL), 0)
        past = blk < i
        gates = lax.dot_general(kmean_ref[...], qf, nt, precision=HIGHEST,
                                preferred_element_type=F32)
        for h in range(H):
            gate = jnp.where(past, gates[h * nb:(h + 1) * nb, :], -jnp.inf)
            rank = jnp.zeros((nb, BL), jnp.int32)
            for j2 in range(nb):
                g2 = gate[j2:j2 + 1, :]
                ahead = (g2 > gate) | ((g2 == gate) & (j2 < blk))
                rank = rank + ahead.astype(jnp.int32)
            sel_ref[h] = (past & (rank < MOBA_TOPK)).astype(F32)

        qh = [qb[:, h * Dh:(h + 1) * Dh] for h in range(H)]

        def masked_scores(h, j, keep):
            k0 = pl.multiple_of(j * BL, BL)
            st = lax.dot_general(ks_ref[h, pl.ds(k0, BL), :], qh[h], nt,
                                 preferred_element_type=F32)
            st = jnp.where(keep, st * LOG2E, -jnp.inf)
            s_ref[h, j] = st
            return jnp.max(st, axis=0, keepdims=True)

        own_mask = (lax.broadcasted_iota(jnp.int32, (BL, BL), 0) <=
                    lax.broadcasted_iota(jnp.int32, (BL, BL), 1))
        m_own = tuple(masked_scores(h, i, own_mask) for h in range(H))

        def pass1(j, ms):
            return tuple(
                jnp.maximum(ms[h], masked_scores(
                    h, j, sel_ref[h, pl.ds(j, 1), :] > 0.5)) for h in range(H))

        ms = over_blocks(pass1, i, m_own)

        def pass2(j, accs):
            k0 = pl.multiple_of(j * BL, BL)
            return tuple(
                accs[h] + jnp.dot(
                    vt_ref[h, :, pl.ds(k0, BL)],
                    jnp.exp2(s_ref[h, j] - ms[h]).astype(BF16),
                    preferred_element_type=F32)
                for h in range(H))

        init = (jnp.zeros((Dh + BF16_SUBLANES, BL), F32),) * H
        accs = over_blocks(pass2, i + 1, init)
        out_t = jnp.concatenate(
            [acc[:Dh] / acc[Dh:Dh + 1] for acc in accs], axis=0)
        o_ref[rows, :] = out_t.T

    for sub in range(MOBA_Q_PER_STEP):
        attend(step * MOBA_Q_PER_STEP + sub, slice(sub * BL, (sub + 1) * BL))


def _moba(q, k, v, tables, cast_ws, cast_layer, batch, seq):
    nb = seq // MOBA_BLOCK
    full = lambda b, i: (b, 0)
    steps = nb // MOBA_Q_PER_STEP
    qrow = lambda b, i: (b * steps + i, 0)
    trow = lambda b, i: (i, 0)
    const = lambda b, i: (0, 0)
    qspec = pl.BlockSpec((MOBA_Q_PER_STEP * MOBA_BLOCK, ATTN_WIDTH), qrow)
    tq = pl.BlockSpec((MOBA_Q_PER_STEP * MOBA_BLOCK, ATTN_WIDTH), trow)
    tk = pl.BlockSpec((seq, ATTN_WIDTH), const)
    kv = pl.BlockSpec((seq, ATTN_WIDTH), full)
    w_in_specs, w_out_specs, w_out_shapes = _weight_cast_specs(
        cast_layer, batch * steps, lambda b, i: b * steps + i)
    return pl.pallas_call(
        _with_weight_cast(_moba_body, 9, 1, len(w_in_specs)),
        grid=(batch, steps),
        in_specs=[qspec, kv, kv, tq, tq, tq, tk, tk, tk, *w_in_specs],
        out_specs=[qspec, *w_out_specs],
        out_shape=[jax.ShapeDtypeStruct((batch * seq, ATTN_WIDTH), F32),
                   *w_out_shapes],
        scratch_shapes=[
            pltpu.VMEM((ATTN_HEADS, seq, ATTN_HEAD_DIM), BF16),
            pltpu.VMEM((ATTN_HEADS, ATTN_HEAD_DIM + BF16_SUBLANES, seq), BF16),
            pltpu.VMEM((ATTN_HEADS * nb, ATTN_WIDTH), F32),
            pltpu.VMEM((ATTN_HEADS, nb, MOBA_BLOCK), F32),
            pltpu.VMEM((ATTN_HEADS, nb, MOBA_BLOCK, MOBA_BLOCK), F32),
        ],
        compiler_params=_params("arbitrary", "arbitrary"),
        name="moba",
    )(q, k, v, *tables, *tables, *(cast_ws if w_in_specs else ()))


def _outproj_ffn_body(x_ref, ys_ref, yp_ref, ya_ref, w_ref, g_ref,
                      gpre_ref, wg_ref, wu_ref, wd_ref, gpost_ref, o_ref, h_ref):
    o1 = SSD_INNER
    o2 = SSD_INNER + POOL_WIDTH
    tiles = _sub_tiles(OUT_TOKEN_TILE)
    xs = []
    for rows in tiles:
        m = jnp.dot(ys_ref[rows, :].astype(BF16), w_ref[0:o1, :],
                    preferred_element_type=F32)
        m = m + jnp.dot(yp_ref[rows, :].astype(BF16), w_ref[o1:o2, :],
                        preferred_element_type=F32)
        m = m + jnp.dot(ya_ref[rows, :].astype(BF16), w_ref[o2:, :],
                        preferred_element_type=F32)
        xs.append(x_ref[rows, :] + _rms(m) * g_ref[...])
    for rows, x in zip(tiles, xs):
        _ffn_gate_up(x, gpre_ref, wg_ref, wu_ref, h_ref.at[rows])
        o_ref[rows, :] = _ffn_down(x, wd_ref, gpost_ref, h_ref.at[rows])


def _outproj_ffn(x, y_ssd, y_pool, y_attn, w, g, gpre, wg, wu, wd, gpost, layer):
    m = x.shape[0]
    row = lambda i: (i, 0)
    return pl.pallas_call(
        _outproj_ffn_body,
        grid=(m // OUT_TOKEN_TILE,),
        in_specs=[
            pl.BlockSpec((OUT_TOKEN_TILE, D_MODEL), row),
            pl.BlockSpec((OUT_TOKEN_TILE, SSD_INNER), row),
            pl.BlockSpec((OUT_TOKEN_TILE, POOL_WIDTH), row),
            pl.BlockSpec((OUT_TOKEN_TILE, ATTN_WIDTH), row),
            _layer_spec((D_MODEL, D_MODEL), layer),
            _layer_spec((1, D_MODEL), layer),
            _layer_spec((1, D_MODEL), layer),
            _const_spec((D_MODEL, D_FF)),
            _const_spec((D_MODEL, D_FF)),
            _const_spec((D_FF, D_MODEL)),
            _layer_spec((1, D_MODEL), layer),
        ],
        out_specs=pl.BlockSpec((OUT_TOKEN_TILE, D_MODEL), row),
        out_shape=jax.ShapeDtypeStruct((m, D_MODEL), F32),
        scratch_shapes=[pltpu.VMEM((OUT_TOKEN_TILE, D_FF), BF16)],
        compiler_params=_params("parallel"),
        name="outproj_ffn",
    )(x, y_ssd, y_pool, y_attn, w, g, gpre, wg, wu, wd, gpost)


def _rope_tables(seq):
    half = ROPE_DIM // 2
    inv_freq = ROPE_THETA ** (-jnp.arange(0, ROPE_DIM, 2, dtype=F32) / ROPE_DIM)
    ang = jnp.arange(seq, dtype=F32)[:, None] * inv_freq[None, :]
    rest = ATTN_HEAD_DIM - ROPE_DIM
    one = jnp.ones((seq, rest), F32)
    zero = jnp.zeros((seq, rest), F32)
    zh = jnp.zeros((seq, half), F32)
    cos = jnp.concatenate([jnp.cos(ang), jnp.cos(ang), one], axis=-1)
    sin_up = jnp.concatenate([zh, jnp.sin(ang), zero], axis=-1)
    sin_dn = jnp.concatenate([-jnp.sin(ang), zh, zero], axis=-1)
    return tuple(jnp.tile(t, (1, ATTN_HEADS)) for t in (cos, sin_up, sin_dn))


def _split_w_in(w):
    pad = [(0, 0)] * (w.ndim - 1) + [(0, DT_PAD - SSD_HEADS)]
    pieces = [w[..., OFF_XBC:OFF_DT], w[..., OFF_POOL:OFF_Q], w[..., OFF_Z:OFF_XBC],
              w[..., OFF_Q:IN_COLS], jnp.pad(w[..., OFF_DT:OFF_POOL], pad)]
    return tuple(p.astype(BF16) for p in pieces)


def _per_head_lanes(p):
    return jnp.repeat(p, SSD_HEAD_DIM)[None, :]


def _per_head_cols(p):
    pad = [(0, 0)] * (p.ndim - 1) + [(0, DT_PAD - SSD_HEADS)]
    return jnp.pad(p, pad)[..., None, :]


def _pool_blockdiag(w):
    out = jnp.zeros(w.shape[:-3] + (POOL_WIDTH, POOL_WIDTH), F32)
    for g in range(POOL_GROUPS):
        sl = slice(g * POOL_GROUP_DIM, (g + 1) * POOL_GROUP_DIM)
        out = out.at[..., sl, sl].set(w[..., g, :, :])
    return out.astype(BF16)


def kernel(x, ff1_norm_pre, ff1_w_gate, ff1_w_up, ff1_w_down, ff1_norm_post,
           mix_norm_pre, w_in, conv_w, conv_b, dt_bias, a_log, d_skip, ssd_norm,
           pool_w, pool_scale, w_out, mix_norm_post,
           ff2_norm_pre, ff2_w_gate, ff2_w_up, ff2_w_down, ff2_norm_post):
    batch, seq, d = x.shape
    depth = w_in.shape[0]
    h = x.reshape(batch * seq, d)
    rope_tables = _rope_tables(seq)
    row = lambda p: p[None, :]
    stack_row = lambda p: p[:, None, :]
    bf = lambda w: w.astype(BF16)
    ff1_w = (ff1_w_gate, ff1_w_up, ff1_w_down)
    ff2_w = (ff2_w_gate, ff2_w_up, ff2_w_down)
    g_ff1 = (stack_row(ff1_norm_pre), stack_row(ff1_norm_post))
    g_ff2 = (stack_row(ff2_norm_pre), stack_row(ff2_norm_post))
    w_in_b = _split_w_in(w_in)
    w_out_b = bf(w_out)
    g_mix_pre = stack_row(mix_norm_pre)
    g_mix_post = stack_row(mix_norm_post)
    mix_front = (g_mix_pre, w_in_b, _pool_blockdiag(pool_w), stack_row(pool_scale))
    ff1_b = tuple(bf(w[0]) for w in ff1_w)
    for l in range(depth):
        h, xbc, y_pool, z, q, k, v, dt = _ffn_inproj(
            h, g_ff1[0], *ff1_b, g_ff1[1], *mix_front, l, seq)
        y_ssd, *ff2_b = _ssd(xbc, z, dt, conv_w[l], row(conv_b[l]),
                             _per_head_cols(dt_bias[l]), _per_head_cols(a_log[l]),
                             _per_head_lanes(d_skip[l]), row(ssd_norm[l]),
                             ff2_w, l, batch, seq)
        nxt = l + 1 if l + 1 < depth else None
        y_attn, *ff1_b = _moba(q, k, v, rope_tables, ff1_w, nxt, batch, seq)
        h = _outproj_ffn(h, y_ssd, y_pool, y_attn, w_out_b, g_mix_post,
                         g_ff2[0], *ff2_b, g_ff2[1], l)
    return h.reshape(batch, seq, d)
```

```python
import functools
import math

import jax
import jax.numpy as jnp
from jax import lax
from jax.experimental import pallas as pl
from jax.experimental.pallas import tpu as pltpu

F32 = jnp.float32
BF16 = jnp.bfloat16
HIGHEST = lax.Precision.HIGHEST

D_MODEL = 1024
D_FF = 2816
SSD_INNER = 512
SSD_HEAD_DIM = 64
SSD_HEADS = SSD_INNER // SSD_HEAD_DIM
SSD_GROUPS = 2
SSD_STATE = 128
SSD_CONV = 4
SSD_CHUNK = 128
SSD_BC = SSD_GROUPS * SSD_STATE
SSD_XBC = SSD_INNER + 2 * SSD_BC
POOL_WIDTH = 256
POOL_GROUPS = 4
POOL_GROUP_DIM = POOL_WIDTH // POOL_GROUPS
ATTN_WIDTH = 256
ATTN_HEAD_DIM = 64
ATTN_HEADS = ATTN_WIDTH // ATTN_HEAD_DIM
ROPE_DIM = ATTN_HEAD_DIM // 4
ROPE_THETA = 500000.0
MOBA_BLOCK = 256
MOBA_TOPK = 3
RMS_EPS = 1e-6

OFF_Z = 0
OFF_XBC = OFF_Z + SSD_INNER
OFF_DT = OFF_XBC + SSD_XBC
OFF_POOL = OFF_DT + SSD_HEADS
OFF_Q = OFF_POOL + POOL_WIDTH
OFF_K = OFF_Q + ATTN_WIDTH
OFF_V = OFF_K + ATTN_WIDTH
IN_COLS = OFF_V + ATTN_WIDTH

LANES = 128
SUBLANES = 8
MXU_DIM = 256
VMEM_LIMIT_BYTES = 56 * 1024 * 1024

DT_PAD = LANES
TOKEN_TILE = 512
OUT_TOKEN_TILE = 1024
SUB_TILE = 256
FF_CHUNK = MXU_DIM
SSD_STEP_ROWS = 8 * SSD_CHUNK
MOBA_UNROLLS = (4, 2, 1)
BF16_SUBLANES = 16
LOG2E = math.log2(math.e)


def _rms(x):
    return x * lax.rsqrt(jnp.mean(x * x, axis=-1, keepdims=True) + RMS_EPS)


def _silu(x):
    return x * jax.nn.sigmoid(x)


def _softplus(x):
    return jnp.maximum(x, 0.0) + jnp.log1p(jnp.exp(-jnp.abs(x)))


def _const_spec(shape):
    zeros = (0,) * len(shape)
    return pl.BlockSpec(shape, lambda *_: zeros, pipeline_mode=pl.Buffered(1))


def _layer_spec(shape, layer):
    zeros = (0,) * len(shape)
    return pl.BlockSpec((None,) + shape, lambda *_: (layer,) + zeros,
                        pipeline_mode=pl.Buffered(1))


def _params(*sem):
    return pltpu.CompilerParams(dimension_semantics=sem,
                                vmem_limit_bytes=VMEM_LIMIT_BYTES)


def _ffn_gate_up(x, gpre_ref, wg_ref, wu_ref, h_ref):
    xb = (_rms(x) * gpre_ref[...]).astype(BF16)
    for c in range(0, D_FF, FF_CHUNK):
        sl = slice(c, min(c + FF_CHUNK, D_FF))
        g = jnp.dot(xb, wg_ref[:, sl], preferred_element_type=F32)
        u = jnp.dot(xb, wu_ref[:, sl], preferred_element_type=F32)
        h_ref[:, sl] = (_silu(g) * u).astype(BF16)


def _ffn_down(x, wd_ref, gpost_ref, h_ref):
    f = jnp.dot(h_ref[...], wd_ref[...], preferred_element_type=F32)
    return x + 0.5 * (_rms(f) * gpost_ref[...])


def _sub_tiles(tile_rows):
    return [slice(r0, r0 + SUB_TILE) for r0 in range(0, tile_rows, SUB_TILE)]


_PROJ_WIDTHS = (SSD_XBC, POOL_WIDTH, SSD_INNER, 3 * ATTN_WIDTH, DT_PAD)
_PROJ_OUT_WIDTHS = (SSD_XBC, POOL_WIDTH, SSD_INNER, ATTN_WIDTH, ATTN_WIDTH,
                    ATTN_WIDTH, DT_PAD)
POOL_HALO = 2 ** POOL_GROUPS


def _pool_tile(u, tail, pos, w_ref, scale_ref):
    group = lax.broadcasted_iota(jnp.int32, u.shape, 1) // POOL_GROUP_DIM
    win_sum = jnp.concatenate([tail, u], axis=0)
    mean = jnp.zeros_like(u)
    for g in range(POOL_GROUPS):
        half = 2 ** g
        win_sum = win_sum + pltpu.roll(win_sum, half, 0)
        count = jnp.minimum(pos + 1, 2 * half).astype(F32)
        mean = jnp.where(group == g, win_sum[POOL_HALO:, :] / count, mean)
    d = (mean - u).astype(BF16)
    return jnp.dot(d, w_ref[...], preferred_element_type=F32) * scale_ref[...]


def _ffn_inproj_body(tiles_per_seq, x_ref, gpre_ref, wg_ref, wu_ref, wd_ref,
                     gpost_ref, g_ref, *rest):
    n_proj = len(_PROJ_WIDTHS)
    w_refs = rest[:n_proj]
    (poolw_ref, pscale_ref, o_ref, xbc_ref, yp_ref, z_ref, q_ref, k_ref, v_ref,
     dt_ref, h_ref, utail_ref) = rest[n_proj:]
    tile_in_seq = pl.program_id(0) % tiles_per_seq

    @pl.when(tile_in_seq == 0)
    def _():
        utail_ref[...] = jnp.zeros_like(utail_ref)

    tiles = _sub_tiles(TOKEN_TILE)

    def down(rows):
        x = _ffn_down(x_ref[rows, :], wd_ref, gpost_ref, h_ref.at[rows])
        o_ref[rows, :] = x
        return (_rms(x) * g_ref[...]).astype(BF16)

    def project(rows, xb, utail):
        def proj(i):
            return jnp.dot(xb, w_refs[i][...], preferred_element_type=F32)

        u = proj(1)
        xbc_ref[rows, :] = proj(0)
        z_ref[rows, :] = proj(2)
        qkv = proj(3)
        for i, ref in enumerate((q_ref, k_ref, v_ref)):
            ref[rows, :] = qkv[:, i * ATTN_WIDTH:(i + 1) * ATTN_WIDTH]
        dt_ref[rows, :] = proj(4)
        pos = (tile_in_seq * TOKEN_TILE + rows.start
               + lax.broadcasted_iota(jnp.int32, u.shape, 0))
        yp_ref[rows, :] = _pool_tile(u, utail, pos, poolw_ref, pscale_ref)
        return u[SUB_TILE - POOL_HALO:, :]

    utail = utail_ref[...]
    pending = None
    for rows in tiles:
        _ffn_gate_up(x_ref[rows, :], gpre_ref, wg_ref, wu_ref, h_ref.at[rows])
        xb = down(rows)
        if pending is not None:
            utail = project(*pending, utail)
        pending = (rows, xb)
    utail_ref[...] = project(*pending, utail)


def _ffn_inproj(x, gpre, wg, wu, wd, gpost, g, ws, poolw, pscale, layer, seq):
    m = x.shape[0]
    row = lambda i: (i, 0)
    widths = (D_MODEL,) + _PROJ_OUT_WIDTHS
    return pl.pallas_call(
        functools.partial(_ffn_inproj_body, seq // TOKEN_TILE),
        grid=(m // TOKEN_TILE,),
        in_specs=[
            pl.BlockSpec((TOKEN_TILE, D_MODEL), row),
            _layer_spec((1, D_MODEL), layer),
            _const_spec((D_MODEL, D_FF)),
            _const_spec((D_MODEL, D_FF)),
            _const_spec((D_FF, D_MODEL)),
            _layer_spec((1, D_MODEL), layer),
            _layer_spec((1, D_MODEL), layer),
            *[_layer_spec((D_MODEL, width), layer) for width in _PROJ_WIDTHS],
            _layer_spec((POOL_WIDTH, POOL_WIDTH), layer),
            _layer_spec((1, POOL_WIDTH), layer),
        ],
        out_specs=[pl.BlockSpec((TOKEN_TILE, width), row) for width in widths],
        out_shape=[jax.ShapeDtypeStruct((m, width), F32) for width in widths],
        scratch_shapes=[
            pltpu.VMEM((TOKEN_TILE, D_FF), BF16),
            pltpu.VMEM((POOL_HALO, POOL_WIDTH), F32),
        ],
        compiler_params=_params("arbitrary"),
        name="ffn_inproj",
    )(x, gpre, wg, wu, wd, gpost, g, *ws, poolw, pscale)


CAST_CHUNKS = 16
_FFN_W_SHAPES = ((D_MODEL, D_FF), (D_MODEL, D_FF), (D_FF, D_MODEL))


def _with_weight_cast(body, n_in, n_out, n_w):
    def wrapped(*refs):
        ins, refs = refs[:n_in], refs[n_in:]
        w_ins, refs = refs[:n_w], refs[n_w:]
        outs, refs = refs[:n_out], refs[n_out:]
        w_outs, scratch = refs[:n_w], refs[n_w:]
        body(*ins, *outs, *scratch)
        for w_in, w_out in zip(w_ins, w_outs):
            w_out[...] = w_in[...].astype(BF16)

    return wrapped


def _weight_cast_specs(layer, n_steps, step_of_grid):
    if layer is None:
        return [], [], []
    n_chunks = math.gcd(n_steps, CAST_CHUNKS)
    steps_per_chunk = n_steps // n_chunks
    chunk_of_step = lambda *g: step_of_grid(*g) // steps_per_chunk
    in_specs, out_specs, out_shapes = [], [], []
    for rows, cols in _FFN_W_SHAPES:
        chunk = rows // n_chunks
        in_specs.append(pl.BlockSpec(
            (None, chunk, cols), lambda *g: (layer, chunk_of_step(*g), 0)))
        out_specs.append(pl.BlockSpec(
            (chunk, cols), lambda *g: (chunk_of_step(*g), 0)))
        out_shapes.append(jax.ShapeDtypeStruct((rows, cols), BF16))
    return in_specs, out_specs, out_shapes


def _causal_conv4(x, tail, w_ref, b_ref):
    x_ext = jnp.concatenate([tail, x], axis=0)
    prev_ext = pltpu.roll(x_ext, 1, 0)
    near = w_ref[3:4, :] * x + w_ref[2:3, :] * prev_ext[SUBLANES:, :]
    far_ext = w_ref[1:2, :] * x_ext + w_ref[0:1, :] * prev_ext
    return near + pltpu.roll(far_ext, 2, 0)[SUBLANES:, :] + b_ref[...]


def _heads_to_lanes(xc):
    rows = xc.shape[0]
    lane = lax.broadcasted_iota(jnp.int32, (rows, LANES), 1)
    per_tile = LANES // SSD_HEAD_DIM
    parts = []
    for t in range(SSD_INNER // LANES):
        tile = jnp.broadcast_to(xc[:, t * per_tile:t * per_tile + 1], (rows, LANES))
        for i in range(1, per_tile):
            h = t * per_tile + i
            tile = jnp.where(lane < i * SSD_HEAD_DIM, tile,
                             jnp.broadcast_to(xc[:, h:h + 1], (rows, LANES)))
        parts.append(tile)
    return jnp.concatenate(parts, axis=1)


def _ssd_body(xbc_ref, z_ref, dt_ref, convw_ref, convb_ref, dtb_ref, alog_ref,
              dskip_ref, gn_ref, o_ref, tail_ref, state_ref):
    L = SSD_CHUNK
    HP = SSD_INNER
    P = SSD_HEAD_DIM
    N = SSD_STATE
    GW = HP // SSD_GROUPS
    HPG = SSD_HEADS // SSD_GROUPS

    @pl.when(pl.program_id(1) == 0)
    def _():
        tail_ref[...] = jnp.zeros_like(tail_ref)
        state_ref[...] = jnp.zeros_like(state_ref)

    r = lax.broadcasted_iota(jnp.int32, (L, L), 0)
    s = lax.broadcasted_iota(jnp.int32, (L, L), 1)
    causal = s <= r
    tril = causal.astype(F32)
    pair_lane = lax.broadcasted_iota(jnp.int32, (L, 2 * P), 1)
    neg_a = -jnp.exp(alog_ref[...])

    def chunk(c, tail):
        r0 = pl.multiple_of(c * L, L)
        rows = pl.ds(r0, L)
        x = xbc_ref[rows, :]
        xc = _silu(_causal_conv4(x, tail, convw_ref, convb_ref))
        xs = xc[:, :HP]
        bm = xc[:, HP:HP + SSD_BC].astype(BF16)
        cm = xc[:, HP + SSD_BC:].astype(BF16)

        dt_c = _softplus(dt_ref[rows, :] + dtb_ref[...])
        acs_c = jnp.dot(tril, dt_c * neg_a, precision=HIGHEST,
                        preferred_element_type=F32)
        acs_t = acs_c.T
        dt = _heads_to_lanes(dt_c)
        acs = _heads_to_lanes(acs_c)
        a_last = acs[L - 1:L, :]
        decay_out = jnp.exp(acs)
        decay_in = jnp.exp(a_last - acs)
        chunk_decay = jnp.exp(a_last)

        xdt = xs * dt
        xdt_b = xdt.astype(BF16)
        xw_b = (xdt * decay_in).astype(BF16)

        y_parts = []
        for g in range(SSD_GROUPS):
            bg = bm[:, g * N:(g + 1) * N]
            cg = cm[:, g * N:(g + 1) * N]
            gsl = slice(g * GW, (g + 1) * GW)
            cb = lax.dot_general(cg, bg, (((1,), (1,)), ((), ())),
                                 preferred_element_type=F32)
            st = state_ref[:, gsl]
            y_off = jnp.dot(cg, st.astype(BF16), preferred_element_type=F32)
            new_st = lax.dot_general(bg, xw_b[:, gsl], (((0,), (0,)), ((), ())),
                                     preferred_element_type=F32)
            state_ref[:, gsl] = st * chunk_decay[:, gsl] + new_st
            for pair in range(HPG // 2):
                lo = g * GW + pair * 2 * P
                x_pair = xdt_b[:, lo:lo + 2 * P]
                ys = []
                for i in range(2):
                    h = lo // P + i
                    seg = jnp.exp(jnp.where(
                        causal, acs_c[:, h:h + 1] - acs_t[h:h + 1, :], -jnp.inf))
                    ys.append(jnp.dot((cb * seg).astype(BF16), x_pair,
                                      preferred_element_type=F32))
                y_diag = jnp.where(pair_lane < P, ys[0], ys[1])
                psl = slice(pair * 2 * P, (pair + 1) * 2 * P)
                y_parts.append(
                    y_diag + y_off[:, psl] * decay_out[:, lo:lo + 2 * P])
        y = jnp.concatenate(y_parts, axis=-1) + dskip_ref[...] * xs
        o_ref[rows, :] = _rms(y * _silu(z_ref[rows, :])) * gn_ref[...]
        return x[L - SUBLANES:, :]

    n_chunks = xbc_ref.shape[0] // L
    tail_ref[...] = lax.fori_loop(0, n_chunks, chunk, tail_ref[...], unroll=True)


def _ssd(xbc, z, dt, convw, convb, dtb, alog, dskip, gn, cast_ws, cast_layer,
         batch, seq):
    nblk = seq // SSD_STEP_ROWS
    row = lambda b, c: (b * nblk + c, 0)
    w_in_specs, w_out_specs, w_out_shapes = _weight_cast_specs(
        cast_layer, batch * nblk, lambda b, c: b * nblk + c)
    return pl.pallas_call(
        _with_weight_cast(_ssd_body, 9, 1, len(w_in_specs)),
        grid=(batch, nblk),
        in_specs=[
            pl.BlockSpec((SSD_STEP_ROWS, SSD_XBC), row),
            pl.BlockSpec((SSD_STEP_ROWS, SSD_INNER), row),
            pl.BlockSpec((SSD_STEP_ROWS, DT_PAD), row),
            _const_spec((SSD_CONV, SSD_XBC)),
            _const_spec((1, SSD_XBC)),
            _const_spec((1, DT_PAD)),
            _const_spec((1, DT_PAD)),
            _const_spec((1, SSD_INNER)),
            _const_spec((1, SSD_INNER)),
            *w_in_specs,
        ],
        out_specs=[pl.BlockSpec((SSD_STEP_ROWS, SSD_INNER), row), *w_out_specs],
        out_shape=[jax.ShapeDtypeStruct((batch * seq, SSD_INNER), F32),
                   *w_out_shapes],
        scratch_shapes=[
            pltpu.VMEM((SUBLANES, SSD_XBC), F32),
            pltpu.VMEM((SSD_STATE, SSD_INNER), F32),
        ],
        compiler_params=_params("arbitrary", "arbitrary"),
        name="ssd",
    )(xbc, z, dt, convw, convb, dtb, alog, dskip, gn, *cast_ws)


def _rope(t, cos, sin_up, sin_dn):
    half = ROPE_DIM // 2
    return (t * cos + pltpu.roll(t, half, 1) * sin_up
            + pltpu.roll(t, ATTN_WIDTH - half, 1) * sin_dn)


def _moba_body(q_ref, k_ref, v_ref, cos_ref, sup_ref, sdn_ref, o_ref,
               ks_ref, vt_ref, kmean_ref, sel_ref, s_ref):
    seq = k_ref.shape[0]
    nb = seq // MOBA_BLOCK
    Dh = ATTN_HEAD_DIM
    BL = MOBA_BLOCK
    H = ATTN_HEADS
    nt = (((1,), (1,)), ((), ()))
    zero = pl.program_id(1)

    head_of_lane = lax.broadcasted_iota(jnp.int32, (1, ATTN_WIDTH), 1) // Dh

    def stage_keys(j):
        rows = slice(j * BL, (j + 1) * BL)
        kj = _rope(k_ref[rows, :], cos_ref[rows, :], sup_ref[rows, :],
                   sdn_ref[rows, :])
        kmean = jnp.mean(kj, axis=0, keepdims=True)
        for h in range(H):
            kmean_ref[h * nb + j:h * nb + j + 1, :] = jnp.where(
                head_of_lane == h, kmean, 0.0)
        kb = kj.astype(BF16)
        for h in range(H):
            ks_ref[h, rows, :] = kb[:, h * Dh:(h + 1) * Dh]
        vt = v_ref[rows, :].T.astype(BF16)
        for h in range(H):
            vt_ref[h, 0:Dh, rows] = vt[h * Dh:(h + 1) * Dh, :]

    vt_ref[:, Dh:, :] = jnp.ones((H, BF16_SUBLANES, seq), BF16)
    kmean_ref[...] = jnp.zeros_like(kmean_ref)

    def over_blocks(fn, n, carry):
        start = 0
        for width in MOBA_UNROLLS:
            def body(jj, carry, start=start, width=width):
                for t in range(width):
                    carry = fn(start + width * jj + t, carry)
                return carry
            groups = lax.div(n - start, width)
            carry = lax.fori_loop(0, groups, body, carry)
            start = start + groups * width
        return carry

    def attend(i, rows):
        qf = _rope(q_ref[rows, :], cos_ref[rows, :], sup_ref[rows, :],
                   sdn_ref[rows, :])
        qb = (qf * (Dh ** -0.5)).astype(BF16)
        blk = lax.broadcasted_iota(jnp.int32, (nb, BL), 0)
        past = blk < i
        gates = lax.dot_general(kmean_ref[...], qf, nt, precision=HIGHEST,
                                preferred_element_type=F32)
        for h in range(H):
            gate = jnp.where(past, gates[h * nb:(h + 1) * nb, :], -jnp.inf)
            rank = jnp.zeros((nb, BL), jnp.int32)
            for j2 in range(nb):
                g2 = gate[j2:j2 + 1, :]
                ahead = (g2 > gate) | ((g2 == gate) & (j2 < blk))
                rank = rank + ahead.astype(jnp.int32)
            sel_ref[h] = (past & (rank < MOBA_TOPK)).astype(F32)

        qh = [qb[:, h * Dh:(h + 1) * Dh] for h in range(H)]

        def masked_scores(h, j, keep):
            k0 = pl.multiple_of(j * BL, BL)
            st = lax.dot_general(ks_ref[h, pl.ds(k0, BL), :], qh[h], nt,
                                 preferred_element_type=F32)
            st = jnp.where(keep, st * LOG2E, -jnp.inf)
            s_ref[h, j] = st
            return jnp.max(st, axis=0, keepdims=True)

        own_mask = (lax.broadcasted_iota(jnp.int32, (BL, BL), 0) <=
                    lax.broadcasted_iota(jnp.int32, (BL, BL), 1))
        m_own = tuple(masked_scores(h, i, own_mask) for h in range(H))

        def pass1(j, ms):
            return tuple(
                jnp.maximum(ms[h], masked_scores(
                    h, j, sel_ref[h, pl.ds(j, 1), :] > 0.5)) for h in range(H))

        ms = over_blocks(pass1, i, m_own)

        def pass2(j, accs):
            k0 = pl.multiple_of(j * BL, BL)
            return tuple(
                accs[h] + jnp.dot(
                    vt_ref[h, :, pl.ds(k0, BL)],
                    jnp.exp2(s_ref[h, j] - ms[h]).astype(BF16),
                    preferred_element_type=F32)
                for h in range(H))

        init = (jnp.zeros((Dh + BF16_SUBLANES, BL), F32),) * H
        accs = over_blocks(pass2, i + 1, init)
        out_t = jnp.concatenate(
            [acc[:Dh] / acc[Dh:Dh + 1] for acc in accs], axis=0)
        o_ref[rows, :] = out_t.T

    for j in range(nb):
        stage_keys(j)
        attend(zero + j, slice(j * BL, (j + 1) * BL))


def _moba(q, k, v, tables, cast_ws, cast_layer, batch, seq):
    nb = seq // MOBA_BLOCK
    whole = pl.BlockSpec((seq, ATTN_WIDTH), lambda b, _: (b, 0))
    table = _const_spec((seq, ATTN_WIDTH))
    w_in_specs, w_out_specs, w_out_shapes = _weight_cast_specs(
        cast_layer, batch, lambda b, _: b)
    return pl.pallas_call(
        _with_weight_cast(_moba_body, 6, 1, len(w_in_specs)),
        grid=(batch, 1),
        in_specs=[whole, whole, whole, table, table, table, *w_in_specs],
        out_specs=[whole, *w_out_specs],
        out_shape=[jax.ShapeDtypeStruct((batch * seq, ATTN_WIDTH), F32),
                   *w_out_shapes],
        scratch_shapes=[
            pltpu.VMEM((ATTN_HEADS, seq, ATTN_HEAD_DIM), BF16),
            pltpu.VMEM((ATTN_HEADS, ATTN_HEAD_DIM + BF16_SUBLANES, seq), BF16),
            pltpu.VMEM((ATTN_HEADS * nb, ATTN_WIDTH), F32),
            pltpu.VMEM((ATTN_HEADS, nb, MOBA_BLOCK), F32),
            pltpu.VMEM((ATTN_HEADS, nb, MOBA_BLOCK, MOBA_BLOCK), F32),
        ],
        compiler_params=_params("arbitrary", "arbitrary"),
        name="moba",
    )(q, k, v, *tables, *(cast_ws if w_in_specs else ()))


def _outproj_ffn_body(x_ref, ys_ref, yp_ref, ya_ref, w_ref, g_ref,
                      gpre_ref, wg_ref, wu_ref, wd_ref, gpost_ref, o_ref, h_ref):
    o1 = SSD_INNER
    o2 = SSD_INNER + POOL_WIDTH
    tiles = _sub_tiles(OUT_TOKEN_TILE)
    xs = []
    for rows in tiles:
        m = jnp.dot(ys_ref[rows, :].astype(BF16), w_ref[0:o1, :],
                    preferred_element_type=F32)
        m = m + jnp.dot(yp_ref[rows, :].astype(BF16), w_ref[o1:o2, :],
                        preferred_element_type=F32)
        m = m + jnp.dot(ya_ref[rows, :].astype(BF16), w_ref[o2:, :],
                        preferred_element_type=F32)
        xs.append(x_ref[rows, :] + _rms(m) * g_ref[...])
    for rows, x in zip(tiles, xs):
        _ffn_gate_up(x, gpre_ref, wg_ref, wu_ref, h_ref.at[rows])
        o_ref[rows, :] = _ffn_down(x, wd_ref, gpost_ref, h_ref.at[rows])


def _outproj_ffn(x, y_ssd, y_pool, y_attn, w, g, gpre, wg, wu, wd, gpost, layer):
    m = x.shape[0]
    row = lambda i: (i, 0)
    return pl.pallas_call(
        _outproj_ffn_body,
        grid=(m // OUT_TOKEN_TILE,),
        in_specs=[
            pl.BlockSpec((OUT_TOKEN_TILE, D_MODEL), row),
            pl.BlockSpec((OUT_TOKEN_TILE, SSD_INNER), row),
            pl.BlockSpec((OUT_TOKEN_TILE, POOL_WIDTH), row),
            pl.BlockSpec((OUT_TOKEN_TILE, ATTN_WIDTH), row),
            _layer_spec((D_MODEL, D_MODEL), layer),
            _layer_spec((1, D_MODEL), layer),
            _layer_spec((1, D_MODEL), layer),
            _const_spec((D_MODEL, D_FF)),
            _const_spec((D_MODEL, D_FF)),
            _const_spec((D_FF, D_MODEL)),
            _layer_spec((1, D_MODEL), layer),
        ],
        out_specs=pl.BlockSpec((OUT_TOKEN_TILE, D_MODEL), row),
        out_shape=jax.ShapeDtypeStruct((m, D_MODEL), F32),
        scratch_shapes=[pltpu.VMEM((OUT_TOKEN_TILE, D_FF), BF16)],
        compiler_params=_params("parallel"),
        name="outproj_ffn",
    )(x, y_ssd, y_pool, y_attn, w, g, gpre, wg, wu, wd, gpost)


def _rope_tables(seq):
    half = ROPE_DIM // 2
    inv_freq = ROPE_THETA ** (-jnp.arange(0, ROPE_DIM, 2, dtype=F32) / ROPE_DIM)
    ang = jnp.arange(seq, dtype=F32)[:, None] * inv_freq[None, :]
    rest = ATTN_HEAD_DIM - ROPE_DIM
    one = jnp.ones((seq, rest), F32)
    zero = jnp.zeros((seq, rest), F32)
    zh = jnp.zeros((seq, half), F32)
    cos = jnp.concatenate([jnp.cos(ang), jnp.cos(ang), one], axis=-1)
    sin_up = jnp.concatenate([zh, jnp.sin(ang), zero], axis=-1)
    sin_dn = jnp.concatenate([-jnp.sin(ang), zh, zero], axis=-1)
    return tuple(jnp.tile(t, (1, ATTN_HEADS)) for t in (cos, sin_up, sin_dn))


def _split_w_in(w):
    pad = [(0, 0)] * (w.ndim - 1) + [(0, DT_PAD - SSD_HEADS)]
    pieces = [w[..., OFF_XBC:OFF_DT], w[..., OFF_POOL:OFF_Q], w[..., OFF_Z:OFF_XBC],
              w[..., OFF_Q:IN_COLS], jnp.pad(w[..., OFF_DT:OFF_POOL], pad)]
    return tuple(p.astype(BF16) for p in pieces)


def _per_head_lanes(p):
    return jnp.repeat(p, SSD_HEAD_DIM)[None, :]


def _per_head_cols(p):
    pad = [(0, 0)] * (p.ndim - 1) + [(0, DT_PAD - SSD_HEADS)]
    return jnp.pad(p, pad)[..., None, :]


def _pool_blockdiag(w):
    out = jnp.zeros(w.shape[:-3] + (POOL_WIDTH, POOL_WIDTH), F32)
    for g in range(POOL_GROUPS):
        sl = slice(g * POOL_GROUP_DIM, (g + 1) * POOL_GROUP_DIM)
        out = out.at[..., sl, sl].set(w[..., g, :, :])
    return out.astype(BF16)


def kernel(x, ff1_norm_pre, ff1_w_gate, ff1_w_up, ff1_w_down, ff1_norm_post,
           mix_norm_pre, w_in, conv_w, conv_b, dt_bias, a_log, d_skip, ssd_norm,
           pool_w, pool_scale, w_out, mix_norm_post,
           ff2_norm_pre, ff2_w_gate, ff2_w_up, ff2_w_down, ff2_norm_post):
    batch, seq, d = x.shape
    depth = w_in.shape[0]
    h = x.reshape(batch * seq, d)
    rope_tables = _rope_tables(seq)
    row = lambda p: p[None, :]
    stack_row = lambda p: p[:, None, :]
    bf = lambda w: w.astype(BF16)
    ff1_w = (ff1_w_gate, ff1_w_up, ff1_w_down)
    ff2_w = (ff2_w_gate, ff2_w_up, ff2_w_down)
    g_ff1 = (stack_row(ff1_norm_pre), stack_row(ff1_norm_post))
    g_ff2 = (stack_row(ff2_norm_pre), stack_row(ff2_norm_post))
    w_in_b = _split_w_in(w_in)
    w_out_b = bf(w_out)
    g_mix_pre = stack_row(mix_norm_pre)
    g_mix_post = stack_row(mix_norm_post)
    mix_front = (g_mix_pre, w_in_b, _pool_blockdiag(pool_w), stack_row(pool_scale))
    ff1_b = tuple(bf(w[0]) for w in ff1_w)
    for l in range(depth):
        h, xbc, y_pool, z, q, k, v, dt = _ffn_inproj(
            h, g_ff1[0], *ff1_b, g_ff1[1], *mix_front, l, seq)
        y_ssd, *ff2_b = _ssd(xbc, z, dt, conv_w[l], row(conv_b[l]),
                             _per_head_cols(dt_bias[l]), _per_head_cols(a_log[l]),
                             _per_head_lanes(d_skip[l]), row(ssd_norm[l]),
                             ff2_w, l, batch, seq)
        nxt = l + 1 if l + 1 < depth else None
        y_attn, *ff1_b = _moba(q, k, v, rope_tables, ff1_w, nxt, batch, seq)
        h = _outproj_ffn(h, y_ssd, y_pool, y_attn, w_out_b, g_mix_post,
                         g_ff2[0], *ff2_b, g_ff2[1], l)
    return h.reshape(batch, seq, d)
```

```python
import functools
import math

import jax
import jax.numpy as jnp
from jax import lax
from jax.experimental import pallas as pl
from jax.experimental.pallas import tpu as pltpu

F32 = jnp.float32
BF16 = jnp.bfloat16
HIGHEST = lax.Precision.HIGHEST

D_MODEL = 1024
D_FF = 2816
SSD_INNER = 512
SSD_HEAD_DIM = 64
SSD_HEADS = SSD_INNER // SSD_HEAD_DIM
SSD_GROUPS = 2
SSD_STATE = 128
SSD_CONV = 4
SSD_CHUNK = 128
SSD_BC = SSD_GROUPS * SSD_STATE
SSD_XBC = SSD_INNER + 2 * SSD_BC
POOL_WIDTH = 256
POOL_GROUPS = 4
POOL_GROUP_DIM = POOL_WIDTH // POOL_GROUPS
ATTN_WIDTH = 256
ATTN_HEAD_DIM = 64
ATTN_HEADS = ATTN_WIDTH // ATTN_HEAD_DIM
ROPE_DIM = ATTN_HEAD_DIM // 4
ROPE_THETA = 500000.0
MOBA_BLOCK = 256
MOBA_TOPK = 3
RMS_EPS = 1e-6

OFF_Z = 0
OFF_XBC = OFF_Z + SSD_INNER
OFF_DT = OFF_XBC + SSD_XBC
OFF_POOL = OFF_DT + SSD_HEADS
OFF_Q = OFF_POOL + POOL_WIDTH
OFF_K = OFF_Q + ATTN_WIDTH
OFF_V = OFF_K + ATTN_WIDTH
IN_COLS = OFF_V + ATTN_WIDTH

LANES = 128
SUBLANES = 8
MXU_DIM = 256
VMEM_LIMIT_BYTES = 56 * 1024 * 1024

DT_PAD = LANES
TOKEN_TILE = 512
OUT_TOKEN_TILE = 1024
SUB_TILE = 256
FF_CHUNK = MXU_DIM
SSD_STEP_ROWS = 8 * SSD_CHUNK
BF16_SUBLANES = 16
LOG2E = math.log2(math.e)


def _rms(x):
    return x * lax.rsqrt(jnp.mean(x * x, axis=-1, keepdims=True) + RMS_EPS)


def _silu(x):
    return x * jax.nn.sigmoid(x)


def _softplus(x):
    return jnp.maximum(x, 0.0) + jnp.log1p(jnp.exp(-jnp.abs(x)))


def _const_spec(shape):
    zeros = (0,) * len(shape)
    return pl.BlockSpec(shape, lambda *_: zeros, pipeline_mode=pl.Buffered(1))


def _layer_spec(shape, layer):
    zeros = (0,) * len(shape)
    return pl.BlockSpec((None,) + shape, lambda *_: (layer,) + zeros,
                        pipeline_mode=pl.Buffered(1))


def _params(*sem):
    return pltpu.CompilerParams(dimension_semantics=sem,
                                vmem_limit_bytes=VMEM_LIMIT_BYTES)


def _ffn_gate_up(x, gpre_ref, wg_ref, wu_ref, h_ref):
    xb = (_rms(x) * gpre_ref[...]).astype(BF16)
    for c in range(0, D_FF, FF_CHUNK):
        sl = slice(c, min(c + FF_CHUNK, D_FF))
        g = jnp.dot(xb, wg_ref[:, sl], preferred_element_type=F32)
        u = jnp.dot(xb, wu_ref[:, sl], preferred_element_type=F32)
        h_ref[:, sl] = (_silu(g) * u).astype(BF16)


def _ffn_down(x, wd_ref, gpost_ref, h_ref):
    f = jnp.dot(h_ref[...], wd_ref[...], preferred_element_type=F32)
    return x + 0.5 * (_rms(f) * gpost_ref[...])


def _sub_tiles(tile_rows):
    return [slice(r0, r0 + SUB_TILE) for r0 in range(0, tile_rows, SUB_TILE)]


_PROJ_WIDTHS = (SSD_XBC, POOL_WIDTH, SSD_INNER, 3 * ATTN_WIDTH, DT_PAD)
_PROJ_OUT_WIDTHS = (SSD_XBC, POOL_WIDTH, SSD_INNER, ATTN_WIDTH, ATTN_WIDTH,
                    ATTN_WIDTH, DT_PAD)
POOL_HALO = 2 ** POOL_GROUPS


def _pool_tile(u, tail, pos, w_ref, scale_ref):
    group = lax.broadcasted_iota(jnp.int32, u.shape, 1) // POOL_GROUP_DIM
    win_sum = jnp.concatenate([tail, u], axis=0)
    mean = jnp.zeros_like(u)
    for g in range(POOL_GROUPS):
        half = 2 ** g
        win_sum = win_sum + pltpu.roll(win_sum, half, 0)
        count = jnp.minimum(pos + 1, 2 * half).astype(F32)
        mean = jnp.where(group == g, win_sum[POOL_HALO:, :] / count, mean)
    d = (mean - u).astype(BF16)
    return jnp.dot(d, w_ref[...], preferred_element_type=F32) * scale_ref[...]


def _ffn_inproj_body(tiles_per_seq, x_ref, gpre_ref, wg_ref, wu_ref, wd_ref,
                     gpost_ref, g_ref, *rest):
    n_proj = len(_PROJ_WIDTHS)
    w_refs = rest[:n_proj]
    (poolw_ref, pscale_ref, o_ref, xbc_ref, yp_ref, z_ref, q_ref, k_ref, v_ref,
     dt_ref, h_ref, utail_ref) = rest[n_proj:]
    tile_in_seq = pl.program_id(0) % tiles_per_seq

    @pl.when(tile_in_seq == 0)
    def _():
        utail_ref[...] = jnp.zeros_like(utail_ref)

    tiles = _sub_tiles(TOKEN_TILE)

    def down(rows):
        x = _ffn_down(x_ref[rows, :], wd_ref, gpost_ref, h_ref.at[rows])
        o_ref[rows, :] = x
        return (_rms(x) * g_ref[...]).astype(BF16)

    def project(rows, xb, utail):
        def proj(i):
            return jnp.dot(xb, w_refs[i][...], preferred_element_type=F32)

        u = proj(1)
        xbc_ref[rows, :] = proj(0)
        z_ref[rows, :] = proj(2)
        qkv = proj(3)
        for i, ref in enumerate((q_ref, k_ref, v_ref)):
            ref[rows, :] = qkv[:, i * ATTN_WIDTH:(i + 1) * ATTN_WIDTH]
        dt_ref[rows, :] = proj(4)
        pos = (tile_in_seq * TOKEN_TILE + rows.start
               + lax.broadcasted_iota(jnp.int32, u.shape, 0))
        yp_ref[rows, :] = _pool_tile(u, utail, pos, poolw_ref, pscale_ref)
        return u[SUB_TILE - POOL_HALO:, :]

    utail = utail_ref[...]
    pending = None
    for rows in tiles:
        _ffn_gate_up(x_ref[rows, :], gpre_ref, wg_ref, wu_ref, h_ref.at[rows])
        xb = down(rows)
        if pending is not None:
            utail = project(*pending, utail)
        pending = (rows, xb)
    utail_ref[...] = project(*pending, utail)


def _ffn_inproj(x, gpre, wg, wu, wd, gpost, g, ws, poolw, pscale, layer, seq):
    m = x.shape[0]
    row = lambda i: (i, 0)
    widths = (D_MODEL,) + _PROJ_OUT_WIDTHS
    return pl.pallas_call(
        functools.partial(_ffn_inproj_body, seq // TOKEN_TILE),
        grid=(m // TOKEN_TILE,),
        in_specs=[
            pl.BlockSpec((TOKEN_TILE, D_MODEL), row),
            _layer_spec((1, D_MODEL), layer),
            _const_spec((D_MODEL, D_FF)),
            _const_spec((D_MODEL, D_FF)),
            _const_spec((D_FF, D_MODEL)),
            _layer_spec((1, D_MODEL), layer),
            _layer_spec((1, D_MODEL), layer),
            *[_layer_spec((D_MODEL, width), layer) for width in _PROJ_WIDTHS],
            _layer_spec((POOL_WIDTH, POOL_WIDTH), layer),
            _layer_spec((1, POOL_WIDTH), layer),
        ],
        out_specs=[pl.BlockSpec((TOKEN_TILE, width), row) for width in widths],
        out_shape=[jax.ShapeDtypeStruct((m, width), F32) for width in widths],
        scratch_shapes=[
            pltpu.VMEM((TOKEN_TILE, D_FF), BF16),
            pltpu.VMEM((POOL_HALO, POOL_WIDTH), F32),
        ],
        compiler_params=_params("arbitrary"),
        name="ffn_inproj",
    )(x, gpre, wg, wu, wd, gpost, g, *ws, poolw, pscale)


CAST_CHUNKS = 16
_FFN_W_SHAPES = ((D_MODEL, D_FF), (D_MODEL, D_FF), (D_FF, D_MODEL))


def _with_weight_cast(body, n_in, n_out, n_w):
    def wrapped(*refs):
        ins, refs = refs[:n_in], refs[n_in:]
        w_ins, refs = refs[:n_w], refs[n_w:]
        outs, refs = refs[:n_out], refs[n_out:]
        w_outs, scratch = refs[:n_w], refs[n_w:]
        body(*ins, *outs, *scratch)
        for w_in, w_out in zip(w_ins, w_outs):
            w_out[...] = w_in[...].astype(BF16)

    return wrapped


def _weight_cast_specs(layer, n_steps, step_of_grid):
    if layer is None:
        return [], [], []
    n_chunks = math.gcd(n_steps, CAST_CHUNKS)
    steps_per_chunk = n_steps // n_chunks
    chunk_of_step = lambda *g: step_of_grid(*g) // steps_per_chunk
    in_specs, out_specs, out_shapes = [], [], []
    for rows, cols in _FFN_W_SHAPES:
        chunk = rows // n_chunks
        in_specs.append(pl.BlockSpec(
            (None, chunk, cols), lambda *g: (layer, chunk_of_step(*g), 0)))
        out_specs.append(pl.BlockSpec(
            (chunk, cols), lambda *g: (chunk_of_step(*g), 0)))
        out_shapes.append(jax.ShapeDtypeStruct((rows, cols), BF16))
    return in_specs, out_specs, out_shapes


def _causal_conv4(x, tail, w_ref, b_ref):
    x_ext = jnp.concatenate([tail, x], axis=0)
    prev_ext = pltpu.roll(x_ext, 1, 0)
    near = w_ref[3:4, :] * x + w_ref[2:3, :] * prev_ext[SUBLANES:, :]
    far_ext = w_ref[1:2, :] * x_ext + w_ref[0:1, :] * prev_ext
    return near + pltpu.roll(far_ext, 2, 0)[SUBLANES:, :] + b_ref[...]


def _heads_to_lanes(xc):
    rows = xc.shape[0]
    lane = lax.broadcasted_iota(jnp.int32, (rows, LANES), 1)
    per_tile = LANES // SSD_HEAD_DIM
    parts = []
    for t in range(SSD_INNER // LANES):
        tile = jnp.broadcast_to(xc[:, t * per_tile:t * per_tile + 1], (rows, LANES))
        for i in range(1, per_tile):
            h = t * per_tile + i
            tile = jnp.where(lane < i * SSD_HEAD_DIM, tile,
                             jnp.broadcast_to(xc[:, h:h + 1], (rows, LANES)))
        parts.append(tile)
    return jnp.concatenate(parts, axis=1)


def _ssd_body(xbc_ref, z_ref, dt_ref, convw_ref, convb_ref, dtb_ref, alog_ref,
              dskip_ref, gn_ref, o_ref, tail_ref, state_ref):
    L = SSD_CHUNK
    HP = SSD_INNER
    P = SSD_HEAD_DIM
    N = SSD_STATE
    GW = HP // SSD_GROUPS
    HPG = SSD_HEADS // SSD_GROUPS

    @pl.when(pl.program_id(1) == 0)
    def _():
        tail_ref[...] = jnp.zeros_like(tail_ref)
        state_ref[...] = jnp.zeros_like(state_ref)

    r = lax.broadcasted_iota(jnp.int32, (L, L), 0)
    s = lax.broadcasted_iota(jnp.int32, (L, L), 1)
    causal = s <= r
    tril = causal.astype(F32)
    pair_lane = lax.broadcasted_iota(jnp.int32, (L, 2 * P), 1)
    neg_a = -jnp.exp(alog_ref[...])

    def chunk(c, tail):
        r0 = pl.multiple_of(c * L, L)
        rows = pl.ds(r0, L)
        x = xbc_ref[rows, :]
        xc = _silu(_causal_conv4(x, tail, convw_ref, convb_ref))
        xs = xc[:, :HP]
        bm = xc[:, HP:HP + SSD_BC].astype(BF16)
        cm = xc[:, HP + SSD_BC:].astype(BF16)

        dt_c = _softplus(dt_ref[rows, :] + dtb_ref[...])
        acs_c = jnp.dot(tril, dt_c * neg_a, precision=HIGHEST,
                        preferred_element_type=F32)
        acs_t = acs_c.T
        dt = _heads_to_lanes(dt_c)
        acs = _heads_to_lanes(acs_c)
        a_last = acs[L - 1:L, :]
        decay_out = jnp.exp(acs)
        decay_in = jnp.exp(a_last - acs)
        chunk_decay = jnp.exp(a_last)

        xdt = xs * dt
        xdt_b = xdt.astype(BF16)
        xw_b = (xdt * decay_in).astype(BF16)

        y_parts = []
        for g in range(SSD_GROUPS):
            bg = bm[:, g * N:(g + 1) * N]
            cg = cm[:, g * N:(g + 1) * N]
            gsl = slice(g * GW, (g + 1) * GW)
            cb = lax.dot_general(cg, bg, (((1,), (1,)), ((), ())),
                                 preferred_element_type=F32)
            st = state_ref[:, gsl]
            y_off = jnp.dot(cg, st.astype(BF16), preferred_element_type=F32)
            new_st = lax.dot_general(bg, xw_b[:, gsl], (((0,), (0,)), ((), ())),
                                     preferred_element_type=F32)
            state_ref[:, gsl] = st * chunk_decay[:, gsl] + new_st
            for pair in range(HPG // 2):
                lo = g * GW + pair * 2 * P
                x_pair = xdt_b[:, lo:lo + 2 * P]
                ys = []
                for i in range(2):
                    h = lo // P + i
                    seg = jnp.exp(jnp.where(
                        causal, acs_c[:, h:h + 1] - acs_t[h:h + 1, :], -jnp.inf))
                    ys.append(jnp.dot((cb * seg).astype(BF16), x_pair,
                                      preferred_element_type=F32))
                y_diag = jnp.where(pair_lane < P, ys[0], ys[1])
                psl = slice(pair * 2 * P, (pair + 1) * 2 * P)
                y_parts.append(
                    y_diag + y_off[:, psl] * decay_out[:, lo:lo + 2 * P])
        y = jnp.concatenate(y_parts, axis=-1) + dskip_ref[...] * xs
        o_ref[rows, :] = _rms(y * _silu(z_ref[rows, :])) * gn_ref[...]
        return x[L - SUBLANES:, :]

    n_chunks = xbc_ref.shape[0] // L
    tail_ref[...] = lax.fori_loop(0, n_chunks, chunk, tail_ref[...], unroll=True)


def _ssd(xbc, z, dt, convw, convb, dtb, alog, dskip, gn, cast_ws, cast_layer,
         batch, seq):
    nblk = seq // SSD_STEP_ROWS
    row = lambda b, c: (b * nblk + c, 0)
    w_in_specs, w_out_specs, w_out_shapes = _weight_cast_specs(
        cast_layer, batch * nblk, lambda b, c: b * nblk + c)
    return pl.pallas_call(
        _with_weight_cast(_ssd_body, 9, 1, len(w_in_specs)),
        grid=(batch, nblk),
        in_specs=[
            pl.BlockSpec((SSD_STEP_ROWS, SSD_XBC), row),
            pl.BlockSpec((SSD_STEP_ROWS, SSD_INNER), row),
            pl.BlockSpec((SSD_STEP_ROWS, DT_PAD), row),
            _const_spec((SSD_CONV, SSD_XBC)),
            _const_spec((1, SSD_XBC)),
            _const_spec((1, DT_PAD)),
            _const_spec((1, DT_PAD)),
            _const_spec((1, SSD_INNER)),
            _const_spec((1, SSD_INNER)),
            *w_in_specs,
        ],
        out_specs=[pl.BlockSpec((SSD_STEP_ROWS, SSD_INNER), row), *w_out_specs],
        out_shape=[jax.ShapeDtypeStruct((batch * seq, SSD_INNER), F32),
                   *w_out_shapes],
        scratch_shapes=[
            pltpu.VMEM((SUBLANES, SSD_XBC), F32),
            pltpu.VMEM((SSD_STATE, SSD_INNER), F32),
        ],
        compiler_params=_params("arbitrary", "arbitrary"),
        name="ssd",
    )(xbc, z, dt, convw, convb, dtb, alog, dskip, gn, *cast_ws)


def _rope(t, cos, sin_up, sin_dn):
    half = ROPE_DIM // 2
    return (t * cos + pltpu.roll(t, half, 1) * sin_up
            + pltpu.roll(t, ATTN_WIDTH - half, 1) * sin_dn)


def _moba_body(q_ref, k_ref, v_ref, cos_ref, sup_ref, sdn_ref, o_ref,
               ks_ref, vt_ref, kmean_ref, sel_ref, s_ref):
    seq = k_ref.shape[0]
    nb = seq // MOBA_BLOCK
    Dh = ATTN_HEAD_DIM
    BL = MOBA_BLOCK
    H = ATTN_HEADS
    nt = (((1,), (1,)), ((), ()))
    zero = pl.program_id(1)

    head_of_lane = lax.broadcasted_iota(jnp.int32, (1, ATTN_WIDTH), 1) // Dh

    def stage_keys(j):
        rows = slice(j * BL, (j + 1) * BL)
        kj = _rope(k_ref[rows, :], cos_ref[rows, :], sup_ref[rows, :],
                   sdn_ref[rows, :])
        kmean = jnp.mean(kj, axis=0, keepdims=True)
        for h in range(H):
            kmean_ref[h * nb + j:h * nb + j + 1, :] = jnp.where(
                head_of_lane == h, kmean, 0.0)
        kb = kj.astype(BF16)
        for h in range(H):
            ks_ref[h, rows, :] = kb[:, h * Dh:(h + 1) * Dh]
        vt = v_ref[rows, :].T.astype(BF16)
        for h in range(H):
            vt_ref[h, 0:Dh, rows] = vt[h * Dh:(h + 1) * Dh, :]

    vt_ref[:, Dh:, :] = jnp.ones((H, BF16_SUBLANES, seq), BF16)
    kmean_ref[...] = jnp.zeros_like(kmean_ref)

    def over_blocks(fn, n, carry):
        if n == 0:
            return carry

        def body(_, carry):
            for j in range(n):
                carry = fn(j, carry)
            return carry

        return lax.fori_loop(0, zero + 1, body, carry)

    def attend(i, rows):
        qf = _rope(q_ref[rows, :], cos_ref[rows, :], sup_ref[rows, :],
                   sdn_ref[rows, :])
        qb = (qf * (Dh ** -0.5)).astype(BF16)
        blk = lax.broadcasted_iota(jnp.int32, (nb, BL), 0)
        past = blk < i
        gates = lax.dot_general(kmean_ref[...], qf, nt, precision=HIGHEST,
                                preferred_element_type=F32)
        for h in range(H):
            gate = jnp.where(past, gates[h * nb:(h + 1) * nb, :], -jnp.inf)
            rank = jnp.zeros((nb, BL), jnp.int32)
            for j2 in range(nb):
                g2 = gate[j2:j2 + 1, :]
                ahead = (g2 > gate) | ((g2 == gate) & (j2 < blk))
                rank = rank + ahead.astype(jnp.int32)
            sel_ref[h] = (past & (rank < MOBA_TOPK)).astype(F32)

        qh = [qb[:, h * Dh:(h + 1) * Dh] for h in range(H)]

        def masked_scores(h, j, keep):
            st = lax.dot_general(ks_ref[h, j * BL:(j + 1) * BL, :], qh[h], nt,
                                 preferred_element_type=F32)
            st = jnp.where(keep, st * LOG2E, -jnp.inf)
            s_ref[h, j] = st
            return jnp.max(st, axis=0, keepdims=True)

        own_mask = (lax.broadcasted_iota(jnp.int32, (BL, BL), 0) <=
                    lax.broadcasted_iota(jnp.int32, (BL, BL), 1))
        m_own = tuple(masked_scores(h, i, own_mask) for h in range(H))

        def pass1(j, ms):
            return tuple(
                jnp.maximum(ms[h], masked_scores(
                    h, j, sel_ref[h, j:j + 1, :] > 0.5)) for h in range(H))

        ms = over_blocks(pass1, i, m_own)

        def pass2(j, accs):
            return tuple(
                accs[h] + jnp.dot(
                    vt_ref[h, :, j * BL:(j + 1) * BL],
                    jnp.exp2(s_ref[h, j] - ms[h]).astype(BF16),
                    preferred_element_type=F32)
                for h in range(H))

        init = (jnp.zeros((Dh + BF16_SUBLANES, BL), F32),) * H
        accs = over_blocks(pass2, i + 1, init)
        out_t = jnp.concatenate(
            [acc[:Dh] / acc[Dh:Dh + 1] for acc in accs], axis=0)
        o_ref[rows, :] = out_t.T

    for j in range(nb):
        stage_keys(j)
        attend(j, slice(j * BL, (j + 1) * BL))


def _moba(q, k, v, tables, cast_ws, cast_layer, batch, seq):
    nb = seq // MOBA_BLOCK
    whole = pl.BlockSpec((seq, ATTN_WIDTH), lambda b, _: (b, 0))
    table = _const_spec((seq, ATTN_WIDTH))
    w_in_specs, w_out_specs, w_out_shapes = _weight_cast_specs(
        cast_layer, batch, lambda b, _: b)
    return pl.pallas_call(
        _with_weight_cast(_moba_body, 6, 1, len(w_in_specs)),
        grid=(batch, 1),
        in_specs=[whole, whole, whole, table, table, table, *w_in_specs],
        out_specs=[whole, *w_out_specs],
        out_shape=[jax.ShapeDtypeStruct((batch * seq, ATTN_WIDTH), F32),
                   *w_out_shapes],
        scratch_shapes=[
            pltpu.VMEM((ATTN_HEADS, seq, ATTN_HEAD_DIM), BF16),
            pltpu.VMEM((ATTN_HEADS, ATTN_HEAD_DIM + BF16_SUBLANES, seq), BF16),
            pltpu.VMEM((ATTN_HEADS * nb, ATTN_WIDTH), F32),
            pltpu.VMEM((ATTN_HEADS, nb, MOBA_BLOCK), F32),
            pltpu.VMEM((ATTN_HEADS, nb, MOBA_BLOCK, MOBA_BLOCK), F32),
        ],
        compiler_params=_params("arbitrary", "arbitrary"),
        name="moba",
    )(q, k, v, *tables, *(cast_ws if w_in_specs else ()))


def _outproj_ffn_body(x_ref, ys_ref, yp_ref, ya_ref, w_ref, g_ref,
                      gpre_ref, wg_ref, wu_ref, wd_ref, gpost_ref, o_ref, h_ref):
    o1 = SSD_INNER
    o2 = SSD_INNER + POOL_WIDTH
    tiles = _sub_tiles(OUT_TOKEN_TILE)
    xs = []
    for rows in tiles:
        m = jnp.dot(ys_ref[rows, :].astype(BF16), w_ref[0:o1, :],
                    preferred_element_type=F32)
        m = m + jnp.dot(yp_ref[rows, :].astype(BF16), w_ref[o1:o2, :],
                        preferred_element_type=F32)
        m = m + jnp.dot(ya_ref[rows, :].astype(BF16), w_ref[o2:, :],
                        preferred_element_type=F32)
        xs.append(x_ref[rows, :] + _rms(m) * g_ref[...])
    for rows, x in zip(tiles, xs):
        _ffn_gate_up(x, gpre_ref, wg_ref, wu_ref, h_ref.at[rows])
        o_ref[rows, :] = _ffn_down(x, wd_ref, gpost_ref, h_ref.at[rows])


def _outproj_ffn(x, y_ssd, y_pool, y_attn, w, g, gpre, wg, wu, wd, gpost, layer):
    m = x.shape[0]
    row = lambda i: (i, 0)
    return pl.pallas_call(
        _outproj_ffn_body,
        grid=(m // OUT_TOKEN_TILE,),
        in_specs=[
            pl.BlockSpec((OUT_TOKEN_TILE, D_MODEL), row),
            pl.BlockSpec((OUT_TOKEN_TILE, SSD_INNER), row),
            pl.BlockSpec((OUT_TOKEN_TILE, POOL_WIDTH), row),
            pl.BlockSpec((OUT_TOKEN_TILE, ATTN_WIDTH), row),
            _layer_spec((D_MODEL, D_MODEL), layer),
            _layer_spec((1, D_MODEL), layer),
            _layer_spec((1, D_MODEL), layer),
            _const_spec((D_MODEL, D_FF)),
            _const_spec((D_MODEL, D_FF)),
            _const_spec((D_FF, D_MODEL)),
            _layer_spec((1, D_MODEL), layer),
        ],
        out_specs=pl.BlockSpec((OUT_TOKEN_TILE, D_MODEL), row),
        out_shape=jax.ShapeDtypeStruct((m, D_MODEL), F32),
        scratch_shapes=[pltpu.VMEM((OUT_TOKEN_TILE, D_FF), BF16)],
        compiler_params=_params("parallel"),
        name="outproj_ffn",
    )(x, y_ssd, y_pool, y_attn, w, g, gpre, wg, wu, wd, gpost)


def _rope_tables(seq):
    half = ROPE_DIM // 2
    inv_freq = ROPE_THETA ** (-jnp.arange(0, ROPE_DIM, 2, dtype=F32) / ROPE_DIM)
    ang = jnp.arange(seq, dtype=F32)[:, None] * inv_freq[None, :]
    rest = ATTN_HEAD_DIM - ROPE_DIM
    one = jnp.ones((seq, rest), F32)
    zero = jnp.zeros((seq, rest), F32)
    zh = jnp.zeros((seq, half), F32)
    cos = jnp.concatenate([jnp.cos(ang), jnp.cos(ang), one], axis=-1)
    sin_up = jnp.concatenate([zh, jnp.sin(ang), zero], axis=-1)
    sin_dn = jnp.concatenate([-jnp.sin(ang), zh, zero], axis=-1)
    return tuple(jnp.tile(t, (1, ATTN_HEADS)) for t in (cos, sin_up, sin_dn))


def _split_w_in(w):
    pad = [(0, 0)] * (w.ndim - 1) + [(0, DT_PAD - SSD_HEADS)]
    pieces = [w[..., OFF_XBC:OFF_DT], w[..., OFF_POOL:OFF_Q], w[..., OFF_Z:OFF_XBC],
              w[..., OFF_Q:IN_COLS], jnp.pad(w[..., OFF_DT:OFF_POOL], pad)]
    return tuple(p.astype(BF16) for p in pieces)


def _per_head_lanes(p):
    return jnp.repeat(p, SSD_HEAD_DIM)[None, :]


def _per_head_cols(p):
    pad = [(0, 0)] * (p.ndim - 1) + [(0, DT_PAD - SSD_HEADS)]
    return jnp.pad(p, pad)[..., None, :]


def _pool_blockdiag(w):
    out = jnp.zeros(w.shape[:-3] + (POOL_WIDTH, POOL_WIDTH), F32)
    for g in range(POOL_GROUPS):
        sl = slice(g * POOL_GROUP_DIM, (g + 1) * POOL_GROUP_DIM)
        out = out.at[..., sl, sl].set(w[..., g, :, :])
    return out.astype(BF16)


def kernel(x, ff1_norm_pre, ff1_w_gate, ff1_w_up, ff1_w_down, ff1_norm_post,
           mix_norm_pre, w_in, conv_w, conv_b, dt_bias, a_log, d_skip, ssd_norm,
           pool_w, pool_scale, w_out, mix_norm_post,
           ff2_norm_pre, ff2_w_gate, ff2_w_up, ff2_w_down, ff2_norm_post):
    batch, seq, d = x.shape
    depth = w_in.shape[0]
    h = x.reshape(batch * seq, d)
    rope_tables = _rope_tables(seq)
    row = lambda p: p[None, :]
    stack_row = lambda p: p[:, None, :]
    bf = lambda w: w.astype(BF16)
    ff1_w = (ff1_w_gate, ff1_w_up, ff1_w_down)
    ff2_w = (ff2_w_gate, ff2_w_up, ff2_w_down)
    g_ff1 = (stack_row(ff1_norm_pre), stack_row(ff1_norm_post))
    g_ff2 = (stack_row(ff2_norm_pre), stack_row(ff2_norm_post))
    w_in_b = _split_w_in(w_in)
    w_out_b = bf(w_out)
    g_mix_pre = stack_row(mix_norm_pre)
    g_mix_post = stack_row(mix_norm_post)
    mix_front = (g_mix_pre, w_in_b, _pool_blockdiag(pool_w), stack_row(pool_scale))
    ff1_b = tuple(bf(w[0]) for w in ff1_w)
    for l in range(depth):
        h, xbc, y_pool, z, q, k, v, dt = _ffn_inproj(
            h, g_ff1[0], *ff1_b, g_ff1[1], *mix_front, l, seq)
        y_ssd, *ff2_b = _ssd(xbc, z, dt, conv_w[l], row(conv_b[l]),
                             _per_head_cols(dt_bias[l]), _per_head_cols(a_log[l]),
                             _per_head_lanes(d_skip[l]), row(ssd_norm[l]),
                             ff2_w, l, batch, seq)
        nxt = l + 1 if l + 1 < depth else None
        y_attn, *ff1_b = _moba(q, k, v, rope_tables, ff1_w, nxt, batch, seq)
        h = _outproj_ffn(h, y_ssd, y_pool, y_attn, w_out_b, g_mix_post,
                         g_ff2[0], *ff2_b, g_ff2[1], l)
    return h.reshape(batch, seq, d)
```

```python
import functools
import math

import jax
import jax.numpy as jnp
from jax import lax
from jax.experimental import pallas as pl
from jax.experimental.pallas import tpu as pltpu

F32 = jnp.float32
BF16 = jnp.bfloat16
HIGHEST = lax.Precision.HIGHEST

D_MODEL = 1024
D_FF = 2816
SSD_INNER = 512
SSD_HEAD_DIM = 64
SSD_HEADS = SSD_INNER // SSD_HEAD_DIM
SSD_GROUPS = 2
SSD_STATE = 128
SSD_CONV = 4
SSD_CHUNK = 128
SSD_BC = SSD_GROUPS * SSD_STATE
SSD_XBC = SSD_INNER + 2 * SSD_BC
POOL_WIDTH = 256
POOL_GROUPS = 4
POOL_GROUP_DIM = POOL_WIDTH // POOL_GROUPS
ATTN_WIDTH = 256
ATTN_HEAD_DIM = 64
ATTN_HEADS = ATTN_WIDTH // ATTN_HEAD_DIM
ROPE_DIM = ATTN_HEAD_DIM // 4
ROPE_THETA = 500000.0
MOBA_BLOCK = 256
MOBA_TOPK = 3
RMS_EPS = 1e-6

OFF_Z = 0
OFF_XBC = OFF_Z + SSD_INNER
OFF_DT = OFF_XBC + SSD_XBC
OFF_POOL = OFF_DT + SSD_HEADS
OFF_Q = OFF_POOL + POOL_WIDTH
OFF_K = OFF_Q + ATTN_WIDTH
OFF_V = OFF_K + ATTN_WIDTH
IN_COLS = OFF_V + ATTN_WIDTH

LANES = 128
SUBLANES = 8
MXU_DIM = 256
VMEM_LIMIT_BYTES = 56 * 1024 * 1024

DT_PAD = LANES
TOKEN_TILE = 512
OUT_TOKEN_TILE = 1024
SUB_TILE = 256
FF_CHUNK = MXU_DIM
SSD_STEP_ROWS = 8 * SSD_CHUNK
BF16_SUBLANES = 16
LOG2E = math.log2(math.e)


def _rms(x):
    return x * lax.rsqrt(jnp.mean(x * x, axis=-1, keepdims=True) + RMS_EPS)


def _silu(x):
    return x * jax.nn.sigmoid(x)


def _softplus(x):
    return jnp.maximum(x, 0.0) + jnp.log1p(jnp.exp(-jnp.abs(x)))


def _const_spec(shape):
    zeros = (0,) * len(shape)
    return pl.BlockSpec(shape, lambda *_: zeros, pipeline_mode=pl.Buffered(1))


def _layer_spec(shape, layer):
    zeros = (0,) * len(shape)
    return pl.BlockSpec((None,) + shape, lambda *_: (layer,) + zeros,
                        pipeline_mode=pl.Buffered(1))


def _params(*sem):
    return pltpu.CompilerParams(dimension_semantics=sem,
                                vmem_limit_bytes=VMEM_LIMIT_BYTES)


def _ffn_gate_up(x, gpre_ref, wg_ref, wu_ref, h_ref):
    xb = (_rms(x) * gpre_ref[...]).astype(BF16)
    for c in range(0, D_FF, FF_CHUNK):
        sl = slice(c, min(c + FF_CHUNK, D_FF))
        g = jnp.dot(xb, wg_ref[:, sl], preferred_element_type=F32)
        u = jnp.dot(xb, wu_ref[:, sl], preferred_element_type=F32)
        h_ref[:, sl] = (_silu(g) * u).astype(BF16)


def _ffn_down(x, wd_ref, gpost_ref, h_ref):
    f = jnp.dot(h_ref[...], wd_ref[...], preferred_element_type=F32)
    return x + 0.5 * (_rms(f) * gpost_ref[...])


def _sub_tiles(tile_rows):
    return [slice(r0, r0 + SUB_TILE) for r0 in range(0, tile_rows, SUB_TILE)]


_PROJ_WIDTHS = (SSD_XBC, POOL_WIDTH, SSD_INNER, 3 * ATTN_WIDTH, DT_PAD)
_PROJ_OUT_WIDTHS = (SSD_XBC, POOL_WIDTH, SSD_INNER, ATTN_WIDTH, ATTN_WIDTH,
                    ATTN_WIDTH, DT_PAD)
POOL_HALO = 2 ** POOL_GROUPS


def _pool_tile(u, tail, pos, w_ref, scale_ref):
    group = lax.broadcasted_iota(jnp.int32, u.shape, 1) // POOL_GROUP_DIM
    win_sum = jnp.concatenate([tail, u], axis=0)
    mean = jnp.zeros_like(u)
    for g in range(POOL_GROUPS):
        half = 2 ** g
        win_sum = win_sum + pltpu.roll(win_sum, half, 0)
        count = jnp.minimum(pos + 1, 2 * half).astype(F32)
        mean = jnp.where(group == g, win_sum[POOL_HALO:, :] / count, mean)
    d = (mean - u).astype(BF16)
    return jnp.dot(d, w_ref[...], preferred_element_type=F32) * scale_ref[...]


def _ffn_inproj_body(tiles_per_seq, x_ref, gpre_ref, wg_ref, wu_ref, wd_ref,
                     gpost_ref, g_ref, *rest):
    n_proj = len(_PROJ_WIDTHS)
    w_refs = rest[:n_proj]
    (poolw_ref, pscale_ref, o_ref, xbc_ref, yp_ref, z_ref, q_ref, k_ref, v_ref,
     dt_ref, h_ref, utail_ref) = rest[n_proj:]
    tile_in_seq = pl.program_id(0) % tiles_per_seq

    @pl.when(tile_in_seq == 0)
    def _():
        utail_ref[...] = jnp.zeros_like(utail_ref)

    tiles = _sub_tiles(TOKEN_TILE)

    def down(rows):
        x = _ffn_down(x_ref[rows, :], wd_ref, gpost_ref, h_ref.at[rows])
        o_ref[rows, :] = x
        return (_rms(x) * g_ref[...]).astype(BF16)

    def project(rows, xb, utail):
        def proj(i):
            return jnp.dot(xb, w_refs[i][...], preferred_element_type=F32)

        u = proj(1)
        xbc_ref[rows, :] = proj(0)
        z_ref[rows, :] = proj(2)
        qkv = proj(3)
        for i, ref in enumerate((q_ref, k_ref, v_ref)):
            ref[rows, :] = qkv[:, i * ATTN_WIDTH:(i + 1) * ATTN_WIDTH]
        dt_ref[rows, :] = proj(4)
        pos = (tile_in_seq * TOKEN_TILE + rows.start
               + lax.broadcasted_iota(jnp.int32, u.shape, 0))
        yp_ref[rows, :] = _pool_tile(
            u, utail, pos, poolw_ref, pscale_ref).astype(yp_ref.dtype)
        return u[SUB_TILE - POOL_HALO:, :]

    utail = utail_ref[...]
    pending = None
    for rows in tiles:
        _ffn_gate_up(x_ref[rows, :], gpre_ref, wg_ref, wu_ref, h_ref.at[rows])
        xb = down(rows)
        if pending is not None:
            utail = project(*pending, utail)
        pending = (rows, xb)
    utail_ref[...] = project(*pending, utail)


def _ffn_inproj(x, gpre, wg, wu, wd, gpost, g, ws, poolw, pscale, layer, seq):
    m = x.shape[0]
    row = lambda i: (i, 0)
    widths = (D_MODEL,) + _PROJ_OUT_WIDTHS
    return pl.pallas_call(
        functools.partial(_ffn_inproj_body, seq // TOKEN_TILE),
        grid=(m // TOKEN_TILE,),
        in_specs=[
            pl.BlockSpec((TOKEN_TILE, D_MODEL), row),
            _layer_spec((1, D_MODEL), layer),
            _const_spec((D_MODEL, D_FF)),
            _const_spec((D_MODEL, D_FF)),
            _const_spec((D_FF, D_MODEL)),
            _layer_spec((1, D_MODEL), layer),
            _layer_spec((1, D_MODEL), layer),
            *[_layer_spec((D_MODEL, width), layer) for width in _PROJ_WIDTHS],
            _layer_spec((POOL_WIDTH, POOL_WIDTH), layer),
            _layer_spec((1, POOL_WIDTH), layer),
        ],
        out_specs=[pl.BlockSpec((TOKEN_TILE, width), row) for width in widths],
        out_shape=[jax.ShapeDtypeStruct((m, width), BF16 if i == 2 else F32)
                   for i, width in enumerate(widths)],
        scratch_shapes=[
            pltpu.VMEM((TOKEN_TILE, D_FF), BF16),
            pltpu.VMEM((POOL_HALO, POOL_WIDTH), F32),
        ],
        compiler_params=_params("arbitrary"),
        name="ffn_inproj",
    )(x, gpre, wg, wu, wd, gpost, g, *ws, poolw, pscale)


CAST_CHUNKS = 16
_FFN_W_SHAPES = ((D_MODEL, D_FF), (D_MODEL, D_FF), (D_FF, D_MODEL))


def _with_weight_cast(body, n_in, n_out, n_w):
    def wrapped(*refs):
        ins, refs = refs[:n_in], refs[n_in:]
        w_ins, refs = refs[:n_w], refs[n_w:]
        outs, refs = refs[:n_out], refs[n_out:]
        w_outs, scratch = refs[:n_w], refs[n_w:]
        body(*ins, *outs, *scratch)
        for w_in, w_out in zip(w_ins, w_outs):
            w_out[...] = w_in[...].astype(BF16)

    return wrapped


def _weight_cast_specs(layer, n_steps, step_of_grid):
    if layer is None:
        return [], [], []
    n_chunks = math.gcd(n_steps, CAST_CHUNKS)
    steps_per_chunk = n_steps // n_chunks
    chunk_of_step = lambda *g: step_of_grid(*g) // steps_per_chunk
    in_specs, out_specs, out_shapes = [], [], []
    for rows, cols in _FFN_W_SHAPES:
        chunk = rows // n_chunks
        in_specs.append(pl.BlockSpec(
            (None, chunk, cols), lambda *g: (layer, chunk_of_step(*g), 0)))
        out_specs.append(pl.BlockSpec(
            (chunk, cols), lambda *g: (chunk_of_step(*g), 0)))
        out_shapes.append(jax.ShapeDtypeStruct((rows, cols), BF16))
    return in_specs, out_specs, out_shapes


def _causal_conv4(x, tail, w_ref, b_ref):
    x_ext = jnp.concatenate([tail, x], axis=0)
    prev_ext = pltpu.roll(x_ext, 1, 0)
    near = w_ref[3:4, :] * x + w_ref[2:3, :] * prev_ext[SUBLANES:, :]
    far_ext = w_ref[1:2, :] * x_ext + w_ref[0:1, :] * prev_ext
    return near + pltpu.roll(far_ext, 2, 0)[SUBLANES:, :] + b_ref[...]


def _heads_to_lanes(xc):
    rows = xc.shape[0]
    lane = lax.broadcasted_iota(jnp.int32, (rows, LANES), 1)
    per_tile = LANES // SSD_HEAD_DIM
    parts = []
    for t in range(SSD_INNER // LANES):
        tile = jnp.broadcast_to(xc[:, t * per_tile:t * per_tile + 1], (rows, LANES))
        for i in range(1, per_tile):
            h = t * per_tile + i
            tile = jnp.where(lane < i * SSD_HEAD_DIM, tile,
                             jnp.broadcast_to(xc[:, h:h + 1], (rows, LANES)))
        parts.append(tile)
    return jnp.concatenate(parts, axis=1)


def _ssd_body(xbc_ref, z_ref, dt_ref, convw_ref, convb_ref, dtb_ref, alog_ref,
              dskip_ref, gn_ref, o_ref, tail_ref, state_ref):
    L = SSD_CHUNK
    HP = SSD_INNER
    P = SSD_HEAD_DIM
    N = SSD_STATE
    GW = HP // SSD_GROUPS
    HPG = SSD_HEADS // SSD_GROUPS

    @pl.when(pl.program_id(1) == 0)
    def _():
        tail_ref[...] = jnp.zeros_like(tail_ref)
        state_ref[...] = jnp.zeros_like(state_ref)

    r = lax.broadcasted_iota(jnp.int32, (L, L), 0)
    s = lax.broadcasted_iota(jnp.int32, (L, L), 1)
    causal = s <= r
    tril = causal.astype(F32)
    pair_lane = lax.broadcasted_iota(jnp.int32, (L, 2 * P), 1)
    neg_a = -jnp.exp(alog_ref[...])

    def chunk(c, tail):
        r0 = pl.multiple_of(c * L, L)
        rows = pl.ds(r0, L)
        x = xbc_ref[rows, :]
        xc = _silu(_causal_conv4(x, tail, convw_ref, convb_ref))
        xs = xc[:, :HP]
        bm = xc[:, HP:HP + SSD_BC].astype(BF16)
        cm = xc[:, HP + SSD_BC:].astype(BF16)

        dt_c = _softplus(dt_ref[rows, :] + dtb_ref[...])
        acs_c = jnp.dot(tril, dt_c * neg_a, precision=HIGHEST,
                        preferred_element_type=F32)
        acs_t = acs_c.T
        dt = _heads_to_lanes(dt_c)
        acs = _heads_to_lanes(acs_c)
        a_last = acs[L - 1:L, :]
        decay_out = jnp.exp(acs)
        decay_in = jnp.exp(a_last - acs)
        chunk_decay = jnp.exp(a_last)

        xdt = xs * dt
        xdt_b = xdt.astype(BF16)
        xw_b = (xdt * decay_in).astype(BF16)

        y_parts = []
        for g in range(SSD_GROUPS):
            bg = bm[:, g * N:(g + 1) * N]
            cg = cm[:, g * N:(g + 1) * N]
            gsl = slice(g * GW, (g + 1) * GW)
            cb = lax.dot_general(cg, bg, (((1,), (1,)), ((), ())),
                                 preferred_element_type=F32)
            st = state_ref[:, gsl]
            y_off = jnp.dot(cg, st.astype(BF16), preferred_element_type=F32)
            new_st = lax.dot_general(bg, xw_b[:, gsl], (((0,), (0,)), ((), ())),
                                     preferred_element_type=F32)
            state_ref[:, gsl] = st * chunk_decay[:, gsl] + new_st
            for pair in range(HPG // 2):
                lo = g * GW + pair * 2 * P
                x_pair = xdt_b[:, lo:lo + 2 * P]
                ys = []
                for i in range(2):
                    h = lo // P + i
                    seg = jnp.exp(jnp.where(
                        causal, acs_c[:, h:h + 1] - acs_t[h:h + 1, :], -jnp.inf))
                    ys.append(jnp.dot((cb * seg).astype(BF16), x_pair,
                                      preferred_element_type=F32))
                y_diag = jnp.where(pair_lane < P, ys[0], ys[1])
                psl = slice(pair * 2 * P, (pair + 1) * 2 * P)
                y_parts.append(
                    y_diag + y_off[:, psl] * decay_out[:, lo:lo + 2 * P])
        y = jnp.concatenate(y_parts, axis=-1) + dskip_ref[...] * xs
        o_ref[rows, :] = (_rms(y * _silu(z_ref[rows, :])) *
                          gn_ref[...]).astype(o_ref.dtype)
        return x[L - SUBLANES:, :]

    n_chunks = xbc_ref.shape[0] // L
    tail_ref[...] = lax.fori_loop(0, n_chunks, chunk, tail_ref[...], unroll=True)


def _ssd(xbc, z, dt, convw, convb, dtb, alog, dskip, gn, cast_ws, cast_layer,
         batch, seq):
    nblk = seq // SSD_STEP_ROWS
    row = lambda b, c: (b * nblk + c, 0)
    w_in_specs, w_out_specs, w_out_shapes = _weight_cast_specs(
        cast_layer, batch * nblk, lambda b, c: b * nblk + c)
    return pl.pallas_call(
        _with_weight_cast(_ssd_body, 9, 1, len(w_in_specs)),
        grid=(batch, nblk),
        in_specs=[
            pl.BlockSpec((SSD_STEP_ROWS, SSD_XBC), row),
            pl.BlockSpec((SSD_STEP_ROWS, SSD_INNER), row),
            pl.BlockSpec((SSD_STEP_ROWS, DT_PAD), row),
            _const_spec((SSD_CONV, SSD_XBC)),
            _const_spec((1, SSD_XBC)),
            _const_spec((1, DT_PAD)),
            _const_spec((1, DT_PAD)),
            _const_spec((1, SSD_INNER)),
            _const_spec((1, SSD_INNER)),
            *w_in_specs,
        ],
        out_specs=[pl.BlockSpec((SSD_STEP_ROWS, SSD_INNER), row), *w_out_specs],
        out_shape=[jax.ShapeDtypeStruct((batch * seq, SSD_INNER), BF16),
                   *w_out_shapes],
        scratch_shapes=[
            pltpu.VMEM((SUBLANES, SSD_XBC), F32),
            pltpu.VMEM((SSD_STATE, SSD_INNER), F32),
        ],
        compiler_params=_params("arbitrary", "arbitrary"),
        name="ssd",
    )(xbc, z, dt, convw, convb, dtb, alog, dskip, gn, *cast_ws)


def _rope(t, cos, sin_up, sin_dn):
    half = ROPE_DIM // 2
    return (t * cos + pltpu.roll(t, half, 1) * sin_up
            + pltpu.roll(t, ATTN_WIDTH - half, 1) * sin_dn)


def _moba_body(q_ref, k_ref, v_ref, cos_ref, sup_ref, sdn_ref, o_ref,
               ks_ref, vt_ref, kmean_ref, sel_ref, s_ref):
    seq = k_ref.shape[0]
    nb = seq // MOBA_BLOCK
    Dh = ATTN_HEAD_DIM
    BL = MOBA_BLOCK
    H = ATTN_HEADS
    nt = (((1,), (1,)), ((), ()))
    zero = pl.program_id(1)

    head_of_lane = lax.broadcasted_iota(jnp.int32, (1, ATTN_WIDTH), 1) // Dh

    def stage_keys(j):
        rows = slice(j * BL, (j + 1) * BL)
        kj = _rope(k_ref[rows, :], cos_ref[rows, :], sup_ref[rows, :],
                   sdn_ref[rows, :])
        kmean = jnp.mean(kj, axis=0, keepdims=True)
        for h in range(H):
            kmean_ref[h * nb + j:h * nb + j + 1, :] = jnp.where(
                head_of_lane == h, kmean, 0.0)
        kb = kj.astype(BF16)
        for h in range(H):
            ks_ref[h, rows, :] = kb[:, h * Dh:(h + 1) * Dh]
        vt = v_ref[rows, :].T.astype(BF16)
        for h in range(H):
            vt_ref[h, 0:Dh, rows] = vt[h * Dh:(h + 1) * Dh, :]

    vt_ref[:, Dh:, :] = jnp.ones((H, BF16_SUBLANES, seq), BF16)
    kmean_ref[...] = jnp.zeros_like(kmean_ref)

    def over_blocks(fn, n, carry):
        if n == 0:
            return carry

        def body(_, carry):
            for j in range(n):
                carry = fn(j, carry)
            return carry

        return lax.fori_loop(0, zero + 1, body, carry)

    def attend(i, rows):
        qf = _rope(q_ref[rows, :], cos_ref[rows, :], sup_ref[rows, :],
                   sdn_ref[rows, :])
        qb = (qf * (Dh ** -0.5)).astype(BF16)
        blk = lax.broadcasted_iota(jnp.int32, (nb, BL), 0)
        past = blk < i
        gates = lax.dot_general(kmean_ref[...], qf, nt, precision=HIGHEST,
                                preferred_element_type=F32)
        for h in range(H):
            gate = jnp.where(past, gates[h * nb:(h + 1) * nb, :], -jnp.inf)
            rank = jnp.zeros((nb, BL), jnp.int32)
            for j2 in range(nb):
                g2 = gate[j2:j2 + 1, :]
                ahead = (g2 > gate) | ((g2 == gate) & (j2 < blk))
                rank = rank + ahead.astype(jnp.int32)
            sel_ref[h] = (past & (rank < MOBA_TOPK)).astype(F32)

        qh = [qb[:, h * Dh:(h + 1) * Dh] for h in range(H)]

        def masked_scores(h, j, keep):
            st = lax.dot_general(ks_ref[h, j * BL:(j + 1) * BL, :], qh[h], nt,
                                 preferred_element_type=F32)
            st = jnp.where(keep, st * LOG2E, -jnp.inf)
            s_ref[h, j] = st
            return jnp.max(st, axis=0, keepdims=True)

        own_mask = (lax.broadcasted_iota(jnp.int32, (BL, BL), 0) <=
                    lax.broadcasted_iota(jnp.int32, (BL, BL), 1))
        m_own = tuple(masked_scores(h, i, own_mask) for h in range(H))

        def pass1(j, ms):
            return tuple(
                jnp.maximum(ms[h], masked_scores(
                    h, j, sel_ref[h, j:j + 1, :] > 0.5)) for h in range(H))

        ms = over_blocks(pass1, i, m_own)

        def pass2(j, accs):
            return tuple(
                accs[h] + jnp.dot(
                    vt_ref[h, :, j * BL:(j + 1) * BL],
                    jnp.exp2(s_ref[h, j] - ms[h]).astype(BF16),
                    preferred_element_type=F32)
                for h in range(H))

        init = (jnp.zeros((Dh + BF16_SUBLANES, BL), F32),) * H
        accs = over_blocks(pass2, i + 1, init)
        out_t = jnp.concatenate(
            [acc[:Dh] / acc[Dh:Dh + 1] for acc in accs], axis=0)
        o_ref[rows, :] = out_t.T.astype(o_ref.dtype)

    for j in range(nb):
        stage_keys(j)
        attend(j, slice(j * BL, (j + 1) * BL))


def _moba(q, k, v, tables, cast_ws, cast_layer, batch, seq):
    nb = seq // MOBA_BLOCK
    whole = pl.BlockSpec((seq, ATTN_WIDTH), lambda b, _: (b, 0))
    table = _const_spec((seq, ATTN_WIDTH))
    w_in_specs, w_out_specs, w_out_shapes = _weight_cast_specs(
        cast_layer, batch, lambda b, _: b)
    return pl.pallas_call(
        _with_weight_cast(_moba_body, 6, 1, len(w_in_specs)),
        grid=(batch, 1),
        in_specs=[whole, whole, whole, table, table, table, *w_in_specs],
        out_specs=[whole, *w_out_specs],
        out_shape=[jax.ShapeDtypeStruct((batch * seq, ATTN_WIDTH), BF16),
                   *w_out_shapes],
        scratch_shapes=[
            pltpu.VMEM((ATTN_HEADS, seq, ATTN_HEAD_DIM), BF16),
            pltpu.VMEM((ATTN_HEADS, ATTN_HEAD_DIM + BF16_SUBLANES, seq), BF16),
            pltpu.VMEM((ATTN_HEADS * nb, ATTN_WIDTH), F32),
            pltpu.VMEM((ATTN_HEADS, nb, MOBA_BLOCK), F32),
            pltpu.VMEM((ATTN_HEADS, nb, MOBA_BLOCK, MOBA_BLOCK), F32),
        ],
        compiler_params=_params("arbitrary", "arbitrary"),
        name="moba",
    )(q, k, v, *tables, *(cast_ws if w_in_specs else ()))


def _outproj_ffn_body(x_ref, ys_ref, yp_ref, ya_ref, w_ref, g_ref,
                      gpre_ref, wg_ref, wu_ref, wd_ref, gpost_ref, o_ref, h_ref):
    o1 = SSD_INNER
    o2 = SSD_INNER + POOL_WIDTH
    tiles = _sub_tiles(OUT_TOKEN_TILE)
    xs = []
    for rows in tiles:
        m = jnp.dot(ys_ref[rows, :], w_ref[0:o1, :],
                    preferred_element_type=F32)
        m = m + jnp.dot(yp_ref[rows, :], w_ref[o1:o2, :],
                        preferred_element_type=F32)
        m = m + jnp.dot(ya_ref[rows, :], w_ref[o2:, :],
                        preferred_element_type=F32)
        xs.append(x_ref[rows, :] + _rms(m) * g_ref[...])
    for rows, x in zip(tiles, xs):
        _ffn_gate_up(x, gpre_ref, wg_ref, wu_ref, h_ref.at[rows])
        o_ref[rows, :] = _ffn_down(x, wd_ref, gpost_ref, h_ref.at[rows])


def _outproj_ffn(x, y_ssd, y_pool, y_attn, w, g, gpre, wg, wu, wd, gpost, layer):
    m = x.shape[0]
    row = lambda i: (i, 0)
    return pl.pallas_call(
        _outproj_ffn_body,
        grid=(m // OUT_TOKEN_TILE,),
        in_specs=[
            pl.BlockSpec((OUT_TOKEN_TILE, D_MODEL), row),
            pl.BlockSpec((OUT_TOKEN_TILE, SSD_INNER), row),
            pl.BlockSpec((OUT_TOKEN_TILE, POOL_WIDTH), row),
            pl.BlockSpec((OUT_TOKEN_TILE, ATTN_WIDTH), row),
            _layer_spec((D_MODEL, D_MODEL), layer),
            _layer_spec((1, D_MODEL), layer),
            _layer_spec((1, D_MODEL), layer),
            _const_spec((D_MODEL, D_FF)),
            _const_spec((D_MODEL, D_FF)),
            _const_spec((D_FF, D_MODEL)),
            _layer_spec((1, D_MODEL), layer),
        ],
        out_specs=pl.BlockSpec((OUT_TOKEN_TILE, D_MODEL), row),
        out_shape=jax.ShapeDtypeStruct((m, D_MODEL), F32),
        scratch_shapes=[pltpu.VMEM((OUT_TOKEN_TILE, D_FF), BF16)],
        compiler_params=_params("parallel"),
        name="outproj_ffn",
    )(x, y_ssd, y_pool, y_attn, w, g, gpre, wg, wu, wd, gpost)


def _rope_tables(seq):
    half = ROPE_DIM // 2
    inv_freq = ROPE_THETA ** (-jnp.arange(0, ROPE_DIM, 2, dtype=F32) / ROPE_DIM)
    ang = jnp.arange(seq, dtype=F32)[:, None] * inv_freq[None, :]
    rest = ATTN_HEAD_DIM - ROPE_DIM
    one = jnp.ones((seq, rest), F32)
    zero = jnp.zeros((seq, rest), F32)
    zh = jnp.zeros((seq, half), F32)
    cos = jnp.concatenate([jnp.cos(ang), jnp.cos(ang), one], axis=-1)
    sin_up = jnp.concatenate([zh, jnp.sin(ang), zero], axis=-1)
    sin_dn = jnp.concatenate([-jnp.sin(ang), zh, zero], axis=-1)
    return tuple(jnp.tile(t, (1, ATTN_HEADS)) for t in (cos, sin_up, sin_dn))


def _split_w_in(w):
    pad = [(0, 0)] * (w.ndim - 1) + [(0, DT_PAD - SSD_HEADS)]
    pieces = [w[..., OFF_XBC:OFF_DT], w[..., OFF_POOL:OFF_Q], w[..., OFF_Z:OFF_XBC],
              w[..., OFF_Q:IN_COLS], jnp.pad(w[..., OFF_DT:OFF_POOL], pad)]
    return tuple(p.astype(BF16) for p in pieces)


def _per_head_lanes(p):
    return jnp.repeat(p, SSD_HEAD_DIM)[None, :]


def _per_head_cols(p):
    pad = [(0, 0)] * (p.ndim - 1) + [(0, DT_PAD - SSD_HEADS)]
    return jnp.pad(p, pad)[..., None, :]


def _pool_blockdiag(w):
    out = jnp.zeros(w.shape[:-3] + (POOL_WIDTH, POOL_WIDTH), F32)
    for g in range(POOL_GROUPS):
        sl = slice(g * POOL_GROUP_DIM, (g + 1) * POOL_GROUP_DIM)
        out = out.at[..., sl, sl].set(w[..., g, :, :])
    return out.astype(BF16)


def kernel(x, ff1_norm_pre, ff1_w_gate, ff1_w_up, ff1_w_down, ff1_norm_post,
           mix_norm_pre, w_in, conv_w, conv_b, dt_bias, a_log, d_skip, ssd_norm,
           pool_w, pool_scale, w_out, mix_norm_post,
           ff2_norm_pre, ff2_w_gate, ff2_w_up, ff2_w_down, ff2_norm_post):
    batch, seq, d = x.shape
    depth = w_in.shape[0]
    h = x.reshape(batch * seq, d)
    rope_tables = _rope_tables(seq)
    row = lambda p: p[None, :]
    stack_row = lambda p: p[:, None, :]
    bf = lambda w: w.astype(BF16)
    ff1_w = (ff1_w_gate, ff1_w_up, ff1_w_down)
    ff2_w = (ff2_w_gate, ff2_w_up, ff2_w_down)
    g_ff1 = (stack_row(ff1_norm_pre), stack_row(ff1_norm_post))
    g_ff2 = (stack_row(ff2_norm_pre), stack_row(ff2_norm_post))
    w_in_b = _split_w_in(w_in)
    w_out_b = bf(w_out)
    g_mix_pre = stack_row(mix_norm_pre)
    g_mix_post = stack_row(mix_norm_post)
    mix_front = (g_mix_pre, w_in_b, _pool_blockdiag(pool_w), stack_row(pool_scale))
    ff1_b = tuple(bf(w[0]) for w in ff1_w)
    for l in range(depth):
        h, xbc, y_pool, z, q, k, v, dt = _ffn_inproj(
            h, g_ff1[0], *ff1_b, g_ff1[1], *mix_front, l, seq)
        y_ssd, *ff2_b = _ssd(xbc, z, dt, conv_w[l], row(conv_b[l]),
                             _per_head_cols(dt_bias[l]), _per_head_cols(a_log[l]),
                             _per_head_lanes(d_skip[l]), row(ssd_norm[l]),
                             ff2_w, l, batch, seq)
        nxt = l + 1 if l + 1 < depth else None
        y_attn, *ff1_b = _moba(q, k, v, rope_tables, ff1_w, nxt, batch, seq)
        h = _outproj_ffn(h, y_ssd, y_pool, y_attn, w_out_b, g_mix_post,
                         g_ff2[0], *ff2_b, g_ff2[1], l)
    return h.reshape(batch, seq, d)
```

```python
import functools
import math

import jax
import jax.numpy as jnp
from jax import lax
from jax.experimental import pallas as pl
from jax.experimental.pallas import tpu as pltpu

F32 = jnp.float32
BF16 = jnp.bfloat16
HIGHEST = lax.Precision.HIGHEST

D_MODEL = 1024
D_FF = 2816
SSD_INNER = 512
SSD_HEAD_DIM = 64
SSD_HEADS = SSD_INNER // SSD_HEAD_DIM
SSD_GROUPS = 2
SSD_STATE = 128
SSD_CONV = 4
SSD_CHUNK = 128
SSD_BC = SSD_GROUPS * SSD_STATE
SSD_XBC = SSD_INNER + 2 * SSD_BC
POOL_WIDTH = 256
POOL_GROUPS = 4
POOL_GROUP_DIM = POOL_WIDTH // POOL_GROUPS
ATTN_WIDTH = 256
ATTN_HEAD_DIM = 64
ATTN_HEADS = ATTN_WIDTH // ATTN_HEAD_DIM
ROPE_DIM = ATTN_HEAD_DIM // 4
ROPE_THETA = 500000.0
MOBA_BLOCK = 256
MOBA_TOPK = 3
RMS_EPS = 1e-6

OFF_Z = 0
OFF_XBC = OFF_Z + SSD_INNER
OFF_DT = OFF_XBC + SSD_XBC
OFF_POOL = OFF_DT + SSD_HEADS
OFF_Q = OFF_POOL + POOL_WIDTH
OFF_K = OFF_Q + ATTN_WIDTH
OFF_V = OFF_K + ATTN_WIDTH
IN_COLS = OFF_V + ATTN_WIDTH

LANES = 128
SUBLANES = 8
MXU_DIM = 256
VMEM_LIMIT_BYTES = 56 * 1024 * 1024

DT_PAD = LANES
TOKEN_TILE = 512
OUT_TOKEN_TILE = 1024
SUB_TILE = 256
FF_CHUNK = MXU_DIM
SSD_STEP_ROWS = 8 * SSD_CHUNK
BF16_SUBLANES = 16
LOG2E = math.log2(math.e)


def _rms(x):
    return x * lax.rsqrt(jnp.mean(x * x, axis=-1, keepdims=True) + RMS_EPS)


def _silu(x):
    return x * jax.nn.sigmoid(x)


def _softplus(x):
    return jnp.maximum(x, 0.0) + jnp.log1p(jnp.exp(-jnp.abs(x)))


def _const_spec(shape):
    zeros = (0,) * len(shape)
    return pl.BlockSpec(shape, lambda *_: zeros, pipeline_mode=pl.Buffered(1))


def _layer_spec(shape, layer):
    zeros = (0,) * len(shape)
    return pl.BlockSpec((None,) + shape, lambda *_: (layer,) + zeros,
                        pipeline_mode=pl.Buffered(1))


def _params(*sem):
    return pltpu.CompilerParams(dimension_semantics=sem,
                                vmem_limit_bytes=VMEM_LIMIT_BYTES)


def _ffn_gate_up(x, gpre_ref, wg_ref, wu_ref, h_ref):
    xb = (_rms(x) * gpre_ref[...]).astype(BF16)
    for c in range(0, D_FF, FF_CHUNK):
        sl = slice(c, min(c + FF_CHUNK, D_FF))
        g = jnp.dot(xb, wg_ref[:, sl], preferred_element_type=F32)
        u = jnp.dot(xb, wu_ref[:, sl], preferred_element_type=F32)
        h_ref[:, sl] = (_silu(g) * u).astype(BF16)


def _ffn_down(x, wd_ref, gpost_ref, h_ref):
    f = jnp.dot(h_ref[...], wd_ref[...], preferred_element_type=F32)
    return x + 0.5 * (_rms(f) * gpost_ref[...])


def _sub_tiles(tile_rows):
    return [slice(r0, r0 + SUB_TILE) for r0 in range(0, tile_rows, SUB_TILE)]


_PROJ_WIDTHS = (SSD_XBC, POOL_WIDTH, SSD_INNER, 3 * ATTN_WIDTH, DT_PAD)
_PROJ_OUT_WIDTHS = (SSD_XBC, POOL_WIDTH, SSD_INNER, ATTN_WIDTH, ATTN_WIDTH,
                    ATTN_WIDTH, DT_PAD)
POOL_HALO = 2 ** POOL_GROUPS


def _pool_tile(u, tail, pos, w_ref, scale_ref):
    group = lax.broadcasted_iota(jnp.int32, u.shape, 1) // POOL_GROUP_DIM
    win_sum = jnp.concatenate([tail, u], axis=0)
    mean = jnp.zeros_like(u)
    for g in range(POOL_GROUPS):
        half = 2 ** g
        win_sum = win_sum + pltpu.roll(win_sum, half, 0)
        count = jnp.minimum(pos + 1, 2 * half).astype(F32)
        mean = jnp.where(group == g, win_sum[POOL_HALO:, :] / count, mean)
    d = (mean - u).astype(BF16)
    return jnp.dot(d, w_ref[...], preferred_element_type=F32) * scale_ref[...]


def _ffn_inproj_body(tiles_per_seq, x_ref, gpre_ref, wg_ref, wu_ref, wd_ref,
                     gpost_ref, g_ref, *rest):
    n_proj = len(_PROJ_WIDTHS)
    w_refs = rest[:n_proj]
    (poolw_ref, pscale_ref, o_ref, xbc_ref, yp_ref, z_ref, q_ref, k_ref, v_ref,
     dt_ref, h_ref, utail_ref) = rest[n_proj:]
    tile_in_seq = pl.program_id(0) % tiles_per_seq

    @pl.when(tile_in_seq == 0)
    def _():
        utail_ref[...] = jnp.zeros_like(utail_ref)

    tiles = _sub_tiles(TOKEN_TILE)

    def down(rows):
        x = _ffn_down(x_ref[rows, :], wd_ref, gpost_ref, h_ref.at[rows])
        o_ref[rows, :] = x
        return (_rms(x) * g_ref[...]).astype(BF16)

    def project(rows, xb, utail):
        def proj(i):
            return jnp.dot(xb, w_refs[i][...], preferred_element_type=F32)

        u = proj(1)
        xbc_ref[rows, :] = proj(0)
        z_ref[rows, :] = proj(2)
        qkv = proj(3)
        for i, ref in enumerate((q_ref, k_ref, v_ref)):
            ref[rows, :] = qkv[:, i * ATTN_WIDTH:(i + 1) * ATTN_WIDTH].astype(
                ref.dtype)
        dt_ref[rows, :] = proj(4)
        pos = (tile_in_seq * TOKEN_TILE + rows.start
               + lax.broadcasted_iota(jnp.int32, u.shape, 0))
        yp_ref[rows, :] = _pool_tile(
            u, utail, pos, poolw_ref, pscale_ref).astype(yp_ref.dtype)
        return u[SUB_TILE - POOL_HALO:, :]

    utail = utail_ref[...]
    pending = None
    for rows in tiles:
        _ffn_gate_up(x_ref[rows, :], gpre_ref, wg_ref, wu_ref, h_ref.at[rows])
        xb = down(rows)
        if pending is not None:
            utail = project(*pending, utail)
        pending = (rows, xb)
    utail_ref[...] = project(*pending, utail)


def _ffn_inproj(x, gpre, wg, wu, wd, gpost, g, ws, poolw, pscale, layer, seq):
    m = x.shape[0]
    row = lambda i: (i, 0)
    widths = (D_MODEL,) + _PROJ_OUT_WIDTHS
    return pl.pallas_call(
        functools.partial(_ffn_inproj_body, seq // TOKEN_TILE),
        grid=(m // TOKEN_TILE,),
        in_specs=[
            pl.BlockSpec((TOKEN_TILE, D_MODEL), row),
            _layer_spec((1, D_MODEL), layer),
            _const_spec((D_MODEL, D_FF)),
            _const_spec((D_MODEL, D_FF)),
            _const_spec((D_FF, D_MODEL)),
            _layer_spec((1, D_MODEL), layer),
            _layer_spec((1, D_MODEL), layer),
            *[_layer_spec((D_MODEL, width), layer) for width in _PROJ_WIDTHS],
            _layer_spec((POOL_WIDTH, POOL_WIDTH), layer),
            _layer_spec((1, POOL_WIDTH), layer),
        ],
        out_specs=[pl.BlockSpec((TOKEN_TILE, width), row) for width in widths],
        out_shape=[jax.ShapeDtypeStruct((m, width), BF16 if i in (2, 6) else F32)
                   for i, width in enumerate(widths)],
        scratch_shapes=[
            pltpu.VMEM((TOKEN_TILE, D_FF), BF16),
            pltpu.VMEM((POOL_HALO, POOL_WIDTH), F32),
        ],
        compiler_params=_params("arbitrary"),
        name="ffn_inproj",
    )(x, gpre, wg, wu, wd, gpost, g, *ws, poolw, pscale)


CAST_CHUNKS = 16
_FFN_W_SHAPES = ((D_MODEL, D_FF), (D_MODEL, D_FF), (D_FF, D_MODEL))


def _with_weight_cast(body, n_in, n_out, n_w):
    def wrapped(*refs):
        ins, refs = refs[:n_in], refs[n_in:]
        w_ins, refs = refs[:n_w], refs[n_w:]
        outs, refs = refs[:n_out], refs[n_out:]
        w_outs, scratch = refs[:n_w], refs[n_w:]
        body(*ins, *outs, *scratch)
        for w_in, w_out in zip(w_ins, w_outs):
            w_out[...] = w_in[...].astype(BF16)

    return wrapped


def _weight_cast_specs(layer, n_steps, step_of_grid):
    if layer is None:
        return [], [], []
    n_chunks = math.gcd(n_steps, CAST_CHUNKS)
    steps_per_chunk = n_steps // n_chunks
    chunk_of_step = lambda *g: step_of_grid(*g) // steps_per_chunk
    in_specs, out_specs, out_shapes = [], [], []
    for rows, cols in _FFN_W_SHAPES:
        chunk = rows // n_chunks
        in_specs.append(pl.BlockSpec(
            (None, chunk, cols), lambda *g: (layer, chunk_of_step(*g), 0)))
        out_specs.append(pl.BlockSpec(
            (chunk, cols), lambda *g: (chunk_of_step(*g), 0)))
        out_shapes.append(jax.ShapeDtypeStruct((rows, cols), BF16))
    return in_specs, out_specs, out_shapes


def _causal_conv4(x, tail, w_ref, b_ref):
    x_ext = jnp.concatenate([tail, x], axis=0)
    prev_ext = pltpu.roll(x_ext, 1, 0)
    near = w_ref[3:4, :] * x + w_ref[2:3, :] * prev_ext[SUBLANES:, :]
    far_ext = w_ref[1:2, :] * x_ext + w_ref[0:1, :] * prev_ext
    return near + pltpu.roll(far_ext, 2, 0)[SUBLANES:, :] + b_ref[...]


def _heads_to_lanes(xc):
    rows = xc.shape[0]
    lane = lax.broadcasted_iota(jnp.int32, (rows, LANES), 1)
    per_tile = LANES // SSD_HEAD_DIM
    parts = []
    for t in range(SSD_INNER // LANES):
        tile = jnp.broadcast_to(xc[:, t * per_tile:t * per_tile + 1], (rows, LANES))
        for i in range(1, per_tile):
            h = t * per_tile + i
            tile = jnp.where(lane < i * SSD_HEAD_DIM, tile,
                             jnp.broadcast_to(xc[:, h:h + 1], (rows, LANES)))
        parts.append(tile)
    return jnp.concatenate(parts, axis=1)


def _ssd_body(xbc_ref, z_ref, dt_ref, convw_ref, convb_ref, dtb_ref, alog_ref,
              dskip_ref, gn_ref, o_ref, tail_ref, state_ref):
    L = SSD_CHUNK
    HP = SSD_INNER
    P = SSD_HEAD_DIM
    N = SSD_STATE
    GW = HP // SSD_GROUPS
    HPG = SSD_HEADS // SSD_GROUPS

    @pl.when(pl.program_id(1) == 0)
    def _():
        tail_ref[...] = jnp.zeros_like(tail_ref)
        state_ref[...] = jnp.zeros_like(state_ref)

    r = lax.broadcasted_iota(jnp.int32, (L, L), 0)
    s = lax.broadcasted_iota(jnp.int32, (L, L), 1)
    causal = s <= r
    tril = causal.astype(F32)
    pair_lane = lax.broadcasted_iota(jnp.int32, (L, 2 * P), 1)
    neg_a = -jnp.exp(alog_ref[...])

    def chunk(c, tail):
        r0 = pl.multiple_of(c * L, L)
        rows = pl.ds(r0, L)
        x = xbc_ref[rows, :]
        xc = _silu(_causal_conv4(x, tail, convw_ref, convb_ref))
        xs = xc[:, :HP]
        bm = xc[:, HP:HP + SSD_BC].astype(BF16)
        cm = xc[:, HP + SSD_BC:].astype(BF16)

        dt_c = _softplus(dt_ref[rows, :] + dtb_ref[...])
        acs_c = jnp.dot(tril, dt_c * neg_a, precision=HIGHEST,
                        preferred_element_type=F32)
        acs_t = acs_c.T
        dt = _heads_to_lanes(dt_c)
        acs = _heads_to_lanes(acs_c)
        a_last = acs[L - 1:L, :]
        decay_out = jnp.exp(acs)
        decay_in = jnp.exp(a_last - acs)
        chunk_decay = jnp.exp(a_last)

        xdt = xs * dt
        xdt_b = xdt.astype(BF16)
        xw_b = (xdt * decay_in).astype(BF16)

        y_parts = []
        for g in range(SSD_GROUPS):
            bg = bm[:, g * N:(g + 1) * N]
            cg = cm[:, g * N:(g + 1) * N]
            gsl = slice(g * GW, (g + 1) * GW)
            cb = lax.dot_general(cg, bg, (((1,), (1,)), ((), ())),
                                 preferred_element_type=F32)
            st = state_ref[:, gsl]
            y_off = jnp.dot(cg, st.astype(BF16), preferred_element_type=F32)
            new_st = lax.dot_general(bg, xw_b[:, gsl], (((0,), (0,)), ((), ())),
                                     preferred_element_type=F32)
            state_ref[:, gsl] = st * chunk_decay[:, gsl] + new_st
            for pair in range(HPG // 2):
                lo = g * GW + pair * 2 * P
                x_pair = xdt_b[:, lo:lo + 2 * P]
                ys = []
                for i in range(2):
                    h = lo // P + i
                    seg = jnp.exp(jnp.where(
                        causal, acs_c[:, h:h + 1] - acs_t[h:h + 1, :], -jnp.inf))
                    ys.append(jnp.dot((cb * seg).astype(BF16), x_pair,
                                      preferred_element_type=F32))
                y_diag = jnp.where(pair_lane < P, ys[0], ys[1])
                psl = slice(pair * 2 * P, (pair + 1) * 2 * P)
                y_parts.append(
                    y_diag + y_off[:, psl] * decay_out[:, lo:lo + 2 * P])
        y = jnp.concatenate(y_parts, axis=-1) + dskip_ref[...] * xs
        o_ref[rows, :] = (_rms(y * _silu(z_ref[rows, :])) *
                          gn_ref[...]).astype(o_ref.dtype)
        return x[L - SUBLANES:, :]

    n_chunks = xbc_ref.shape[0] // L
    tail_ref[...] = lax.fori_loop(0, n_chunks, chunk, tail_ref[...], unroll=True)


def _ssd(xbc, z, dt, convw, convb, dtb, alog, dskip, gn, cast_ws, cast_layer,
         batch, seq):
    nblk = seq // SSD_STEP_ROWS
    row = lambda b, c: (b * nblk + c, 0)
    w_in_specs, w_out_specs, w_out_shapes = _weight_cast_specs(
        cast_layer, batch * nblk, lambda b, c: b * nblk + c)
    return pl.pallas_call(
        _with_weight_cast(_ssd_body, 9, 1, len(w_in_specs)),
        grid=(batch, nblk),
        in_specs=[
            pl.BlockSpec((SSD_STEP_ROWS, SSD_XBC), row),
            pl.BlockSpec((SSD_STEP_ROWS, SSD_INNER), row),
            pl.BlockSpec((SSD_STEP_ROWS, DT_PAD), row),
            _const_spec((SSD_CONV, SSD_XBC)),
            _const_spec((1, SSD_XBC)),
            _const_spec((1, DT_PAD)),
            _const_spec((1, DT_PAD)),
            _const_spec((1, SSD_INNER)),
            _const_spec((1, SSD_INNER)),
            *w_in_specs,
        ],
        out_specs=[pl.BlockSpec((SSD_STEP_ROWS, SSD_INNER), row), *w_out_specs],
        out_shape=[jax.ShapeDtypeStruct((batch * seq, SSD_INNER), BF16),
                   *w_out_shapes],
        scratch_shapes=[
            pltpu.VMEM((SUBLANES, SSD_XBC), F32),
            pltpu.VMEM((SSD_STATE, SSD_INNER), F32),
        ],
        compiler_params=_params("arbitrary", "arbitrary"),
        name="ssd",
    )(xbc, z, dt, convw, convb, dtb, alog, dskip, gn, *cast_ws)


def _rope(t, cos, sin_up, sin_dn):
    half = ROPE_DIM // 2
    return (t * cos + pltpu.roll(t, half, 1) * sin_up
            + pltpu.roll(t, ATTN_WIDTH - half, 1) * sin_dn)


def _moba_body(q_ref, k_ref, v_ref, cos_ref, sup_ref, sdn_ref, o_ref,
               ks_ref, vt_ref, kmean_ref, sel_ref, s_ref):
    seq = k_ref.shape[0]
    nb = seq // MOBA_BLOCK
    Dh = ATTN_HEAD_DIM
    BL = MOBA_BLOCK
    H = ATTN_HEADS
    nt = (((1,), (1,)), ((), ()))
    zero = pl.program_id(1)

    head_of_lane = lax.broadcasted_iota(jnp.int32, (1, ATTN_WIDTH), 1) // Dh

    def stage_keys(j):
        rows = slice(j * BL, (j + 1) * BL)
        kj = _rope(k_ref[rows, :], cos_ref[rows, :], sup_ref[rows, :],
                   sdn_ref[rows, :])
        kmean = jnp.mean(kj, axis=0, keepdims=True)
        for h in range(H):
            kmean_ref[h * nb + j:h * nb + j + 1, :] = jnp.where(
                head_of_lane == h, kmean, 0.0)
        kb = kj.astype(BF16)
        for h in range(H):
            ks_ref[h, rows, :] = kb[:, h * Dh:(h + 1) * Dh]
        vt = v_ref[rows, :].T
        for h in range(H):
            vt_ref[h, 0:Dh, rows] = vt[h * Dh:(h + 1) * Dh, :]

    vt_ref[:, Dh:, :] = jnp.ones((H, BF16_SUBLANES, seq), BF16)
    kmean_ref[...] = jnp.zeros_like(kmean_ref)

    def over_blocks(fn, n, carry):
        if n == 0:
            return carry

        def body(_, carry):
            for j in range(n):
                carry = fn(j, carry)
            return carry

        return lax.fori_loop(0, zero + 1, body, carry)

    def attend(i, rows):
        qf = _rope(q_ref[rows, :], cos_ref[rows, :], sup_ref[rows, :],
                   sdn_ref[rows, :])
        qb = (qf * (Dh ** -0.5)).astype(BF16)
        blk = lax.broadcasted_iota(jnp.int32, (nb, BL), 0)
        past = blk < i
        gates = lax.dot_general(kmean_ref[...], qf, nt, precision=HIGHEST,
                                preferred_element_type=F32)
        for h in range(H):
            gate = jnp.where(past, gates[h * nb:(h + 1) * nb, :], -jnp.inf)
            rank = jnp.zeros((nb, BL), jnp.int32)
            for j2 in range(nb):
                g2 = gate[j2:j2 + 1, :]
                ahead = (g2 > gate) | ((g2 == gate) & (j2 < blk))
                rank = rank + ahead.astype(jnp.int32)
            sel_ref[h] = (past & (rank < MOBA_TOPK)).astype(F32)

        qh = [qb[:, h * Dh:(h + 1) * Dh] for h in range(H)]

        def masked_scores(h, j, keep):
            st = lax.dot_general(ks_ref[h, j * BL:(j + 1) * BL, :], qh[h], nt,
                                 preferred_element_type=F32)
            st = jnp.where(keep, st * LOG2E, -jnp.inf)
            s_ref[h, j] = st
            return jnp.max(st, axis=0, keepdims=True)

        own_mask = (lax.broadcasted_iota(jnp.int32, (BL, BL), 0) <=
                    lax.broadcasted_iota(jnp.int32, (BL, BL), 1))
        m_own = tuple(masked_scores(h, i, own_mask) for h in range(H))

        def pass1(j, ms):
            return tuple(
                jnp.maximum(ms[h], masked_scores(
                    h, j, sel_ref[h, j:j + 1, :] > 0.5)) for h in range(H))

        ms = over_blocks(pass1, i, m_own)

        def pass2(j, accs):
            return tuple(
                accs[h] + jnp.dot(
                    vt_ref[h, :, j * BL:(j + 1) * BL],
                    jnp.exp2(s_ref[h, j] - ms[h]).astype(BF16),
                    preferred_element_type=F32)
                for h in range(H))

        init = (jnp.zeros((Dh + BF16_SUBLANES, BL), F32),) * H
        accs = over_blocks(pass2, i + 1, init)
        out_t = jnp.concatenate(
            [acc[:Dh] / acc[Dh:Dh + 1] for acc in accs], axis=0)
        o_ref[rows, :] = out_t.T.astype(o_ref.dtype)

    for j in range(nb):
        stage_keys(j)
        attend(j, slice(j * BL, (j + 1) * BL))


def _moba(q, k, v, tables, cast_ws, cast_layer, batch, seq):
    nb = seq // MOBA_BLOCK
    whole = pl.BlockSpec((seq, ATTN_WIDTH), lambda b, _: (b, 0))
    table = _const_spec((seq, ATTN_WIDTH))
    w_in_specs, w_out_specs, w_out_shapes = _weight_cast_specs(
        cast_layer, batch, lambda b, _: b)
    return pl.pallas_call(
        _with_weight_cast(_moba_body, 6, 1, len(w_in_specs)),
        grid=(batch, 1),
        in_specs=[whole, whole, whole, table, table, table, *w_in_specs],
        out_specs=[whole, *w_out_specs],
        out_shape=[jax.ShapeDtypeStruct((batch * seq, ATTN_WIDTH), BF16),
                   *w_out_shapes],
        scratch_shapes=[
            pltpu.VMEM((ATTN_HEADS, seq, ATTN_HEAD_DIM), BF16),
            pltpu.VMEM((ATTN_HEADS, ATTN_HEAD_DIM + BF16_SUBLANES, seq), BF16),
            pltpu.VMEM((ATTN_HEADS * nb, ATTN_WIDTH), F32),
            pltpu.VMEM((ATTN_HEADS, nb, MOBA_BLOCK), F32),
            pltpu.VMEM((ATTN_HEADS, nb, MOBA_BLOCK, MOBA_BLOCK), F32),
        ],
        compiler_params=_params("arbitrary", "arbitrary"),
        name="moba",
    )(q, k, v, *tables, *(cast_ws if w_in_specs else ()))


def _outproj_ffn_body(x_ref, ys_ref, yp_ref, ya_ref, w_ref, g_ref,
                      gpre_ref, wg_ref, wu_ref, wd_ref, gpost_ref, o_ref, h_ref):
    o1 = SSD_INNER
    o2 = SSD_INNER + POOL_WIDTH
    tiles = _sub_tiles(OUT_TOKEN_TILE)
    xs = []
    for rows in tiles:
        m = jnp.dot(ys_ref[rows, :], w_ref[0:o1, :],
                    preferred_element_type=F32)
        m = m + jnp.dot(yp_ref[rows, :], w_ref[o1:o2, :],
                        preferred_element_type=F32)
        m = m + jnp.dot(ya_ref[rows, :], w_ref[o2:, :],
                        preferred_element_type=F32)
        xs.append(x_ref[rows, :] + _rms(m) * g_ref[...])
    for rows, x in zip(tiles, xs):
        _ffn_gate_up(x, gpre_ref, wg_ref, wu_ref, h_ref.at[rows])
        o_ref[rows, :] = _ffn_down(x, wd_ref, gpost_ref, h_ref.at[rows])


def _outproj_ffn(x, y_ssd, y_pool, y_attn, w, g, gpre, wg, wu, wd, gpost, layer):
    m = x.shape[0]
    row = lambda i: (i, 0)
    return pl.pallas_call(
        _outproj_ffn_body,
        grid=(m // OUT_TOKEN_TILE,),
        in_specs=[
            pl.BlockSpec((OUT_TOKEN_TILE, D_MODEL), row),
            pl.BlockSpec((OUT_TOKEN_TILE, SSD_INNER), row),
            pl.BlockSpec((OUT_TOKEN_TILE, POOL_WIDTH), row),
            pl.BlockSpec((OUT_TOKEN_TILE, ATTN_WIDTH), row),
            _layer_spec((D_MODEL, D_MODEL), layer),
            _layer_spec((1, D_MODEL), layer),
            _layer_spec((1, D_MODEL), layer),
            _const_spec((D_MODEL, D_FF)),
            _const_spec((D_MODEL, D_FF)),
            _const_spec((D_FF, D_MODEL)),
            _layer_spec((1, D_MODEL), layer),
        ],
        out_specs=pl.BlockSpec((OUT_TOKEN_TILE, D_MODEL), row),
        out_shape=jax.ShapeDtypeStruct((m, D_MODEL), F32),
        scratch_shapes=[pltpu.VMEM((OUT_TOKEN_TILE, D_FF), BF16)],
        compiler_params=_params("parallel"),
        name="outproj_ffn",
    )(x, y_ssd, y_pool, y_attn, w, g, gpre, wg, wu, wd, gpost)


def _rope_tables(seq):
    half = ROPE_DIM // 2
    inv_freq = ROPE_THETA ** (-jnp.arange(0, ROPE_DIM, 2, dtype=F32) / ROPE_DIM)
    ang = jnp.arange(seq, dtype=F32)[:, None] * inv_freq[None, :]
    rest = ATTN_HEAD_DIM - ROPE_DIM
    one = jnp.ones((seq, rest), F32)
    zero = jnp.zeros((seq, rest), F32)
    zh = jnp.zeros((seq, half), F32)
    cos = jnp.concatenate([jnp.cos(ang), jnp.cos(ang), one], axis=-1)
    sin_up = jnp.concatenate([zh, jnp.sin(ang), zero], axis=-1)
    sin_dn = jnp.concatenate([-jnp.sin(ang), zh, zero], axis=-1)
    return tuple(jnp.tile(t, (1, ATTN_HEADS)) for t in (cos, sin_up, sin_dn))


def _split_w_in(w):
    pad = [(0, 0)] * (w.ndim - 1) + [(0, DT_PAD - SSD_HEADS)]
    pieces = [w[..., OFF_XBC:OFF_DT], w[..., OFF_POOL:OFF_Q], w[..., OFF_Z:OFF_XBC],
              w[..., OFF_Q:IN_COLS], jnp.pad(w[..., OFF_DT:OFF_POOL], pad)]
    return tuple(p.astype(BF16) for p in pieces)


def _per_head_lanes(p):
    return jnp.repeat(p, SSD_HEAD_DIM)[None, :]


def _per_head_cols(p):
    pad = [(0, 0)] * (p.ndim - 1) + [(0, DT_PAD - SSD_HEADS)]
    return jnp.pad(p, pad)[..., None, :]


def _pool_blockdiag(w):
    out = jnp.zeros(w.shape[:-3] + (POOL_WIDTH, POOL_WIDTH), F32)
    for g in range(POOL_GROUPS):
        sl = slice(g * POOL_GROUP_DIM, (g + 1) * POOL_GROUP_DIM)
        out = out.at[..., sl, sl].set(w[..., g, :, :])
    return out.astype(BF16)


def kernel(x, ff1_norm_pre, ff1_w_gate, ff1_w_up, ff1_w_down, ff1_norm_post,
           mix_norm_pre, w_in, conv_w, conv_b, dt_bias, a_log, d_skip, ssd_norm,
           pool_w, pool_scale, w_out, mix_norm_post,
           ff2_norm_pre, ff2_w_gate, ff2_w_up, ff2_w_down, ff2_norm_post):
    batch, seq, d = x.shape
    depth = w_in.shape[0]
    h = x.reshape(batch * seq, d)
    rope_tables = _rope_tables(seq)
    row = lambda p: p[None, :]
    stack_row = lambda p: p[:, None, :]
    bf = lambda w: w.astype(BF16)
    ff1_w = (ff1_w_gate, ff1_w_up, ff1_w_down)
    ff2_w = (ff2_w_gate, ff2_w_up, ff2_w_down)
    g_ff1 = (stack_row(ff1_norm_pre), stack_row(ff1_norm_post))
    g_ff2 = (stack_row(ff2_norm_pre), stack_row(ff2_norm_post))
    w_in_b = _split_w_in(w_in)
    w_out_b = bf(w_out)
    g_mix_pre = stack_row(mix_norm_pre)
    g_mix_post = stack_row(mix_norm_post)
    mix_front = (g_mix_pre, w_in_b, _pool_blockdiag(pool_w), stack_row(pool_scale))
    ff1_b = tuple(bf(w[0]) for w in ff1_w)
    for l in range(depth):
        h, xbc, y_pool, z, q, k, v, dt = _ffn_inproj(
            h, g_ff1[0], *ff1_b, g_ff1[1], *mix_front, l, seq)
        y_ssd, *ff2_b = _ssd(xbc, z, dt, conv_w[l], row(conv_b[l]),
                             _per_head_cols(dt_bias[l]), _per_head_cols(a_log[l]),
                             _per_head_lanes(d_skip[l]), row(ssd_norm[l]),
                             ff2_w, l, batch, seq)
        nxt = l + 1 if l + 1 < depth else None
        y_attn, *ff1_b = _moba(q, k, v, rope_tables, ff1_w, nxt, batch, seq)
        h = _outproj_ffn(h, y_ssd, y_pool, y_attn, w_out_b, g_mix_post,
                         g_ff2[0], *ff2_b, g_ff2[1], l)
    return h.reshape(batch, seq, d)
```
